```python
import math
import jax, jax.numpy as jnp
from jax import lax
import numpy as np

D_MODEL = 1024
BATCH = 1
SEQ = 16384
DEPTH = 1

D_MIX = D_MODEL
ATT_WIDTH = D_MIX // 2
HGRN_WIDTH = D_MIX - ATT_WIDTH
ATT_HEADS = 4
ATT_VDIM = ATT_WIDTH // ATT_HEADS
ATT_QKDIM = ATT_VDIM // 2
HGRN_HEADS = 4
HGRN_VDIM = HGRN_WIDTH // HGRN_HEADS
HGRN_EXPAND = 128
HGRN_FDIM = HGRN_HEADS * HGRN_EXPAND
N_GROUPS = 4
EXPERTS_PER_GROUP = 8
TOP_K = 2
D_EXPERT = 512
ROPE_THETA = 10000.0
NORM_EPS = 1e-6
SUBLN_EPS = 1e-5
Q_BLOCK = 128
CHUNK = 64
IN_COLS = 3 * ATT_WIDTH + 3 * HGRN_FDIM + 2 * HGRN_WIDTH

kernel_name = 'hymba_diffattn_hgrn2_hmoe_encoder'


def rmsnorm(x, g, eps=NORM_EPS):
    xf = x.astype(jnp.float32)
    y = xf * lax.rsqrt(jnp.mean(xf * xf, axis=-1, keepdims=True) + eps)
    return (y * g.astype(jnp.float32)).astype(x.dtype)


def rope_tables(seq, dim):
    inv = ROPE_THETA ** (-jnp.arange(0, dim, 2, dtype=jnp.float32) / dim)
    ang = jnp.arange(seq, dtype=jnp.float32)[:, None] * inv[None, :]
    return jnp.cos(ang), jnp.sin(ang)


def apply_rope(x, cos, sin):
    half = x.shape[-1] // 2
    xf = x.astype(jnp.float32)
    x1, x2 = xf[..., :half], xf[..., half:]
    c = cos[None, :, None, None, :]
    s = sin[None, :, None, None, :]
    return jnp.concatenate([x1 * c - x2 * s, x1 * s + x2 * c], axis=-1).astype(x.dtype)


def diff_attention(qa, ka, va, lam_p, subln_g, lambda_init):
    B, T = qa.shape[0], qa.shape[1]
    q = qa.reshape(B, T, ATT_HEADS, 2, ATT_QKDIM)
    k = ka.reshape(B, T, ATT_HEADS, 2, ATT_QKDIM)
    v = va.reshape(B, T, ATT_HEADS, ATT_VDIM)
    cos, sin = rope_tables(T, ATT_QKDIM)
    q = apply_rope(q, cos, sin)
    k = apply_rope(k, cos, sin)
    lp = lam_p.astype(jnp.float32)
    lam = jnp.exp(jnp.sum(lp[0] * lp[1])) - jnp.exp(jnp.sum(lp[2] * lp[3])) + lambda_init
    scale = ATT_QKDIM ** -0.5
    nb = T // Q_BLOCK
    qb = q.reshape(B, nb, Q_BLOCK, ATT_HEADS, 2, ATT_QKDIM).transpose(1, 0, 2, 3, 4, 5)

    def block(qblk):
        s = jnp.einsum('bqhcd,bkhcd->bhcqk', qblk, k,
                       preferred_element_type=jnp.float32) * scale
        p = jax.nn.softmax(s, axis=-1)
        a = p[:, :, 0] - lam * p[:, :, 1]
        return jnp.einsum('bhqk,bkhv->bqhv', a.astype(v.dtype), v)

    o = lax.map(block, qb)
    o = o.transpose(1, 0, 2, 3, 4).reshape(B, T, ATT_HEADS, ATT_VDIM)
    o = rmsnorm(o, subln_g, SUBLN_EPS) * (1.0 - lambda_init)
    return o.reshape(B, T, ATT_WIDTH)


def gla_chunk_scan(q, k, v, logf):
    B, T, H, K = q.shape
    V = v.shape[-1]
    nc = T // CHUNK

    def to_chunks(a):
        a = a.astype(jnp.float32)
        return a.reshape(B, nc, CHUNK, H, a.shape[-1]).transpose(1, 0, 3, 2, 4)

    mask = jnp.tril(jnp.ones((CHUNK, CHUNK), dtype=bool))

    def step(S, xs):
        qc, kc, vc, gc = xs
        b = jnp.cumsum(gc, axis=2)
        o_inter = jnp.einsum('bhtk,bhkv->bhtv', qc * jnp.exp(b), S)
        diff = b[:, :, :, None, :] - b[:, :, None, :, :]
        decay = jnp.exp(jnp.where(mask[:, :, None], diff, -jnp.inf))
        A = jnp.einsum('bhtk,bhsk,bhtsk->bhts', qc, kc, decay)
        o = o_inter + jnp.einsum('bhts,bhsv->bhtv', A, vc)
        b_last = b[:, :, -1:, :]
        S = jnp.exp(b_last[:, :, 0, :])[..., None] * S + jnp.einsum(
            'bhsk,bhsv->bhkv', kc * jnp.exp(b_last - b), vc)
        return S, o

    S0 = jnp.zeros((B, H, K, V), jnp.float32)
    xs = (to_chunks(q), to_chunks(k), to_chunks(v), to_chunks(logf))
    _, o = lax.scan(step, S0, xs)
    return o.transpose(1, 0, 3, 2, 4).reshape(B, T, H, V)


def hgrn2_bidir(hq, hf_f, hf_b, hi, hg, lb_logits, layer_idx, gnorm_g):
    B, T = hq.shape[0], hq.shape[1]
    lb = jnp.cumsum(jax.nn.softmax(lb_logits.astype(jnp.float32), axis=1), axis=1)[:, layer_idx]
    q = jax.nn.silu(hq.astype(jnp.float32)).reshape(B, T, HGRN_HEADS, HGRN_EXPAND)
    v = hi.reshape(B, T, HGRN_HEADS, HGRN_VDIM)

    def gates(z, lbd):
        z = z.astype(jnp.float32)
        f = lbd + (1.0 - lbd) * jax.nn.sigmoid(z)
        k = (1.0 - lbd) * jax.nn.sigmoid(-z)
        shp = (B, T, HGRN_HEADS, HGRN_EXPAND)
        return k.reshape(shp), jnp.log(f).reshape(shp)

    k_f, g_f = gates(hf_f, lb[0])
    k_b, g_b = gates(hf_b, lb[1])
    o_f = gla_chunk_scan(q, k_f, v, g_f)
    flip = lambda a: jnp.flip(a, axis=1)
    o_b = flip(gla_chunk_scan(flip(q), flip(k_b), flip(v), flip(g_b)))
    o = (o_f + o_b).astype(hi.dtype)
    gate = jax.nn.silu(hg.reshape(B, T, HGRN_HEADS, HGRN_VDIM))
    o = rmsnorm(o, gnorm_g) * gate
    return o.reshape(B, T, HGRN_WIDTH)


def hier_moe(h, w_gr, b_gr, w_er, b_er, w_gate, w_up, w_down):
    B, T, D = h.shape
    xt = h.reshape(B * T, D)
    g_prob = jax.nn.softmax(jnp.dot(xt, w_gr).astype(jnp.float32) + b_gr.astype(jnp.float32), axis=-1)
    p_top, g_top = lax.top_k(g_prob, 1)
    p_g, g_idx = p_top[:, 0], g_top[:, 0]
    e_logits = jnp.einsum('nd,gde->nge', xt, w_er).astype(jnp.float32) + b_er.astype(jnp.float32)
    e_sel = jnp.take_along_axis(e_logits, g_idx[:, None, None], axis=1)[:, 0]
    e_prob = jax.nn.softmax(e_sel, axis=-1)
    top_w, top_i = lax.top_k(e_prob, TOP_K)
    top_w = top_w / jnp.sum(top_w, axis=-1, keepdims=True)
    w_e = jnp.sum(jax.nn.one_hot(top_i, EXPERTS_PER_GROUP, dtype=jnp.float32) * top_w[..., None], axis=1)
    combine = (jax.nn.one_hot(g_idx, N_GROUPS, dtype=jnp.float32)[:, :, None]
               * w_e[:, None, :] * p_g[:, None, None]).astype(xt.dtype)
    y = jnp.zeros_like(xt)
    for gi in range(N_GROUPS):
        a = jnp.einsum('nd,edf->nef', xt, w_gate[gi])
        u = jnp.einsum('nd,edf->nef', xt, w_up[gi])
        hid = jax.nn.silu(a) * u * combine[:, gi, :, None]
        y = y + jnp.einsum('nef,efd->nd', hid, w_down[gi])
    return y.reshape(B, T, D)


def setup_inputs(seed: int = 0) -> dict:
    key = jax.random.key(seed)
    ks = jax.random.split(key, 20)
    f32 = jnp.float32
    nrm = lambda k, shp, s: jax.random.normal(k, shp, f32) * s
    L, D, G, E, F = DEPTH, D_MODEL, N_GROUPS, EXPERTS_PER_GROUP, D_EXPERT
    return {
        'x': nrm(ks[0], (BATCH, SEQ, D), 1.0),
        'w_in': nrm(ks[1], (L, D, IN_COLS), D ** -0.5),
        'w_out': nrm(ks[2], (L, D_MIX, D), D_MIX ** -0.5),
        'g_mix': 1.0 + nrm(ks[3], (L, D), 0.02),
        'lam_params': nrm(ks[4], (L, 4, ATT_QKDIM), 0.1),
        'subln_g': 1.0 + nrm(ks[5], (L, ATT_VDIM), 0.02),
        'hgrn_gnorm_g': 1.0 + nrm(ks[6], (L, HGRN_VDIM), 0.02),
        'hgrn_lb': nrm(ks[7], (2, L + 1, HGRN_FDIM), 0.1),
        'g_ffn': 1.0 + nrm(ks[8], (L, D), 0.02),
        'w_gr': nrm(ks[9], (L, D, G), D ** -0.5),
        'b_gr': nrm(ks[10], (L, G), 0.01),
        'w_er': nrm(ks[11], (L, G, D, E), D ** -0.5),
        'b_er': nrm(ks[12], (L, G, E), 0.01),
        'w_gate': nrm(ks[13], (L, G, E, D, F), D ** -0.5),
        'w_up': nrm(ks[14], (L, G, E, D, F), D ** -0.5),
        'w_down': nrm(ks[15], (L, G, E, F, D), F ** -0.5),
        'g_final': 1.0 + nrm(ks[16], (D,), 0.02),
    }


def reference(x, w_in, w_out, g_mix, lam_params, subln_g, hgrn_gnorm_g, hgrn_lb, g_ffn,
              w_gr, b_gr, w_er, b_er, w_gate, w_up, w_down, g_final):
    widths = [ATT_WIDTH, ATT_WIDTH, ATT_WIDTH, HGRN_FDIM, HGRN_FDIM, HGRN_FDIM, HGRN_WIDTH, HGRN_WIDTH]
    splits = [int(s) for s in np.cumsum(widths)[:-1]]
    h_res = x
    for l in range(DEPTH):
        h = rmsnorm(h_res, g_mix[l])
        proj = jnp.dot(h, w_in[l])
        qa, ka, va, hq, hf_f, hf_b, hi, hg = jnp.split(proj, splits, axis=-1)
        lambda_init = 0.8 - 0.6 * math.exp(-0.3 * l)
        att = diff_attention(qa, ka, va, lam_params[l], subln_g[l], lambda_init)
        rec = hgrn2_bidir(hq, hf_f, hf_b, hi, hg, hgrn_lb, l, hgrn_gnorm_g[l])
        mix = jnp.dot(jnp.concatenate([att, rec], axis=-1), w_out[l])
        h_res = h_res + mix
        h_res = h_res + hier_moe(rmsnorm(h_res, g_ffn[l]), w_gr[l], b_gr[l], w_er[l], b_er[l],
                                 w_gate[l], w_up[l], w_down[l])
    return rmsnorm(h_res, g_final)
```

```python
import functools
import math

import numpy as np
import jax
import jax.numpy as jnp
from jax import lax
from jax.experimental import pallas as pl
from jax.experimental.pallas import tpu as pltpu

D_MODEL = 1024
ATT_WIDTH = 512
ATT_HEADS = 4
ATT_VDIM = 128
ATT_QKDIM = 64
HGRN_WIDTH = 512
HGRN_HEADS = 4
HGRN_VDIM = 128
HGRN_EXPAND = 128
HGRN_FDIM = 512
N_GROUPS = 4
EXPERTS_PER_GROUP = 8
N_EXPERTS = N_GROUPS * EXPERTS_PER_GROUP
D_EXPERT = 512
ROPE_THETA = 10000.0
NORM_EPS = 1e-6
SUBLN_EPS = 1e-5
LAMBDA_INIT = 0.8 - 0.6 * math.exp(-0.3 * 0)
IN_COLS = 3 * ATT_WIDTH + 3 * HGRN_FDIM + 2 * HGRN_WIDTH
HGRN_COLS = IN_COLS - 3 * ATT_WIDTH

LANES = 128
VMEM_LIMIT = 56 * 1024 * 1024
HGRN_CHUNK = 64
ROUTER_LANES = LANES

BF16 = jnp.bfloat16
F32 = jnp.float32


def _cparams(sem):
    return pltpu.CompilerParams(dimension_semantics=sem, vmem_limit_bytes=VMEM_LIMIT)


def _rot_half_64(x):
    lane = lax.broadcasted_iota(jnp.int32, x.shape, 1)
    fwd = pltpu.roll(x, 32, axis=1)
    bwd = pltpu.roll(x, 96, axis=1)
    return jnp.where((lane & 63) < 32, bwd, fwd)


def _inproj_kernel(x_ref, g_ref, w_ref, cos_ref, sin_ref, qk_ref, v_ref, hg_ref):
    x = x_ref[...]
    ms = jnp.mean(x * x, axis=-1, keepdims=True)
    h = (x * lax.rsqrt(ms + NORM_EPS) * g_ref[...]).astype(BF16)
    cosv = cos_ref[...]
    sinv = sin_ref[...]
    scale = ATT_QKDIM ** -0.5
    for j in range(8):
        a = jnp.dot(h, w_ref[:, j * LANES:(j + 1) * LANES], preferred_element_type=F32)
        r = a * cosv + _rot_half_64(a) * sinv
        if j < 4:
            r = r * scale
        qk_ref[:, j * LANES:(j + 1) * LANES] = r.astype(BF16)
    v_ref[...] = jnp.dot(h, w_ref[:, 2 * ATT_WIDTH:3 * ATT_WIDTH],
                         preferred_element_type=F32).astype(BF16)
    for j in range(HGRN_COLS // 512):
        lo = 3 * ATT_WIDTH + j * 512
        hg_ref[:, j * 512:(j + 1) * 512] = jnp.dot(h, w_ref[:, lo:lo + 512],
                                                   preferred_element_type=F32)


def _in_proj(x2, g_mix, w_in_bf, cos_t, sin_t, tm=256):
    T = x2.shape[0]
    return pl.pallas_call(
        _inproj_kernel,
        grid=(T // tm,),
        in_specs=[
            pl.BlockSpec((tm, D_MODEL), lambda i: (i, 0)),
            pl.BlockSpec((1, D_MODEL), lambda i: (0, 0)),
            pl.BlockSpec((D_MODEL, IN_COLS), lambda i: (0, 0)),
            pl.BlockSpec((tm, LANES), lambda i: (i, 0)),
            pl.BlockSpec((tm, LANES), lambda i: (i, 0)),
        ],
        out_specs=[
            pl.BlockSpec((tm, 2 * ATT_WIDTH), lambda i: (i, 0)),
            pl.BlockSpec((tm, ATT_WIDTH), lambda i: (i, 0)),
            pl.BlockSpec((tm, HGRN_COLS), lambda i: (i, 0)),
        ],
        out_shape=[
            jax.ShapeDtypeStruct((T, 2 * ATT_WIDTH), BF16),
            jax.ShapeDtypeStruct((T, ATT_WIDTH), BF16),
            jax.ShapeDtypeStruct((T, HGRN_COLS), F32),
        ],
        compiler_params=_cparams(("parallel",)),
        name="in_proj",
    )(x2, g_mix, w_in_bf, cos_t, sin_t)


def _attn_kernel(lam_ref, q_ref, k_ref, v_ref, g_ref, o_ref, *, tk):
    T = k_ref.shape[0]
    q = q_ref[...]
    tq = q.shape[0]
    nt = (((1,), (1,)), ((), ()))
    qs = (q[:, :ATT_QKDIM], q[:, ATT_QKDIM:])

    def body(j, carry):
        off = pl.multiple_of(j * tk, tk)
        kc = k_ref[pl.ds(off, tk), :]
        vc = v_ref[pl.ds(off, tk), :]
        out = []
        for c in range(2):
            m, l, acc = carry[c]
            s = lax.dot_general(qs[c], kc[:, c * ATT_QKDIM:(c + 1) * ATT_QKDIM], nt,
                                preferred_element_type=F32)
            m_new = jnp.maximum(m, jnp.max(s, axis=-1, keepdims=True))
            alpha = jnp.exp(m - m_new)
            p = jnp.exp(s - m_new)
            l = alpha * l + jnp.sum(p, axis=-1, keepdims=True)
            acc = alpha * acc + jnp.dot(p.astype(BF16), vc, preferred_element_type=F32)
            out.append((m_new, l, acc))
        return tuple(out)

    init = tuple((jnp.full((tq, 1), -jnp.inf, F32), jnp.zeros((tq, 1), F32),
                  jnp.zeros((tq, ATT_VDIM), F32)) for _ in range(2))
    (m1, l1, a1), (m2, l2, a2) = lax.fori_loop(0, T // tk, body, init)
    lam = lam_ref[0, 0]
    o = a1 / l1 - lam * (a2 / l2)
    ms = jnp.mean(o * o, axis=-1, keepdims=True)
    o = o * lax.rsqrt(ms + SUBLN_EPS) * g_ref[...] * (1.0 - LAMBDA_INIT)
    o_ref[...] = o.astype(o_ref.dtype)


def _diff_attn(lam, qk, v, subln_g, tq=256, tk=512):
    T = qk.shape[0]
    tk = min(tk, T)
    return pl.pallas_call(
        functools.partial(_attn_kernel, tk=tk),
        grid=(ATT_HEADS, T // tq),
        in_specs=[
            pl.BlockSpec(memory_space=pltpu.SMEM),
            pl.BlockSpec((tq, LANES), lambda h, i: (i, h)),
            pl.BlockSpec((T, LANES), lambda h, i: (0, ATT_HEADS + h)),
            pl.BlockSpec((T, LANES), lambda h, i: (0, h)),
            pl.BlockSpec((1, ATT_VDIM), lambda h, i: (0, 0)),
        ],
        out_specs=pl.BlockSpec((tq, LANES), lambda h, i: (i, h)),
        out_shape=jax.ShapeDtypeStruct((T, ATT_WIDTH), BF16),
        compiler_params=_cparams(("parallel", "parallel")),
        name="diff_attn",
    )(lam, qk, qk, v, subln_g)


def _hgrn_consts(C, backward):
    halves = []
    h = C // 2
    while h >= 1:
        halves.append(h)
        h //= 2
    E = np.zeros((2 + len(halves), C, C), np.float32)
    M = np.zeros((len(halves) + 1, C, C), np.float32)
    idx = np.arange(C)
    for t in range(C):
        if not backward:
            E[0, t, idx <= t] = 1.0
            E[1, t, idx > t] = 1.0
        else:
            E[0, t, idx >= t] = 1.0
            E[1, t, idx < t] = 1.0
    for li, h in enumerate(halves):
        for t in range(C):
            mid = (t // (2 * h)) * 2 * h + h
            upper = t >= mid
            if not backward:
                if upper:
                    E[2 + li, t, (idx >= mid) & (idx <= t)] = 1.0
                else:
                    E[2 + li, t, (idx > t) & (idx <= mid - 1)] = 1.0
            else:
                if not upper:
                    E[2 + li, t, (idx >= t) & (idx <= mid - 1)] = 1.0
                else:
                    E[2 + li, t, (idx >= mid) & (idx <= t - 1)] = 1.0
            for s in range(C):
                same = (s // (2 * h)) == (t // (2 * h))
                s_upper = s >= mid
                if same and ((not backward and upper and not s_upper)
                             or (backward and not upper and s_upper)):
                    M[li, t, s] = 1.0
    M[-1] = np.eye(C, dtype=np.float32)
    E = np.concatenate([E.reshape(-1, C), np.ones((8, C), np.float32)], axis=0)
    return E, M


def _split3(g):
    hi = g.astype(BF16)
    r1 = g - hi.astype(F32)
    mid = r1.astype(BF16)
    lo = (r1 - mid.astype(F32)).astype(BF16)
    return hi, mid, lo


def _hgrn_kernel(*refs, backward, nchunk, final):
    if final:
        (hq_ref, hf_ref, hi_ref, lb_ref, e_ref, m_ref, ob_ref, hgate_ref, gn_ref,
         o_ref, st_ref) = refs
    else:
        hq_ref, hf_ref, hi_ref, lb_ref, e_ref, m_ref, o_ref, st_ref = refs
    C = HGRN_CHUNK
    nlev = m_ref.shape[0] - 1
    nt = (((1,), (1,)), ((), ()))

    @pl.when(pl.program_id(0) == 0)
    def _():
        st_ref[...] = jnp.zeros_like(st_ref)

    lbd = lb_ref[...]
    emat = e_ref[...]
    order = range(nchunk - 1, -1, -1) if backward else range(nchunk)
    for c in order:
        rows = pl.ds(c * C, C)
        z = hf_ref[rows, :]
        hq = hq_ref[rows, :]
        f = lbd + (1.0 - lbd) * jax.nn.sigmoid(z)
        kk = (1.0 - lbd) * jax.nn.sigmoid(-z)
        g = jnp.log(f)
        q = hq * jax.nn.sigmoid(hq)
        vb = hi_ref[rows, :].astype(BF16)
        ghi, gmid, glo = _split3(g)
        ex = (jnp.dot(emat, ghi, preferred_element_type=F32)
              + jnp.dot(emat, gmid, preferred_element_type=F32)
              + jnp.dot(emat, glo, preferred_element_type=F32))
        ee = jnp.exp(ex)
        qi = (q * ee[0:C]).astype(BF16)
        ki = (kk * ee[C:2 * C]).astype(BF16)
        dec = ee[(2 + nlev) * C:(2 + nlev) * C + 1]
        qb = q.astype(BF16)
        kb = kk.astype(BF16)
        qlev = [(q * ee[(2 + l) * C:(3 + l) * C]).astype(BF16) for l in range(nlev)]
        klev = [(kk * ee[(2 + l) * C:(3 + l) * C]).astype(BF16) for l in range(nlev)]
        outs = []
        for hd in range(HGRN_HEADS):
            ln = slice(hd * LANES, (hd + 1) * LANES)
            a = m_ref[nlev] * lax.dot_general(qb[:, ln], kb[:, ln], nt, preferred_element_type=F32)
            for l in range(nlev):
                a = a + m_ref[l] * lax.dot_general(qlev[l][:, ln], klev[l][:, ln], nt,
                                                   preferred_element_type=F32)
            st = st_ref[hd]
            o = jnp.dot(a.astype(BF16), vb[:, ln], preferred_element_type=F32)
            o = o + lax.dot_general(qi[:, ln], st.astype(BF16), nt, preferred_element_type=F32)
            vt = hi_ref[rows, ln].T.astype(BF16)
            st_ref[hd] = st * dec[:, ln] + jnp.dot(vt, ki[:, ln], preferred_element_type=F32)
            outs.append(o)
        o_all = jnp.concatenate(outs, axis=1)
        if final:
            o_all = o_all + ob_ref[rows, :]
            res = []
            for hd in range(HGRN_HEADS):
                ln = slice(hd * LANES, (hd + 1) * LANES)
                oh = o_all[:, ln]
                ms = jnp.mean(oh * oh, axis=-1, keepdims=True)
                res.append(oh * lax.rsqrt(ms + NORM_EPS) * gn_ref[...])
            hg = hgate_ref[rows, :]
            o_all = jnp.concatenate(res, axis=1) * (hg * jax.nn.sigmoid(hg))
        o_ref[rows, :] = o_all.astype(o_ref.dtype)


def _hgrn_pass(hgrn_in, lb_row, backward, o_b=None, gnorm_g=None, tb=256):
    T = hgrn_in.shape[0]
    nblk = T // tb
    final = o_b is not None
    E, M = _hgrn_consts(HGRN_CHUNK, backward)
    e_bf = jnp.asarray(E, BF16)
    m_f = jnp.asarray(M, F32)
    blk = (lambda i: nblk - 1 - i) if backward else (lambda i: i)
    f_part = 2 if backward else 1
    in_specs = [
        pl.BlockSpec((tb, 512), lambda i: (blk(i), 0)),
        pl.BlockSpec((tb, 512), lambda i: (blk(i), f_part)),
        pl.BlockSpec((tb, 512), lambda i: (blk(i), 3)),
        pl.BlockSpec((1, 512), lambda i: (0, 0)),
        pl.BlockSpec(e_bf.shape, lambda i: (0, 0)),
        pl.BlockSpec(m_f.shape, lambda i: (0, 0, 0)),
    ]
    args = [hgrn_in, hgrn_in, hgrn_in, lb_row, e_bf, m_f]
    if final:
        in_specs += [
            pl.BlockSpec((tb, 512), lambda i: (blk(i), 0)),
            pl.BlockSpec((tb, 512), lambda i: (blk(i), 4)),
            pl.BlockSpec((1, HGRN_VDIM), lambda i: (0, 0)),
        ]
        args += [o_b, hgrn_in, gnorm_g]
    return pl.pallas_call(
        functools.partial(_hgrn_kernel, backward=backward, nchunk=tb // HGRN_CHUNK, final=final),
        grid=(nblk,),
        in_specs=in_specs,
        out_specs=pl.BlockSpec((tb, 512), lambda i: (blk(i), 0)),
        out_shape=jax.ShapeDtypeStruct((T, HGRN_WIDTH), BF16 if final else F32),
        scratch_shapes=[pltpu.VMEM((HGRN_HEADS, HGRN_VDIM, HGRN_EXPAND), F32)],
        compiler_params=_cparams(("arbitrary",)),
        name="hgrn_bwd" if backward else "hgrn_fwd",
    )(*args)


def _outproj_kernel(att_ref, rec_ref, x_ref, wo_ref, g_ref, wr_hi_ref, wr_lo_ref, rb_ref,
                    hres_ref, xt_ref, comb_ref):
    mix = (jnp.dot(att_ref[...], wo_ref[0:ATT_WIDTH, :], preferred_element_type=F32)
           + jnp.dot(rec_ref[...], wo_ref[ATT_WIDTH:, :], preferred_element_type=F32))
    hres = x_ref[...] + mix
    hres_ref[...] = hres
    ms = jnp.mean(hres * hres, axis=-1, keepdims=True)
    xt = hres * lax.rsqrt(ms + NORM_EPS) * g_ref[...]
    xt_hi = xt.astype(BF16)
    xt_ref[...] = xt_hi
    xt_lo = (xt - xt_hi.astype(F32)).astype(BF16)
    logits = (jnp.dot(xt_hi, wr_hi_ref[...], preferred_element_type=F32)
              + jnp.dot(xt_hi, wr_lo_ref[...], preferred_element_type=F32)
              + jnp.dot(xt_lo, wr_hi_ref[...], preferred_element_type=F32)) + rb_ref[...]
    lane = lax.broadcasted_iota(jnp.int32, logits.shape, 1)
    neg = -jnp.inf
    big = jnp.int32(1 << 20)
    gmask = (lane >= N_EXPERTS) & (lane < N_EXPERTS + N_GROUPS)
    glog = jnp.where(gmask, logits, neg)
    gmax = jnp.max(glog, axis=-1, keepdims=True)
    gsum = jnp.sum(jnp.exp(glog - gmax), axis=-1, keepdims=True)
    p_g = 1.0 / gsum
    g_idx = jnp.min(jnp.where(glog == gmax, lane, big), axis=-1, keepdims=True) - N_EXPERTS
    emask = (lane < N_EXPERTS) & ((lane >> 3) == g_idx)
    elog = jnp.where(emask, logits, neg)
    e1 = jnp.max(elog, axis=-1, keepdims=True)
    i1 = jnp.min(jnp.where(elog == e1, lane, big), axis=-1, keepdims=True)
    elog2 = jnp.where(lane == i1, neg, elog)
    e2 = jnp.max(elog2, axis=-1, keepdims=True)
    i2 = jnp.min(jnp.where(elog2 == e2, lane, big), axis=-1, keepdims=True)
    r = jnp.exp(e2 - e1)
    w1 = p_g / (1.0 + r)
    w2 = p_g * r / (1.0 + r)
    comb_ref[...] = jnp.where(lane == i1, w1, jnp.where(lane == i2, w2, 0.0))


def _out_proj(att, rec, x2, w_out_bf, g_ffn, wr_hi, wr_lo, rbias, tm=512):
    T = x2.shape[0]
    row = lambda i: (i, 0)
    fixed = lambda i: (0, 0)
    return pl.pallas_call(
        _outproj_kernel,
        grid=(T // tm,),
        in_specs=[
            pl.BlockSpec((tm, ATT_WIDTH), row),
            pl.BlockSpec((tm, HGRN_WIDTH), row),
            pl.BlockSpec((tm, D_MODEL), row),
            pl.BlockSpec((D_MODEL, D_MODEL), fixed),
            pl.BlockSpec((1, D_MODEL), fixed),
            pl.BlockSpec((D_MODEL, ROUTER_LANES), fixed),
            pl.BlockSpec((D_MODEL, ROUTER_LANES), fixed),
            pl.BlockSpec((1, ROUTER_LANES), fixed),
        ],
        out_specs=[
            pl.BlockSpec((tm, D_MODEL), row),
            pl.BlockSpec((tm, D_MODEL), row),
            pl.BlockSpec((tm, ROUTER_LANES), row),
        ],
        out_shape=[
            jax.ShapeDtypeStruct((T, D_MODEL), F32),
            jax.ShapeDtypeStruct((T, D_MODEL), BF16),
            jax.ShapeDtypeStruct((T, ROUTER_LANES), F32),
        ],
        compiler_params=_cparams(("parallel",)),
        name="out_proj_router",
    )(att, rec, x2, w_out_bf, g_ffn, wr_hi, wr_lo, rbias)


def _moe_kernel(xt_ref, comb_ref, hres_ref, wg_ref, wu_ref, wd_ref, gf_ref, o_ref, acc_ref):
    e = pl.program_id(1)

    @pl.when(e == 0)
    def _():
        acc_ref[...] = jnp.zeros_like(acc_ref)

    xt = xt_ref[...]
    comb = comb_ref[...]
    lane = lax.broadcasted_iota(jnp.int32, comb.shape, 1)
    cw = jnp.sum(jnp.where(lane == e, comb, 0.0), axis=-1, keepdims=True)
    a = jnp.dot(xt, wg_ref[0], preferred_element_type=F32)
    u = jnp.dot(xt, wu_ref[0], preferred_element_type=F32)
    hid = (a * jax.nn.sigmoid(a)) * u * cw
    acc_ref[...] += jnp.dot(hid.astype(BF16), wd_ref[0], preferred_element_type=F32)

    @pl.when(e == pl.num_programs(1) - 1)
    def _():
        y = hres_ref[...] + acc_ref[...]
        ms = jnp.mean(y * y, axis=-1, keepdims=True)
        o_ref[...] = y * lax.rsqrt(ms + NORM_EPS) * gf_ref[...]


def _moe(xt, comb, hres, wg, wu, wd, g_final, tm=1024):
    T = xt.shape[0]
    tm = min(tm, T)
    row = lambda i, e: (i, 0)
    return pl.pallas_call(
        _moe_kernel,
        grid=(T // tm, N_EXPERTS),
        in_specs=[
            pl.BlockSpec((tm, D_MODEL), row),
            pl.BlockSpec((tm, ROUTER_LANES), row),
            pl.BlockSpec((tm, D_MODEL), row),
            pl.BlockSpec((1, D_MODEL, D_EXPERT), lambda i, e: (e, 0, 0)),
            pl.BlockSpec((1, D_MODEL, D_EXPERT), lambda i, e: (e, 0, 0)),
            pl.BlockSpec((1, D_EXPERT, D_MODEL), lambda i, e: (e, 0, 0)),
            pl.BlockSpec((1, D_MODEL), lambda i, e: (0, 0)),
        ],
        out_specs=pl.BlockSpec((tm, D_MODEL), row),
        out_shape=jax.ShapeDtypeStruct((T, D_MODEL), F32),
        scratch_shapes=[pltpu.VMEM((tm, D_MODEL), F32)],
        compiler_params=_cparams(("parallel", "arbitrary")),
        name="moe",
    )(xt, comb, hres, wg, wu, wd, g_final)


def _rope_tables(T):
    inv = ROPE_THETA ** (-jnp.arange(0, ATT_QKDIM, 2, dtype=F32) / ATT_QKDIM)
    ang = jnp.arange(T, dtype=F32)[:, None] * inv[None, :]
    c, s = jnp.cos(ang), jnp.sin(ang)
    return jnp.concatenate([c, c, c, c], axis=1), jnp.concatenate([-s, s, -s, s], axis=1)


def kernel(x, w_in, w_out, g_mix, lam_params, subln_g, hgrn_gnorm_g, hgrn_lb, g_ffn, w_gr, b_gr,
           w_er, b_er, w_gate, w_up, w_down, g_final):
    B, T, D = x.shape
    x2 = x.reshape(B * T, D)
    l = 0
    w_in_bf = w_in[l].astype(BF16)
    w_out_bf = w_out[l].astype(BF16)
    lp = lam_params[l].astype(F32)
    lam = (jnp.exp(jnp.sum(lp[0] * lp[1])) - jnp.exp(jnp.sum(lp[2] * lp[3])) + LAMBDA_INIT).reshape(1, 1)
    lb = jnp.cumsum(jax.nn.softmax(hgrn_lb.astype(F32), axis=1), axis=1)[:, l]
    cos_t, sin_t = _rope_tables(T)
    w_r = jnp.concatenate([jnp.transpose(w_er[l], (1, 0, 2)).reshape(D, N_EXPERTS), w_gr[l],
                           jnp.zeros((D, ROUTER_LANES - N_EXPERTS - N_GROUPS), F32)], axis=1)
    wr_hi = w_r.astype(BF16)
    wr_lo = (w_r - wr_hi.astype(F32)).astype(BF16)
    rbias = jnp.concatenate([b_er[l].reshape(-1), b_gr[l],
                             jnp.zeros((ROUTER_LANES - N_EXPERTS - N_GROUPS,), F32)]).reshape(1, -1)
    wg = w_gate[l].reshape(N_EXPERTS, D, D_EXPERT).astype(BF16)
    wu = w_up[l].reshape(N_EXPERTS, D, D_EXPERT).astype(BF16)
    wd = w_down[l].reshape(N_EXPERTS, D_EXPERT, D).astype(BF16)

    qk, v, hgrn_in = _in_proj(x2, g_mix[l].reshape(1, D), w_in_bf, cos_t, sin_t)
    att = _diff_attn(lam, qk, v, subln_g[l].reshape(1, -1))
    o_b = _hgrn_pass(hgrn_in, lb[1:2], backward=True)
    rec = _hgrn_pass(hgrn_in, lb[0:1], backward=False, o_b=o_b,
                     gnorm_g=hgrn_gnorm_g[l].reshape(1, -1))
    hres, xt, comb = _out_proj(att, rec, x2, w_out_bf, g_ffn[l].reshape(1, D), wr_hi, wr_lo, rbias)
    out = _moe(xt, comb, hres, wg, wu, wd, g_final.reshape(1, D))
    return out.reshape(B, T, D)
```

```python
import functools
import math

import numpy as np
import jax
import jax.numpy as jnp
from jax import lax
from jax.experimental import pallas as pl
from jax.experimental.pallas import tpu as pltpu

D_MODEL = 1024
ATT_WIDTH = 512
ATT_HEADS = 4
ATT_VDIM = 128
ATT_QKDIM = 64
HGRN_WIDTH = 512
HGRN_HEADS = 4
HGRN_VDIM = 128
HGRN_EXPAND = 128
HGRN_FDIM = 512
N_GROUPS = 4
EXPERTS_PER_GROUP = 8
N_EXPERTS = N_GROUPS * EXPERTS_PER_GROUP
D_EXPERT = 512
ROPE_THETA = 10000.0
NORM_EPS = 1e-6
SUBLN_EPS = 1e-5
LAMBDA_INIT = 0.8 - 0.6 * math.exp(-0.3 * 0)
IN_COLS = 3 * ATT_WIDTH + 3 * HGRN_FDIM + 2 * HGRN_WIDTH
HGRN_COLS = IN_COLS - 3 * ATT_WIDTH

LANES = 128
VMEM_LIMIT = 56 * 1024 * 1024
HGRN_CHUNK = 64
ATT_TK = 256
ATT_CB = 256
ATT_VPAD = ATT_VDIM + 16
ROUTER_LANES = LANES

BF16 = jnp.bfloat16
F32 = jnp.float32


def _cparams(sem):
    return pltpu.CompilerParams(dimension_semantics=sem, vmem_limit_bytes=VMEM_LIMIT)


def _rot_half_64(x):
    lane = lax.broadcasted_iota(jnp.int32, x.shape, 1)
    fwd = pltpu.roll(x, 32, axis=1)
    bwd = pltpu.roll(x, 96, axis=1)
    return jnp.where((lane & 63) < 32, bwd, fwd)


def _inproj_kernel(x_ref, g_ref, w_ref, wqvt_ref, cos_ref, sin_ref, cost_ref, sint_ref,
                   qt_ref, k_ref, vt_ref, hg_ref):
    x = x_ref[...]
    tm = x.shape[0]
    ms = jnp.mean(x * x, axis=-1, keepdims=True)
    h = (x * lax.rsqrt(ms + NORM_EPS) * g_ref[...]).astype(BF16)
    nt = (((1,), (1,)), ((), ()))
    half = ATT_QKDIM // 2
    ct = cost_ref[...] * ATT_QKDIM ** -0.5
    st = sint_ref[...] * ATT_QKDIM ** -0.5
    for j in range(ATT_WIDTH // LANES):
        a = lax.dot_general(wqvt_ref[j * LANES:(j + 1) * LANES, :], h, nt,
                            preferred_element_type=F32)
        for c in range(LANES // ATT_QKDIM):
            x1 = a[c * ATT_QKDIM:c * ATT_QKDIM + half]
            x2 = a[c * ATT_QKDIM + half:(c + 1) * ATT_QKDIM]
            lo = j * LANES + c * ATT_QKDIM
            qt_ref[lo:lo + half, :] = (x1 * ct - x2 * st).astype(BF16)
            qt_ref[lo + half:lo + ATT_QKDIM, :] = (x1 * st + x2 * ct).astype(BF16)
    ones = jnp.ones((ATT_VPAD - ATT_VDIM, tm), BF16)
    for j in range(ATT_HEADS):
        lo = ATT_WIDTH + j * ATT_VDIM
        vt = lax.dot_general(wqvt_ref[lo:lo + ATT_VDIM, :], h, nt, preferred_element_type=F32)
        vt_ref[0, j * ATT_VPAD:j * ATT_VPAD + ATT_VDIM, :] = vt.astype(BF16)
        vt_ref[0, j * ATT_VPAD + ATT_VDIM:(j + 1) * ATT_VPAD, :] = ones
    cosv = cos_ref[...]
    sinv = sin_ref[...]
    for j in range(ATT_WIDTH // LANES):
        lo = ATT_WIDTH + j * LANES
        a = jnp.dot(h, w_ref[:, lo:lo + LANES], preferred_element_type=F32)
        k_ref[:, j * LANES:(j + 1) * LANES] = (a * cosv + _rot_half_64(a) * sinv).astype(BF16)
    for j in range(HGRN_COLS // 512):
        lo = 3 * ATT_WIDTH + j * 512
        hg_ref[:, j * 512:(j + 1) * 512] = jnp.dot(h, w_ref[:, lo:lo + 512],
                                                   preferred_element_type=F32)


def _in_proj(x2, g_mix, w_in_bf, w_qv_t, cos_t, sin_t, cos_tt, sin_tt):
    T = x2.shape[0]
    tm = ATT_TK
    return pl.pallas_call(
        _inproj_kernel,
        grid=(T // tm,),
        in_specs=[
            pl.BlockSpec((tm, D_MODEL), lambda i: (i, 0)),
            pl.BlockSpec((1, D_MODEL), lambda i: (0, 0)),
            pl.BlockSpec((D_MODEL, IN_COLS), lambda i: (0, 0)),
            pl.BlockSpec((2 * ATT_WIDTH, D_MODEL), lambda i: (0, 0)),
            pl.BlockSpec((tm, LANES), lambda i: (i, 0)),
            pl.BlockSpec((tm, LANES), lambda i: (i, 0)),
            pl.BlockSpec((ATT_QKDIM // 2, tm), lambda i: (0, i)),
            pl.BlockSpec((ATT_QKDIM // 2, tm), lambda i: (0, i)),
        ],
        out_specs=[
            pl.BlockSpec((ATT_WIDTH, tm), lambda i: (0, i)),
            pl.BlockSpec((tm, ATT_WIDTH), lambda i: (i, 0)),
            pl.BlockSpec((1, ATT_HEADS * ATT_VPAD, tm), lambda i: (i, 0, 0)),
            pl.BlockSpec((tm, HGRN_COLS), lambda i: (i, 0)),
        ],
        out_shape=[
            jax.ShapeDtypeStruct((ATT_WIDTH, T), BF16),
            jax.ShapeDtypeStruct((T, ATT_WIDTH), BF16),
            jax.ShapeDtypeStruct((T // tm, ATT_HEADS * ATT_VPAD, tm), BF16),
            jax.ShapeDtypeStruct((T, HGRN_COLS), F32),
        ],
        compiler_params=_cparams(("parallel",)),
        name="in_proj",
    )(x2, g_mix, w_in_bf, w_qv_t, cos_t, sin_t, cos_tt, sin_tt)


def _attn_kernel(lam_ref, qt_ref, k_ref, vt_ref, g_ref, o_ref, qw_ref, sa_ref, sb_ref, acc_ref):
    tq = qt_ref.shape[1]
    nchunk, _, tk = vt_ref.shape
    qt = qt_ref[...].astype(F32)
    row = lax.broadcasted_iota(jnp.int32, qt.shape, 0)
    qw_ref[:, 0:tq] = jnp.where(row < ATT_QKDIM, qt, 0.0).astype(BF16)
    qw_ref[:, tq:] = jnp.where(row >= ATT_QKDIM, qt, 0.0).astype(BF16)
    acc_ref[...] = jnp.zeros_like(acc_ref)
    ncb = 2 * tq // ATT_CB

    def scores(j, s_ref, m_old):
        off = pl.multiple_of(j * tk, tk)
        kc = k_ref[pl.ds(off, tk), :]
        m_new, alpha = [], []
        for cb in range(ncb):
            cols = slice(cb * ATT_CB, (cb + 1) * ATT_CB)
            s = jnp.dot(kc, qw_ref[:, cols], preferred_element_type=F32)
            s_ref[:, cols] = s
            mo = m_old[cb]
            mn = jnp.maximum(mo, jnp.max(s, axis=0, keepdims=True))
            m_new.append(mn)
            alpha.append(jnp.exp(mo - mn))
        return tuple(m_new), tuple(alpha)

    def accumulate(j, s_ref, m_cur, alpha):
        vc = vt_ref[j]
        for cb in range(ncb):
            cols = slice(cb * ATT_CB, (cb + 1) * ATT_CB)
            p = jnp.exp(s_ref[:, cols] - m_cur[cb]).astype(BF16)
            acc_ref[:, cols] = (alpha[cb] * acc_ref[:, cols]
                                + jnp.dot(vc, p, preferred_element_type=F32))

    m0 = tuple(jnp.full((1, ATT_CB), -jnp.inf, F32) for _ in range(ncb))
    state = scores(0, sa_ref, m0)

    def body(i, st):
        m_a, al_a = st
        j = 2 * i
        accumulate(j, sa_ref, m_a, al_a)
        m_b, al_b = scores(j + 1, sb_ref, m_a)
        accumulate(j + 1, sb_ref, m_b, al_b)
        return scores(jnp.minimum(j + 2, nchunk - 1), sa_ref, m_b)

    lax.fori_loop(0, nchunk // 2, body, state)
    lam = lam_ref[0, 0]
    acc = acc_ref[...]
    o1 = acc[0:ATT_VDIM, 0:tq] / acc[ATT_VDIM:ATT_VDIM + 1, 0:tq]
    o2 = acc[0:ATT_VDIM, tq:] / acc[ATT_VDIM:ATT_VDIM + 1, tq:]
    o = o1 - lam * o2
    ms = jnp.mean(o * o, axis=0, keepdims=True)
    o = o * lax.rsqrt(ms + SUBLN_EPS) * g_ref[...] * (1.0 - LAMBDA_INIT)
    o_ref[...] = o.T.astype(o_ref.dtype)


def _diff_attn(lam, qt, k, vt, subln_g, tq=512):
    T = k.shape[0]
    nchunk, _, tk = vt.shape
    return pl.pallas_call(
        _attn_kernel,
        grid=(ATT_HEADS, T // tq),
        in_specs=[
            pl.BlockSpec(memory_space=pltpu.SMEM),
            pl.BlockSpec((LANES, tq), lambda h, i: (h, i)),
            pl.BlockSpec((T, LANES), lambda h, i: (0, h)),
            pl.BlockSpec((nchunk, ATT_VPAD, tk), lambda h, i: (0, h, 0)),
            pl.BlockSpec((ATT_VDIM, 1), lambda h, i: (0, 0)),
        ],
        out_specs=pl.BlockSpec((tq, LANES), lambda h, i: (i, h)),
        out_shape=jax.ShapeDtypeStruct((T, ATT_WIDTH), BF16),
        scratch_shapes=[
            pltpu.VMEM((LANES, 2 * tq), BF16),
            pltpu.VMEM((tk, 2 * tq), F32),
            pltpu.VMEM((tk, 2 * tq), F32),
            pltpu.VMEM((ATT_VPAD, 2 * tq), F32),
        ],
        compiler_params=_cparams(("parallel", "parallel")),
        name="diff_attn",
    )(lam, qt, k, vt, subln_g)


def _hgrn_consts(C, backward):
    halves = []
    h = C // 2
    while h >= 1:
        halves.append(h)
        h //= 2
    E = np.zeros((2 + len(halves), C, C), np.float32)
    M = np.zeros((len(halves) + 1, C, C), np.float32)
    idx = np.arange(C)
    for t in range(C):
        if not backward:
            E[0, t, idx <= t] = 1.0
            E[1, t, idx > t] = 1.0
        else:
            E[0, t, idx >= t] = 1.0
            E[1, t, idx < t] = 1.0
    for li, h in enumerate(halves):
        for t in range(C):
            mid = (t // (2 * h)) * 2 * h + h
            upper = t >= mid
            if not backward:
                if upper:
                    E[2 + li, t, (idx >= mid) & (idx <= t)] = 1.0
                else:
                    E[2 + li, t, (idx > t) & (idx <= mid - 1)] = 1.0
            else:
                if not upper:
                    E[2 + li, t, (idx >= t) & (idx <= mid - 1)] = 1.0
                else:
                    E[2 + li, t, (idx >= mid) & (idx <= t - 1)] = 1.0
            for s in range(C):
                same = (s // (2 * h)) == (t // (2 * h))
                s_upper = s >= mid
                if same and ((not backward and upper and not s_upper)
                             or (backward and not upper and s_upper)):
                    M[li, t, s] = 1.0
    M[-1] = np.eye(C, dtype=np.float32)
    E = np.concatenate([E.reshape(-1, C), np.ones((8, C), np.float32)], axis=0)
    return E, M


def _split3(g):
    hi = g.astype(BF16)
    r1 = g - hi.astype(F32)
    mid = r1.astype(BF16)
    lo = (r1 - mid.astype(F32)).astype(BF16)
    return hi, mid, lo


def _hgrn_kernel(*refs, backward, nchunk, final):
    if final:
        (hq_ref, hf_ref, hi_ref, lb_ref, e_ref, m_ref, ob_ref, hgate_ref, gn_ref,
         o_ref, st_ref) = refs
    else:
        hq_ref, hf_ref, hi_ref, lb_ref, e_ref, m_ref, o_ref, st_ref = refs
    C = HGRN_CHUNK
    nlev = m_ref.shape[0] - 1
    nt = (((1,), (1,)), ((), ()))

    @pl.when(pl.program_id(0) == 0)
    def _():
        st_ref[...] = jnp.zeros_like(st_ref)

    lbd = lb_ref[...]
    emat = e_ref[...]
    order = range(nchunk - 1, -1, -1) if backward else range(nchunk)
    for c in order:
        rows = pl.ds(c * C, C)
        z = hf_ref[rows, :]
        hq = hq_ref[rows, :]
        f = lbd + (1.0 - lbd) * jax.nn.sigmoid(z)
        kk = (1.0 - lbd) * jax.nn.sigmoid(-z)
        g = jnp.log(f)
        q = hq * jax.nn.sigmoid(hq)
        vb = hi_ref[rows, :].astype(BF16)
        ghi, gmid, glo = _split3(g)
        ex = (jnp.dot(emat, ghi, preferred_element_type=F32)
              + jnp.dot(emat, gmid, preferred_element_type=F32)
              + jnp.dot(emat, glo, preferred_element_type=F32))
        ee = jnp.exp(ex)
        qi = (q * ee[0:C]).astype(BF16)
        ki = (kk * ee[C:2 * C]).astype(BF16)
        dec = ee[(2 + nlev) * C:(2 + nlev) * C + 1]
        qb = q.astype(BF16)
        kb = kk.astype(BF16)
        qlev = [(q * ee[(2 + l) * C:(3 + l) * C]).astype(BF16) for l in range(nlev)]
        klev = [(kk * ee[(2 + l) * C:(3 + l) * C]).astype(BF16) for l in range(nlev)]
        outs = []
        for hd in range(HGRN_HEADS):
            ln = slice(hd * LANES, (hd + 1) * LANES)
            a = m_ref[nlev] * lax.dot_general(qb[:, ln], kb[:, ln], nt, preferred_element_type=F32)
            for l in range(nlev):
                a = a + m_ref[l] * lax.dot_general(qlev[l][:, ln], klev[l][:, ln], nt,
                                                   preferred_element_type=F32)
            st = st_ref[hd]
            o = jnp.dot(a.astype(BF16), vb[:, ln], preferred_element_type=F32)
            o = o + lax.dot_general(qi[:, ln], st.astype(BF16), nt, preferred_element_type=F32)
            vt = hi_ref[rows, ln].T.astype(BF16)
            st_ref[hd] = st * dec[:, ln] + jnp.dot(vt, ki[:, ln], preferred_element_type=F32)
            outs.append(o)
        o_all = jnp.concatenate(outs, axis=1)
        if final:
            o_all = o_all + ob_ref[rows, :]
            res = []
            for hd in range(HGRN_HEADS):
                ln = slice(hd * LANES, (hd + 1) * LANES)
                oh = o_all[:, ln]
                ms = jnp.mean(oh * oh, axis=-1, keepdims=True)
                res.append(oh * lax.rsqrt(ms + NORM_EPS) * gn_ref[...])
            hg = hgate_ref[rows, :]
            o_all = jnp.concatenate(res, axis=1) * (hg * jax.nn.sigmoid(hg))
        o_ref[rows, :] = o_all.astype(o_ref.dtype)


def _hgrn_pass(hgrn_in, lb_row, backward, o_b=None, gnorm_g=None, tb=256):
    T = hgrn_in.shape[0]
    nblk = T // tb
    final = o_b is not None
    E, M = _hgrn_consts(HGRN_CHUNK, backward)
    e_bf = jnp.asarray(E, BF16)
    m_f = jnp.asarray(M, F32)
    blk = (lambda i: nblk - 1 - i) if backward else (lambda i: i)
    f_part = 2 if backward else 1
    in_specs = [
        pl.BlockSpec((tb, 512), lambda i: (blk(i), 0)),
        pl.BlockSpec((tb, 512), lambda i: (blk(i), f_part)),
        pl.BlockSpec((tb, 512), lambda i: (blk(i), 3)),
        pl.BlockSpec((1, 512), lambda i: (0, 0)),
        pl.BlockSpec(e_bf.shape, lambda i: (0, 0)),
        pl.BlockSpec(m_f.shape, lambda i: (0, 0, 0)),
    ]
    args = [hgrn_in, hgrn_in, hgrn_in, lb_row, e_bf, m_f]
    if final:
        in_specs += [
            pl.BlockSpec((tb, 512), lambda i: (blk(i), 0)),
            pl.BlockSpec((tb, 512), lambda i: (blk(i), 4)),
            pl.BlockSpec((1, HGRN_VDIM), lambda i: (0, 0)),
        ]
        args += [o_b, hgrn_in, gnorm_g]
    return pl.pallas_call(
        functools.partial(_hgrn_kernel, backward=backward, nchunk=tb // HGRN_CHUNK, final=final),
        grid=(nblk,),
        in_specs=in_specs,
        out_specs=pl.BlockSpec((tb, 512), lambda i: (blk(i), 0)),
        out_shape=jax.ShapeDtypeStruct((T, HGRN_WIDTH), BF16 if final else F32),
        scratch_shapes=[pltpu.VMEM((HGRN_HEADS, HGRN_VDIM, HGRN_EXPAND), F32)],
        compiler_params=_cparams(("arbitrary",)),
        name="hgrn_bwd" if backward else "hgrn_fwd",
    )(*args)


def _outproj_kernel(att_ref, rec_ref, x_ref, wo_ref, g_ref, wr_hi_ref, wr_lo_ref, rb_ref,
                    hres_ref, xt_ref, comb_ref):
    mix = (jnp.dot(att_ref[...], wo_ref[0:ATT_WIDTH, :], preferred_element_type=F32)
           + jnp.dot(rec_ref[...], wo_ref[ATT_WIDTH:, :], preferred_element_type=F32))
    hres = x_ref[...] + mix
    hres_ref[...] = hres
    ms = jnp.mean(hres * hres, axis=-1, keepdims=True)
    xt = hres * lax.rsqrt(ms + NORM_EPS) * g_ref[...]
    xt_hi = xt.astype(BF16)
    xt_ref[...] = xt_hi
    xt_lo = (xt - xt_hi.astype(F32)).astype(BF16)
    logits = (jnp.dot(xt_hi, wr_hi_ref[...], preferred_element_type=F32)
              + jnp.dot(xt_hi, wr_lo_ref[...], preferred_element_type=F32)
              + jnp.dot(xt_lo, wr_hi_ref[...], preferred_element_type=F32)) + rb_ref[...]
    lane = lax.broadcasted_iota(jnp.int32, logits.shape, 1)
    neg = -jnp.inf
    big = jnp.int32(1 << 20)
    gmask = (lane >= N_EXPERTS) & (lane < N_EXPERTS + N_GROUPS)
    glog = jnp.where(gmask, logits, neg)
    gmax = jnp.max(glog, axis=-1, keepdims=True)
    gsum = jnp.sum(jnp.exp(glog - gmax), axis=-1, keepdims=True)
    p_g = 1.0 / gsum
    g_idx = jnp.min(jnp.where(glog == gmax, lane, big), axis=-1, keepdims=True) - N_EXPERTS
    emask = (lane < N_EXPERTS) & ((lane >> 3) == g_idx)
    elog = jnp.where(emask, logits, neg)
    e1 = jnp.max(elog, axis=-1, keepdims=True)
    i1 = jnp.min(jnp.where(elog == e1, lane, big), axis=-1, keepdims=True)
    elog2 = jnp.where(lane == i1, neg, elog)
    e2 = jnp.max(elog2, axis=-1, keepdims=True)
    i2 = jnp.min(jnp.where(elog2 == e2, lane, big), axis=-1, keepdims=True)
    r = jnp.exp(e2 - e1)
    w1 = p_g / (1.0 + r)
    w2 = p_g * r / (1.0 + r)
    comb_ref[...] = jnp.where(lane == i1, w1, jnp.where(lane == i2, w2, 0.0))


def _out_proj(att, rec, x2, w_out_bf, g_ffn, wr_hi, wr_lo, rbias, tm=512):
    T = x2.shape[0]
    row = lambda i: (i, 0)
    fixed = lambda i: (0, 0)
    return pl.pallas_call(
        _outproj_kernel,
        grid=(T // tm,),
        in_specs=[
            pl.BlockSpec((tm, ATT_WIDTH), row),
            pl.BlockSpec((tm, HGRN_WIDTH), row),
            pl.BlockSpec((tm, D_MODEL), row),
            pl.BlockSpec((D_MODEL, D_MODEL), fixed),
            pl.BlockSpec((1, D_MODEL), fixed),
            pl.BlockSpec((D_MODEL, ROUTER_LANES), fixed),
            pl.BlockSpec((D_MODEL, ROUTER_LANES), fixed),
            pl.BlockSpec((1, ROUTER_LANES), fixed),
        ],
        out_specs=[
            pl.BlockSpec((tm, D_MODEL), row),
            pl.BlockSpec((tm, D_MODEL), row),
            pl.BlockSpec((tm, ROUTER_LANES), row),
        ],
        out_shape=[
            jax.ShapeDtypeStruct((T, D_MODEL), F32),
            jax.ShapeDtypeStruct((T, D_MODEL), BF16),
            jax.ShapeDtypeStruct((T, ROUTER_LANES), F32),
        ],
        compiler_params=_cparams(("parallel",)),
        name="out_proj_router",
    )(att, rec, x2, w_out_bf, g_ffn, wr_hi, wr_lo, rbias)


def _moe_kernel(xt_ref, comb_ref, hres_ref, wg_ref, wu_ref, wd_ref, gf_ref, o_ref, acc_ref):
    e = pl.program_id(1)

    @pl.when(e == 0)
    def _():
        acc_ref[...] = jnp.zeros_like(acc_ref)

    xt = xt_ref[...]
    comb = comb_ref[...]
    lane = lax.broadcasted_iota(jnp.int32, comb.shape, 1)
    cw = jnp.sum(jnp.where(lane == e, comb, 0.0), axis=-1, keepdims=True)
    a = jnp.dot(xt, wg_ref[0], preferred_element_type=F32)
    u = jnp.dot(xt, wu_ref[0], preferred_element_type=F32)
    hid = (a * jax.nn.sigmoid(a)) * u * cw
    acc_ref[...] += jnp.dot(hid.astype(BF16), wd_ref[0], preferred_element_type=F32)

    @pl.when(e == pl.num_programs(1) - 1)
    def _():
        y = hres_ref[...] + acc_ref[...]
        ms = jnp.mean(y * y, axis=-1, keepdims=True)
        o_ref[...] = y * lax.rsqrt(ms + NORM_EPS) * gf_ref[...]


def _moe(xt, comb, hres, wg, wu, wd, g_final, tm=1024):
    T = xt.shape[0]
    tm = min(tm, T)
    row = lambda i, e: (i, 0)
    return pl.pallas_call(
        _moe_kernel,
        grid=(T // tm, N_EXPERTS),
        in_specs=[
            pl.BlockSpec((tm, D_MODEL), row),
            pl.BlockSpec((tm, ROUTER_LANES), row),
            pl.BlockSpec((tm, D_MODEL), row),
            pl.BlockSpec((1, D_MODEL, D_EXPERT), lambda i, e: (e, 0, 0)),
            pl.BlockSpec((1, D_MODEL, D_EXPERT), lambda i, e: (e, 0, 0)),
            pl.BlockSpec((1, D_EXPERT, D_MODEL), lambda i, e: (e, 0, 0)),
            pl.BlockSpec((1, D_MODEL), lambda i, e: (0, 0)),
        ],
        out_specs=pl.BlockSpec((tm, D_MODEL), row),
        out_shape=jax.ShapeDtypeStruct((T, D_MODEL), F32),
        scratch_shapes=[pltpu.VMEM((tm, D_MODEL), F32)],
        compiler_params=_cparams(("parallel", "arbitrary")),
        name="moe",
    )(xt, comb, hres, wg, wu, wd, g_final)


def _rope_tables(T):
    inv = ROPE_THETA ** (-jnp.arange(0, ATT_QKDIM, 2, dtype=F32) / ATT_QKDIM)
    ang = jnp.arange(T, dtype=F32)[:, None] * inv[None, :]
    c, s = jnp.cos(ang), jnp.sin(ang)
    return (jnp.concatenate([c, c, c, c], axis=1), jnp.concatenate([-s, s, -s, s], axis=1),
            c.T, s.T)


def kernel(x, w_in, w_out, g_mix, lam_params, subln_g, hgrn_gnorm_g, hgrn_lb, g_ffn, w_gr, b_gr,
           w_er, b_er, w_gate, w_up, w_down, g_final):
    B, T, D = x.shape
    x2 = x.reshape(B * T, D)
    l = 0
    w_in_bf = w_in[l].astype(BF16)
    w_out_bf = w_out[l].astype(BF16)
    lp = lam_params[l].astype(F32)
    lam = (jnp.exp(jnp.sum(lp[0] * lp[1])) - jnp.exp(jnp.sum(lp[2] * lp[3])) + LAMBDA_INIT).reshape(1, 1)
    lb = jnp.cumsum(jax.nn.softmax(hgrn_lb.astype(F32), axis=1), axis=1)[:, l]
    w_qv_t = jnp.concatenate([w_in[l][:, 0:ATT_WIDTH], w_in[l][:, 2 * ATT_WIDTH:3 * ATT_WIDTH]],
                             axis=1).T.astype(BF16)
    cos_t, sin_t, cos_tt, sin_tt = _rope_tables(T)
    w_r = jnp.concatenate([jnp.transpose(w_er[l], (1, 0, 2)).reshape(D, N_EXPERTS), w_gr[l],
                           jnp.zeros((D, ROUTER_LANES - N_EXPERTS - N_GROUPS), F32)], axis=1)
    wr_hi = w_r.astype(BF16)
    wr_lo = (w_r - wr_hi.astype(F32)).astype(BF16)
    rbias = jnp.concatenate([b_er[l].reshape(-1), b_gr[l],
                             jnp.zeros((ROUTER_LANES - N_EXPERTS - N_GROUPS,), F32)]).reshape(1, -1)
    wg = w_gate[l].reshape(N_EXPERTS, D, D_EXPERT).astype(BF16)
    wu = w_up[l].reshape(N_EXPERTS, D, D_EXPERT).astype(BF16)
    wd = w_down[l].reshape(N_EXPERTS, D_EXPERT, D).astype(BF16)

    qt, k, vt, hgrn_in = _in_proj(x2, g_mix[l].reshape(1, D), w_in_bf, w_qv_t, cos_t, sin_t,
                                  cos_tt, sin_tt)
    att = _diff_attn(lam, qt, k, vt, subln_g[l].reshape(-1, 1))
    o_b = _hgrn_pass(hgrn_in, lb[1:2], backward=True)
    rec = _hgrn_pass(hgrn_in, lb[0:1], backward=False, o_b=o_b,
                     gnorm_g=hgrn_gnorm_g[l].reshape(1, -1))
    hres, xt, comb = _out_proj(att, rec, x2, w_out_bf, g_ffn[l].reshape(1, D), wr_hi, wr_lo, rbias)
    out = _moe(xt, comb, hres, wg, wu, wd, g_final.reshape(1, D))
    return out.reshape(B, T, D)
```

```python
import functools
import math

import numpy as np
import jax
import jax.numpy as jnp
from jax import lax
from jax.experimental import pallas as pl
from jax.experimental.pallas import tpu as pltpu

D_MODEL = 1024
ATT_WIDTH = 512
ATT_HEADS = 4
ATT_VDIM = 128
ATT_QKDIM = 64
HGRN_WIDTH = 512
HGRN_HEADS = 4
HGRN_VDIM = 128
HGRN_EXPAND = 128
HGRN_FDIM = 512
N_GROUPS = 4
EXPERTS_PER_GROUP = 8
N_EXPERTS = N_GROUPS * EXPERTS_PER_GROUP
D_EXPERT = 512
ROPE_THETA = 10000.0
NORM_EPS = 1e-6
SUBLN_EPS = 1e-5
LAMBDA_INIT = 0.8 - 0.6 * math.exp(-0.3 * 0)
LOG2E = math.log2(math.e)
IN_COLS = 3 * ATT_WIDTH + 3 * HGRN_FDIM + 2 * HGRN_WIDTH
HGRN_COLS = IN_COLS - 3 * ATT_WIDTH

LANES = 128
VMEM_LIMIT = 56 * 1024 * 1024
HGRN_CHUNK = 64
IN_TM = 256
ATT_TK = 1024
ATT_CB = 256
ATT_VPAD = ATT_VDIM + 16
ROUTER_LANES = LANES

BF16 = jnp.bfloat16
F32 = jnp.float32


def _cparams(sem):
    return pltpu.CompilerParams(dimension_semantics=sem, vmem_limit_bytes=VMEM_LIMIT)


def _rot_half_64(x):
    lane = lax.broadcasted_iota(jnp.int32, x.shape, 1)
    fwd = pltpu.roll(x, 32, axis=1)
    bwd = pltpu.roll(x, 96, axis=1)
    return jnp.where((lane & 63) < 32, bwd, fwd)


def _inproj_kernel(x_ref, g_ref, w_ref, wqvt_ref, cos_ref, sin_ref, cost_ref, sint_ref,
                   qt_ref, k_ref, vt_ref, hg_ref):
    x = x_ref[...]
    tm = x.shape[0]
    ms = jnp.mean(x * x, axis=-1, keepdims=True)
    h = (x * lax.rsqrt(ms + NORM_EPS) * g_ref[...]).astype(BF16)
    nt = (((1,), (1,)), ((), ()))
    half = ATT_QKDIM // 2
    ct = cost_ref[...] * (ATT_QKDIM ** -0.5 * LOG2E)
    st = sint_ref[...] * (ATT_QKDIM ** -0.5 * LOG2E)
    for j in range(ATT_WIDTH // LANES):
        a = lax.dot_general(wqvt_ref[j * LANES:(j + 1) * LANES, :], h, nt,
                            preferred_element_type=F32)
        for c in range(LANES // ATT_QKDIM):
            x1 = a[c * ATT_QKDIM:c * ATT_QKDIM + half]
            x2 = a[c * ATT_QKDIM + half:(c + 1) * ATT_QKDIM]
            lo = j * LANES + c * ATT_QKDIM
            qt_ref[lo:lo + half, :] = (x1 * ct - x2 * st).astype(BF16)
            qt_ref[lo + half:lo + ATT_QKDIM, :] = (x1 * st + x2 * ct).astype(BF16)
    ones = jnp.ones((ATT_VPAD - ATT_VDIM, tm), BF16)
    for j in range(ATT_HEADS):
        lo = ATT_WIDTH + j * ATT_VDIM
        vt = lax.dot_general(wqvt_ref[lo:lo + ATT_VDIM, :], h, nt, preferred_element_type=F32)
        vt_ref[0, j * ATT_VPAD:j * ATT_VPAD + ATT_VDIM, :] = vt.astype(BF16)
        vt_ref[0, j * ATT_VPAD + ATT_VDIM:(j + 1) * ATT_VPAD, :] = ones
    cosv = cos_ref[...]
    sinv = sin_ref[...]
    for j in range(ATT_WIDTH // LANES):
        lo = ATT_WIDTH + j * LANES
        a = jnp.dot(h, w_ref[:, lo:lo + LANES], preferred_element_type=F32)
        k_ref[:, j * LANES:(j + 1) * LANES] = (a * cosv + _rot_half_64(a) * sinv).astype(BF16)
    for j in range(HGRN_COLS // 512):
        lo = 3 * ATT_WIDTH + j * 512
        hg_ref[:, j * 512:(j + 1) * 512] = jnp.dot(h, w_ref[:, lo:lo + 512],
                                                   preferred_element_type=F32)


def _in_proj(x2, g_mix, w_in_bf, w_qv_t, cos_t, sin_t, cos_tt, sin_tt, tk):
    T = x2.shape[0]
    tm = IN_TM
    per = tk // tm
    return pl.pallas_call(
        _inproj_kernel,
        grid=(T // tm,),
        in_specs=[
            pl.BlockSpec((tm, D_MODEL), lambda i: (i, 0)),
            pl.BlockSpec((1, D_MODEL), lambda i: (0, 0)),
            pl.BlockSpec((D_MODEL, IN_COLS), lambda i: (0, 0)),
            pl.BlockSpec((2 * ATT_WIDTH, D_MODEL), lambda i: (0, 0)),
            pl.BlockSpec((tm, LANES), lambda i: (i, 0)),
            pl.BlockSpec((tm, LANES), lambda i: (i, 0)),
            pl.BlockSpec((ATT_QKDIM // 2, tm), lambda i: (0, i)),
            pl.BlockSpec((ATT_QKDIM // 2, tm), lambda i: (0, i)),
        ],
        out_specs=[
            pl.BlockSpec((ATT_WIDTH, tm), lambda i: (0, i)),
            pl.BlockSpec((tm, ATT_WIDTH), lambda i: (i, 0)),
            pl.BlockSpec((1, ATT_HEADS * ATT_VPAD, tm), lambda i: (i // per, 0, i % per)),
            pl.BlockSpec((tm, HGRN_COLS), lambda i: (i, 0)),
        ],
        out_shape=[
            jax.ShapeDtypeStruct((ATT_WIDTH, T), BF16),
            jax.ShapeDtypeStruct((T, ATT_WIDTH), BF16),
            jax.ShapeDtypeStruct((T // tk, ATT_HEADS * ATT_VPAD, tk), BF16),
            jax.ShapeDtypeStruct((T, HGRN_COLS), F32),
        ],
        compiler_params=_cparams(("parallel",)),
        name="in_proj",
    )(x2, g_mix, w_in_bf, w_qv_t, cos_t, sin_t, cos_tt, sin_tt)


def _attn_kernel(lam_ref, qt_ref, k_ref, vt_ref, g_ref, o_ref, qw_ref, sa_ref, sb_ref, acc_ref):
    tq = qt_ref.shape[1]
    nchunk, _, tk = vt_ref.shape
    qt = qt_ref[...].astype(F32)
    row = lax.broadcasted_iota(jnp.int32, qt.shape, 0)
    qw_ref[:, 0:tq] = jnp.where(row < ATT_QKDIM, qt, 0.0).astype(BF16)
    qw_ref[:, tq:] = jnp.where(row >= ATT_QKDIM, qt, 0.0).astype(BF16)
    acc_ref[...] = jnp.zeros_like(acc_ref)
    ncb = 2 * tq // ATT_CB

    def scores(j, s_ref, m_old):
        off = pl.multiple_of(j * tk, tk)
        kc = k_ref[pl.ds(off, tk), :]
        m_new, alpha = [], []
        for cb in range(ncb):
            cols = slice(cb * ATT_CB, (cb + 1) * ATT_CB)
            s = jnp.dot(kc, qw_ref[:, cols], preferred_element_type=F32)
            s_ref[:, cols] = s
            mo = m_old[cb]
            mn = jnp.maximum(mo, jnp.max(s, axis=0, keepdims=True))
            m_new.append(mn)
            alpha.append(jnp.exp2(mo - mn))
        return tuple(m_new), tuple(alpha)

    def accumulate(j, s_ref, m_cur, alpha):
        vc = vt_ref[j]
        for cb in range(ncb):
            cols = slice(cb * ATT_CB, (cb + 1) * ATT_CB)
            p = jnp.exp2(s_ref[:, cols] - m_cur[cb]).astype(BF16)
            acc_ref[:, cols] = (alpha[cb] * acc_ref[:, cols]
                                + jnp.dot(vc, p, preferred_element_type=F32))

    m0 = tuple(jnp.full((1, ATT_CB), -jnp.inf, F32) for _ in range(ncb))
    state = scores(0, sa_ref, m0)

    def body(i, st):
        m_a, al_a = st
        j = 2 * i
        accumulate(j, sa_ref, m_a, al_a)
        m_b, al_b = scores(j + 1, sb_ref, m_a)
        accumulate(j + 1, sb_ref, m_b, al_b)
        return scores(jnp.minimum(j + 2, nchunk - 1), sa_ref, m_b)

    lax.fori_loop(0, nchunk // 2, body, state)
    lam = lam_ref[0, 0]
    acc = acc_ref[...]
    o1 = acc[0:ATT_VDIM, 0:tq] / acc[ATT_VDIM:ATT_VDIM + 1, 0:tq]
    o2 = acc[0:ATT_VDIM, tq:] / acc[ATT_VDIM:ATT_VDIM + 1, tq:]
    o = o1 - lam * o2
    ms = jnp.mean(o * o, axis=0, keepdims=True)
    o = o * lax.rsqrt(ms + SUBLN_EPS) * g_ref[...] * (1.0 - LAMBDA_INIT)
    o_ref[...] = o.T.astype(o_ref.dtype)


def _diff_attn(lam, qt, k, vt, subln_g, tq=512):
    T = k.shape[0]
    nchunk, _, tk = vt.shape
    return pl.pallas_call(
        _attn_kernel,
        grid=(ATT_HEADS, T // tq),
        in_specs=[
            pl.BlockSpec(memory_space=pltpu.SMEM),
            pl.BlockSpec((LANES, tq), lambda h, i: (h, i)),
            pl.BlockSpec((T, LANES), lambda h, i: (0, h)),
            pl.BlockSpec((nchunk, ATT_VPAD, tk), lambda h, i: (0, h, 0)),
            pl.BlockSpec((ATT_VDIM, 1), lambda h, i: (0, 0)),
        ],
        out_specs=pl.BlockSpec((tq, LANES), lambda h, i: (i, h)),
        out_shape=jax.ShapeDtypeStruct((T, ATT_WIDTH), BF16),
        scratch_shapes=[
            pltpu.VMEM((LANES, 2 * tq), BF16),
            pltpu.VMEM((tk, 2 * tq), F32),
            pltpu.VMEM((tk, 2 * tq), F32),
            pltpu.VMEM((ATT_VPAD, 2 * tq), F32),
        ],
        compiler_params=_cparams(("parallel", "parallel")),
        name="diff_attn",
    )(lam, qt, k, vt, subln_g)


def _hgrn_consts(C, backward):
    halves = []
    h = C // 2
    while h >= 1:
        halves.append(h)
        h //= 2
    E = np.zeros((2 + len(halves), C, C), np.float32)
    M = np.zeros((len(halves) + 1, C, C), np.float32)
    idx = np.arange(C)
    for t in range(C):
        if not backward:
            E[0, t, idx <= t] = 1.0
            E[1, t, idx > t] = 1.0
        else:
            E[0, t, idx >= t] = 1.0
            E[1, t, idx < t] = 1.0
    for li, h in enumerate(halves):
        for t in range(C):
            mid = (t // (2 * h)) * 2 * h + h
            upper = t >= mid
            if not backward:
                if upper:
                    E[2 + li, t, (idx >= mid) & (idx <= t)] = 1.0
                else:
                    E[2 + li, t, (idx > t) & (idx <= mid - 1)] = 1.0
            else:
                if not upper:
                    E[2 + li, t, (idx >= t) & (idx <= mid - 1)] = 1.0
                else:
                    E[2 + li, t, (idx >= mid) & (idx <= t - 1)] = 1.0
            for s in range(C):
                same = (s // (2 * h)) == (t // (2 * h))
                s_upper = s >= mid
                if same and ((not backward and upper and not s_upper)
                             or (backward and not upper and s_upper)):
                    M[li, t, s] = 1.0
    M[-1] = np.eye(C, dtype=np.float32)
    E = np.concatenate([E.reshape(-1, C), np.ones((8, C), np.float32)], axis=0)
    return E, M


def _split3(g):
    hi = g.astype(BF16)
    r1 = g - hi.astype(F32)
    mid = r1.astype(BF16)
    lo = (r1 - mid.astype(F32)).astype(BF16)
    return hi, mid, lo


def _hgrn_kernel(*refs, backward, nchunk, final):
    if final:
        (hq_ref, hf_ref, hi_ref, lb_ref, e_ref, m_ref, ob_ref, hgate_ref, gn_ref,
         o_ref, st_ref) = refs
    else:
        hq_ref, hf_ref, hi_ref, lb_ref, e_ref, m_ref, o_ref, st_ref = refs
    C = HGRN_CHUNK
    nlev = m_ref.shape[0] - 1
    nt = (((1,), (1,)), ((), ()))

    @pl.when(pl.program_id(0) == 0)
    def _():
        st_ref[...] = jnp.zeros_like(st_ref)

    lbd = lb_ref[...]
    emat = e_ref[...]
    order = range(nchunk - 1, -1, -1) if backward else range(nchunk)
    for c in order:
        rows = pl.ds(c * C, C)
        z = hf_ref[rows, :]
        hq = hq_ref[rows, :]
        f = lbd + (1.0 - lbd) * jax.nn.sigmoid(z)
        kk = (1.0 - lbd) * jax.nn.sigmoid(-z)
        g = jnp.log(f)
        q = hq * jax.nn.sigmoid(hq)
        vb = hi_ref[rows, :].astype(BF16)
        ghi, gmid, glo = _split3(g)
        ex = (jnp.dot(emat, ghi, preferred_element_type=F32)
              + jnp.dot(emat, gmid, preferred_element_type=F32)
              + jnp.dot(emat, glo, preferred_element_type=F32))
        ee = jnp.exp(ex)
        qi = (q * ee[0:C]).astype(BF16)
        ki = (kk * ee[C:2 * C]).astype(BF16)
        dec = ee[(2 + nlev) * C:(2 + nlev) * C + 1]
        qb = q.astype(BF16)
        kb = kk.astype(BF16)
        qlev = [(q * ee[(2 + l) * C:(3 + l) * C]).astype(BF16) for l in range(nlev)]
        klev = [(kk * ee[(2 + l) * C:(3 + l) * C]).astype(BF16) for l in range(nlev)]
        outs = []
        for hd in range(HGRN_HEADS):
            ln = slice(hd * LANES, (hd + 1) * LANES)
            a = m_ref[nlev] * lax.dot_general(qb[:, ln], kb[:, ln], nt, preferred_element_type=F32)
            for l in range(nlev):
                a = a + m_ref[l] * lax.dot_general(qlev[l][:, ln], klev[l][:, ln], nt,
                                                   preferred_element_type=F32)
            st = st_ref[hd]
            o = jnp.dot(a.astype(BF16), vb[:, ln], preferred_element_type=F32)
            o = o + lax.dot_general(qi[:, ln], st.astype(BF16), nt, preferred_element_type=F32)
            vt = hi_ref[rows, ln].T.astype(BF16)
            st_ref[hd] = st * dec[:, ln] + jnp.dot(vt, ki[:, ln], preferred_element_type=F32)
            outs.append(o)
        o_all = jnp.concatenate(outs, axis=1)
        if final:
            o_all = o_all + ob_ref[rows, :]
            res = []
            for hd in range(HGRN_HEADS):
                ln = slice(hd * LANES, (hd + 1) * LANES)
                oh = o_all[:, ln]
                ms = jnp.mean(oh * oh, axis=-1, keepdims=True)
                res.append(oh * lax.rsqrt(ms + NORM_EPS) * gn_ref[...])
            hg = hgate_ref[rows, :]
            o_all = jnp.concatenate(res, axis=1) * (hg * jax.nn.sigmoid(hg))
        o_ref[rows, :] = o_all.astype(o_ref.dtype)


def _hgrn_pass(hgrn_in, lb_row, backward, o_b=None, gnorm_g=None, tb=256):
    T = hgrn_in.shape[0]
    nblk = T // tb
    final = o_b is not None
    E, M = _hgrn_consts(HGRN_CHUNK, backward)
    e_bf = jnp.asarray(E, BF16)
    m_f = jnp.asarray(M, F32)
    blk = (lambda i: nblk - 1 - i) if backward else (lambda i: i)
    f_part = 2 if backward else 1
    in_specs = [
        pl.BlockSpec((tb, 512), lambda i: (blk(i), 0)),
        pl.BlockSpec((tb, 512), lambda i: (blk(i), f_part)),
        pl.BlockSpec((tb, 512), lambda i: (blk(i), 3)),
        pl.BlockSpec((1, 512), lambda i: (0, 0)),
        pl.BlockSpec(e_bf.shape, lambda i: (0, 0)),
        pl.BlockSpec(m_f.shape, lambda i: (0, 0, 0)),
    ]
    args = [hgrn_in, hgrn_in, hgrn_in, lb_row, e_bf, m_f]
    if final:
        in_specs += [
            pl.BlockSpec((tb, 512), lambda i: (blk(i), 0)),
            pl.BlockSpec((tb, 512), lambda i: (blk(i), 4)),
            pl.BlockSpec((1, HGRN_VDIM), lambda i: (0, 0)),
        ]
        args += [o_b, hgrn_in, gnorm_g]
    return pl.pallas_call(
        functools.partial(_hgrn_kernel, backward=backward, nchunk=tb // HGRN_CHUNK, final=final),
        grid=(nblk,),
        in_specs=in_specs,
        out_specs=pl.BlockSpec((tb, 512), lambda i: (blk(i), 0)),
        out_shape=jax.ShapeDtypeStruct((T, HGRN_WIDTH), BF16 if final else F32),
        scratch_shapes=[pltpu.VMEM((HGRN_HEADS, HGRN_VDIM, HGRN_EXPAND), F32)],
        compiler_params=_cparams(("arbitrary",)),
        name="hgrn_bwd" if backward else "hgrn_fwd",
    )(*args)


def _outproj_kernel(att_ref, rec_ref, x_ref, wo_ref, g_ref, wr_hi_ref, wr_lo_ref, rb_ref,
                    hres_ref, xt_ref, comb_ref):
    mix = (jnp.dot(att_ref[...], wo_ref[0:ATT_WIDTH, :], preferred_element_type=F32)
           + jnp.dot(rec_ref[...], wo_ref[ATT_WIDTH:, :], preferred_element_type=F32))
    hres = x_ref[...] + mix
    hres_ref[...] = hres
    ms = jnp.mean(hres * hres, axis=-1, keepdims=True)
    xt = hres * lax.rsqrt(ms + NORM_EPS) * g_ref[...]
    xt_hi = xt.astype(BF16)
    xt_ref[...] = xt_hi
    xt_lo = (xt - xt_hi.astype(F32)).astype(BF16)
    logits = (jnp.dot(xt_hi, wr_hi_ref[...], preferred_element_type=F32)
              + jnp.dot(xt_hi, wr_lo_ref[...], preferred_element_type=F32)
              + jnp.dot(xt_lo, wr_hi_ref[...], preferred_element_type=F32)) + rb_ref[...]
    lane = lax.broadcasted_iota(jnp.int32, logits.shape, 1)
    neg = -jnp.inf
    big = jnp.int32(1 << 20)
    gmask = (lane >= N_EXPERTS) & (lane < N_EXPERTS + N_GROUPS)
    glog = jnp.where(gmask, logits, neg)
    gmax = jnp.max(glog, axis=-1, keepdims=True)
    gsum = jnp.sum(jnp.exp(glog - gmax), axis=-1, keepdims=True)
    p_g = 1.0 / gsum
    g_idx = jnp.min(jnp.where(glog == gmax, lane, big), axis=-1, keepdims=True) - N_EXPERTS
    emask = (lane < N_EXPERTS) & ((lane >> 3) == g_idx)
    elog = jnp.where(emask, logits, neg)
    e1 = jnp.max(elog, axis=-1, keepdims=True)
    i1 = jnp.min(jnp.where(elog == e1, lane, big), axis=-1, keepdims=True)
    elog2 = jnp.where(lane == i1, neg, elog)
    e2 = jnp.max(elog2, axis=-1, keepdims=True)
    i2 = jnp.min(jnp.where(elog2 == e2, lane, big), axis=-1, keepdims=True)
    r = jnp.exp(e2 - e1)
    w1 = p_g / (1.0 + r)
    w2 = p_g * r / (1.0 + r)
    comb_ref[...] = jnp.where(lane == i1, w1, jnp.where(lane == i2, w2, 0.0))


def _out_proj(att, rec, x2, w_out_bf, g_ffn, wr_hi, wr_lo, rbias, tm=512):
    T = x2.shape[0]
    row = lambda i: (i, 0)
    fixed = lambda i: (0, 0)
    return pl.pallas_call(
        _outproj_kernel,
        grid=(T // tm,),
        in_specs=[
            pl.BlockSpec((tm, ATT_WIDTH), row),
            pl.BlockSpec((tm, HGRN_WIDTH), row),
            pl.BlockSpec((tm, D_MODEL), row),
            pl.BlockSpec((D_MODEL, D_MODEL), fixed),
            pl.BlockSpec((1, D_MODEL), fixed),
            pl.BlockSpec((D_MODEL, ROUTER_LANES), fixed),
            pl.BlockSpec((D_MODEL, ROUTER_LANES), fixed),
            pl.BlockSpec((1, ROUTER_LANES), fixed),
        ],
        out_specs=[
            pl.BlockSpec((tm, D_MODEL), row),
            pl.BlockSpec((tm, D_MODEL), row),
            pl.BlockSpec((tm, ROUTER_LANES), row),
        ],
        out_shape=[
            jax.ShapeDtypeStruct((T, D_MODEL), F32),
            jax.ShapeDtypeStruct((T, D_MODEL), BF16),
            jax.ShapeDtypeStruct((T, ROUTER_LANES), F32),
        ],
        compiler_params=_cparams(("parallel",)),
        name="out_proj_router",
    )(att, rec, x2, w_out_bf, g_ffn, wr_hi, wr_lo, rbias)


def _moe_kernel(xt_ref, comb_ref, hres_ref, wg_ref, wu_ref, wd_ref, gf_ref, o_ref, acc_ref):
    e = pl.program_id(1)

    @pl.when(e == 0)
    def _():
        acc_ref[...] = jnp.zeros_like(acc_ref)

    xt = xt_ref[...]
    comb = comb_ref[...]
    lane = lax.broadcasted_iota(jnp.int32, comb.shape, 1)
    cw = jnp.sum(jnp.where(lane == e, comb, 0.0), axis=-1, keepdims=True)
    a = jnp.dot(xt, wg_ref[0], preferred_element_type=F32)
    u = jnp.dot(xt, wu_ref[0], preferred_element_type=F32)
    hid = (a * jax.nn.sigmoid(a)) * u * cw
    acc_ref[...] += jnp.dot(hid.astype(BF16), wd_ref[0], preferred_element_type=F32)

    @pl.when(e == pl.num_programs(1) - 1)
    def _():
        y = hres_ref[...] + acc_ref[...]
        ms = jnp.mean(y * y, axis=-1, keepdims=True)
        o_ref[...] = y * lax.rsqrt(ms + NORM_EPS) * gf_ref[...]


def _moe(xt, comb, hres, wg, wu, wd, g_final, tm=1024):
    T = xt.shape[0]
    tm = min(tm, T)
    row = lambda i, e: (i, 0)
    return pl.pallas_call(
        _moe_kernel,
        grid=(T // tm, N_EXPERTS),
        in_specs=[
            pl.BlockSpec((tm, D_MODEL), row),
            pl.BlockSpec((tm, ROUTER_LANES), row),
            pl.BlockSpec((tm, D_MODEL), row),
            pl.BlockSpec((1, D_MODEL, D_EXPERT), lambda i, e: (e, 0, 0)),
            pl.BlockSpec((1, D_MODEL, D_EXPERT), lambda i, e: (e, 0, 0)),
            pl.BlockSpec((1, D_EXPERT, D_MODEL), lambda i, e: (e, 0, 0)),
            pl.BlockSpec((1, D_MODEL), lambda i, e: (0, 0)),
        ],
        out_specs=pl.BlockSpec((tm, D_MODEL), row),
        out_shape=jax.ShapeDtypeStruct((T, D_MODEL), F32),
        scratch_shapes=[pltpu.VMEM((tm, D_MODEL), F32)],
        compiler_params=_cparams(("parallel", "arbitrary")),
        name="moe",
    )(xt, comb, hres, wg, wu, wd, g_final)


def _rope_tables(T):
    inv = ROPE_THETA ** (-jnp.arange(0, ATT_QKDIM, 2, dtype=F32) / ATT_QKDIM)
    ang = jnp.arange(T, dtype=F32)[:, None] * inv[None, :]
    c, s = jnp.cos(ang), jnp.sin(ang)
    return (jnp.concatenate([c, c, c, c], axis=1), jnp.concatenate([-s, s, -s, s], axis=1),
            c.T, s.T)


def kernel(x, w_in, w_out, g_mix, lam_params, subln_g, hgrn_gnorm_g, hgrn_lb, g_ffn, w_gr, b_gr,
           w_er, b_er, w_gate, w_up, w_down, g_final):
    B, T, D = x.shape
    x2 = x.reshape(B * T, D)
    l = 0
    w_in_bf = w_in[l].astype(BF16)
    w_out_bf = w_out[l].astype(BF16)
    lp = lam_params[l].astype(F32)
    lam = (jnp.exp(jnp.sum(lp[0] * lp[1])) - jnp.exp(jnp.sum(lp[2] * lp[3])) + LAMBDA_INIT).reshape(1, 1)
    lb = jnp.cumsum(jax.nn.softmax(hgrn_lb.astype(F32), axis=1), axis=1)[:, l]
    w_qv_t = jnp.concatenate([w_in[l][:, 0:ATT_WIDTH], w_in[l][:, 2 * ATT_WIDTH:3 * ATT_WIDTH]],
                             axis=1).T.astype(BF16)
    cos_t, sin_t, cos_tt, sin_tt = _rope_tables(T)
    w_r = jnp.concatenate([jnp.transpose(w_er[l], (1, 0, 2)).reshape(D, N_EXPERTS), w_gr[l],
                           jnp.zeros((D, ROUTER_LANES - N_EXPERTS - N_GROUPS), F32)], axis=1)
    wr_hi = w_r.astype(BF16)
    wr_lo = (w_r - wr_hi.astype(F32)).astype(BF16)
    rbias = jnp.concatenate([b_er[l].reshape(-1), b_gr[l],
                             jnp.zeros((ROUTER_LANES - N_EXPERTS - N_GROUPS,), F32)]).reshape(1, -1)
    wg = w_gate[l].reshape(N_EXPERTS, D, D_EXPERT).astype(BF16)
    wu = w_up[l].reshape(N_EXPERTS, D, D_EXPERT).astype(BF16)
    wd = w_down[l].reshape(N_EXPERTS, D_EXPERT, D).astype(BF16)

    tk = min(ATT_TK, (B * T) // 2)
    qt, k, vt, hgrn_in = _in_proj(x2, g_mix[l].reshape(1, D), w_in_bf, w_qv_t, cos_t, sin_t,
                                  cos_tt, sin_tt, tk)
    att = _diff_attn(lam, qt, k, vt, subln_g[l].reshape(-1, 1))
    o_b = _hgrn_pass(hgrn_in, lb[1:2], backward=True)
    rec = _hgrn_pass(hgrn_in, lb[0:1], backward=False, o_b=o_b,
                     gnorm_g=hgrn_gnorm_g[l].reshape(1, -1))
    hres, xt, comb = _out_proj(att, rec, x2, w_out_bf, g_ffn[l].reshape(1, D), wr_hi, wr_lo, rbias)
    out = _moe(xt, comb, hres, wg, wu, wd, g_final.reshape(1, D))
    return out.reshape(B, T, D)
```

```python
import functools
import math

import numpy as np
import jax
import jax.numpy as jnp
from jax import lax
from jax.experimental import pallas as pl
from jax.experimental.pallas import tpu as pltpu

D_MODEL = 1024
ATT_WIDTH = 512
ATT_HEADS = 4
ATT_VDIM = 128
ATT_QKDIM = 64
HGRN_WIDTH = 512
HGRN_HEADS = 4
HGRN_VDIM = 128
HGRN_EXPAND = 128
HGRN_FDIM = 512
N_GROUPS = 4
EXPERTS_PER_GROUP = 8
N_EXPERTS = N_GROUPS * EXPERTS_PER_GROUP
D_EXPERT = 512
ROPE_THETA = 10000.0
NORM_EPS = 1e-6
SUBLN_EPS = 1e-5
LAMBDA_INIT = 0.8 - 0.6 * math.exp(-0.3 * 0)
LOG2E = math.log2(math.e)
IN_COLS = 3 * ATT_WIDTH + 3 * HGRN_FDIM + 2 * HGRN_WIDTH
HGRN_COLS = IN_COLS - 3 * ATT_WIDTH

LANES = 128
VMEM_LIMIT = 56 * 1024 * 1024
HGRN_CHUNK = 64
IN_TM = 256
ATT_TK = 1024
ATT_CB = 256
MOE_BT = 1024
MOE_ROWS = 128
MOE_UNROLL = 8
ATT_VPAD = ATT_VDIM + 16
ROUTER_LANES = LANES

BF16 = jnp.bfloat16
F32 = jnp.float32


def _cparams(sem):
    return pltpu.CompilerParams(dimension_semantics=sem, vmem_limit_bytes=VMEM_LIMIT)


def _rot_half_64(x):
    lane = lax.broadcasted_iota(jnp.int32, x.shape, 1)
    fwd = pltpu.roll(x, 32, axis=1)
    bwd = pltpu.roll(x, 96, axis=1)
    return jnp.where((lane & 63) < 32, bwd, fwd)


def _inproj_kernel(x_ref, g_ref, w_ref, wqvt_ref, cos_ref, sin_ref, cost_ref, sint_ref,
                   qt_ref, k_ref, vt_ref, hg_ref):
    x = x_ref[...]
    tm = x.shape[0]
    ms = jnp.mean(x * x, axis=-1, keepdims=True)
    h = (x * lax.rsqrt(ms + NORM_EPS) * g_ref[...]).astype(BF16)
    nt = (((1,), (1,)), ((), ()))
    half = ATT_QKDIM // 2
    ct = cost_ref[...] * (ATT_QKDIM ** -0.5 * LOG2E)
    st = sint_ref[...] * (ATT_QKDIM ** -0.5 * LOG2E)
    for j in range(ATT_WIDTH // LANES):
        a = lax.dot_general(wqvt_ref[j * LANES:(j + 1) * LANES, :], h, nt,
                            preferred_element_type=F32)
        for c in range(LANES // ATT_QKDIM):
            x1 = a[c * ATT_QKDIM:c * ATT_QKDIM + half]
            x2 = a[c * ATT_QKDIM + half:(c + 1) * ATT_QKDIM]
            lo = j * LANES + c * ATT_QKDIM
            qt_ref[lo:lo + half, :] = (x1 * ct - x2 * st).astype(BF16)
            qt_ref[lo + half:lo + ATT_QKDIM, :] = (x1 * st + x2 * ct).astype(BF16)
    ones = jnp.ones((ATT_VPAD - ATT_VDIM, tm), BF16)
    for j in range(ATT_HEADS):
        lo = ATT_WIDTH + j * ATT_VDIM
        vt = lax.dot_general(wqvt_ref[lo:lo + ATT_VDIM, :], h, nt, preferred_element_type=F32)
        vt_ref[0, j * ATT_VPAD:j * ATT_VPAD + ATT_VDIM, :] = vt.astype(BF16)
        vt_ref[0, j * ATT_VPAD + ATT_VDIM:(j + 1) * ATT_VPAD, :] = ones
    cosv = cos_ref[...]
    sinv = sin_ref[...]
    for j in range(ATT_WIDTH // LANES):
        lo = ATT_WIDTH + j * LANES
        a = jnp.dot(h, w_ref[:, lo:lo + LANES], preferred_element_type=F32)
        k_ref[:, j * LANES:(j + 1) * LANES] = (a * cosv + _rot_half_64(a) * sinv).astype(BF16)
    for j in range(HGRN_COLS // 512):
        lo = 3 * ATT_WIDTH + j * 512
        hg_ref[:, j * 512:(j + 1) * 512] = jnp.dot(h, w_ref[:, lo:lo + 512],
                                                   preferred_element_type=F32)


def _in_proj(x2, g_mix, w_in_bf, w_qv_t, cos_t, sin_t, cos_tt, sin_tt, tk):
    T = x2.shape[0]
    tm = IN_TM
    per = tk // tm
    return pl.pallas_call(
        _inproj_kernel,
        grid=(T // tm,),
        in_specs=[
            pl.BlockSpec((tm, D_MODEL), lambda i: (i, 0)),
            pl.BlockSpec((1, D_MODEL), lambda i: (0, 0)),
            pl.BlockSpec((D_MODEL, IN_COLS), lambda i: (0, 0)),
            pl.BlockSpec((2 * ATT_WIDTH, D_MODEL), lambda i: (0, 0)),
            pl.BlockSpec((tm, LANES), lambda i: (i, 0)),
            pl.BlockSpec((tm, LANES), lambda i: (i, 0)),
            pl.BlockSpec((ATT_QKDIM // 2, tm), lambda i: (0, i)),
            pl.BlockSpec((ATT_QKDIM // 2, tm), lambda i: (0, i)),
        ],
        out_specs=[
            pl.BlockSpec((ATT_WIDTH, tm), lambda i: (0, i)),
            pl.BlockSpec((tm, ATT_WIDTH), lambda i: (i, 0)),
            pl.BlockSpec((1, ATT_HEADS * ATT_VPAD, tm), lambda i: (i // per, 0, i % per)),
            pl.BlockSpec((tm, HGRN_COLS), lambda i: (i, 0)),
        ],
        out_shape=[
            jax.ShapeDtypeStruct((ATT_WIDTH, T), BF16),
            jax.ShapeDtypeStruct((T, ATT_WIDTH), BF16),
            jax.ShapeDtypeStruct((T // tk, ATT_HEADS * ATT_VPAD, tk), BF16),
            jax.ShapeDtypeStruct((T, HGRN_COLS), F32),
        ],
        compiler_params=_cparams(("parallel",)),
        name="in_proj",
    )(x2, g_mix, w_in_bf, w_qv_t, cos_t, sin_t, cos_tt, sin_tt)


def _attn_kernel(lam_ref, qt_ref, k_ref, vt_ref, g_ref, o_ref, qw_ref, sa_ref, sb_ref, acc_ref):
    tq = qt_ref.shape[1]
    nchunk, _, tk = vt_ref.shape
    qt = qt_ref[...].astype(F32)
    row = lax.broadcasted_iota(jnp.int32, qt.shape, 0)
    qw_ref[:, 0:tq] = jnp.where(row < ATT_QKDIM, qt, 0.0).astype(BF16)
    qw_ref[:, tq:] = jnp.where(row >= ATT_QKDIM, qt, 0.0).astype(BF16)
    acc_ref[...] = jnp.zeros_like(acc_ref)
    ncb = 2 * tq // ATT_CB

    def scores(j, s_ref, m_old):
        off = pl.multiple_of(j * tk, tk)
        kc = k_ref[pl.ds(off, tk), :]
        m_new, alpha = [], []
        for cb in range(ncb):
            cols = slice(cb * ATT_CB, (cb + 1) * ATT_CB)
            s = jnp.dot(kc, qw_ref[:, cols], preferred_element_type=F32)
            s_ref[:, cols] = s
            mo = m_old[cb]
            mn = jnp.maximum(mo, jnp.max(s, axis=0, keepdims=True))
            m_new.append(mn)
            alpha.append(jnp.exp2(mo - mn))
        return tuple(m_new), tuple(alpha)

    def accumulate(j, s_ref, m_cur, alpha):
        vc = vt_ref[j]
        for cb in range(ncb):
            cols = slice(cb * ATT_CB, (cb + 1) * ATT_CB)
            p = jnp.exp2(s_ref[:, cols] - m_cur[cb]).astype(BF16)
            acc_ref[:, cols] = (alpha[cb] * acc_ref[:, cols]
                                + jnp.dot(vc, p, preferred_element_type=F32))

    m0 = tuple(jnp.full((1, ATT_CB), -jnp.inf, F32) for _ in range(ncb))
    state = scores(0, sa_ref, m0)

    def body(i, st):
        m_a, al_a = st
        j = 2 * i
        accumulate(j, sa_ref, m_a, al_a)
        m_b, al_b = scores(j + 1, sb_ref, m_a)
        accumulate(j + 1, sb_ref, m_b, al_b)
        return scores(jnp.minimum(j + 2, nchunk - 1), sa_ref, m_b)

    lax.fori_loop(0, nchunk // 2, body, state)
    lam = lam_ref[0, 0]
    acc = acc_ref[...]
    o1 = acc[0:ATT_VDIM, 0:tq] / acc[ATT_VDIM:ATT_VDIM + 1, 0:tq]
    o2 = acc[0:ATT_VDIM, tq:] / acc[ATT_VDIM:ATT_VDIM + 1, tq:]
    o = o1 - lam * o2
    ms = jnp.mean(o * o, axis=0, keepdims=True)
    o = o * lax.rsqrt(ms + SUBLN_EPS) * g_ref[...] * (1.0 - LAMBDA_INIT)
    o_ref[...] = o.T.astype(o_ref.dtype)


def _diff_attn(lam, qt, k, vt, subln_g, tq=512):
    T = k.shape[0]
    nchunk, _, tk = vt.shape
    return pl.pallas_call(
        _attn_kernel,
        grid=(ATT_HEADS, T // tq),
        in_specs=[
            pl.BlockSpec(memory_space=pltpu.SMEM),
            pl.BlockSpec((LANES, tq), lambda h, i: (h, i)),
            pl.BlockSpec((T, LANES), lambda h, i: (0, h)),
            pl.BlockSpec((nchunk, ATT_VPAD, tk), lambda h, i: (0, h, 0)),
            pl.BlockSpec((ATT_VDIM, 1), lambda h, i: (0, 0)),
        ],
        out_specs=pl.BlockSpec((tq, LANES), lambda h, i: (i, h)),
        out_shape=jax.ShapeDtypeStruct((T, ATT_WIDTH), BF16),
        scratch_shapes=[
            pltpu.VMEM((LANES, 2 * tq), BF16),
            pltpu.VMEM((tk, 2 * tq), F32),
            pltpu.VMEM((tk, 2 * tq), F32),
            pltpu.VMEM((ATT_VPAD, 2 * tq), F32),
        ],
        compiler_params=_cparams(("parallel", "parallel")),
        name="diff_attn",
    )(lam, qt, k, vt, subln_g)


def _hgrn_consts(C, backward):
    halves = []
    h = C // 2
    while h >= 1:
        halves.append(h)
        h //= 2
    E = np.zeros((2 + len(halves), C, C), np.float32)
    M = np.zeros((len(halves) + 1, C, C), np.float32)
    idx = np.arange(C)
    for t in range(C):
        if not backward:
            E[0, t, idx <= t] = 1.0
            E[1, t, idx > t] = 1.0
        else:
            E[0, t, idx >= t] = 1.0
            E[1, t, idx < t] = 1.0
    for li, h in enumerate(halves):
        for t in range(C):
            mid = (t // (2 * h)) * 2 * h + h
            upper = t >= mid
            if not backward:
                if upper:
                    E[2 + li, t, (idx >= mid) & (idx <= t)] = 1.0
                else:
                    E[2 + li, t, (idx > t) & (idx <= mid - 1)] = 1.0
            else:
                if not upper:
                    E[2 + li, t, (idx >= t) & (idx <= mid - 1)] = 1.0
                else:
                    E[2 + li, t, (idx >= mid) & (idx <= t - 1)] = 1.0
            for s in range(C):
                same = (s // (2 * h)) == (t // (2 * h))
                s_upper = s >= mid
                if same and ((not backward and upper and not s_upper)
                             or (backward and not upper and s_upper)):
                    M[li, t, s] = 1.0
    M[-1] = np.eye(C, dtype=np.float32)
    E = np.concatenate([E.reshape(-1, C), np.ones((8, C), np.float32)], axis=0)
    return E, M


def _split3(g):
    hi = g.astype(BF16)
    r1 = g - hi.astype(F32)
    mid = r1.astype(BF16)
    lo = (r1 - mid.astype(F32)).astype(BF16)
    return hi, mid, lo


def _hgrn_kernel(*refs, backward, nchunk, final):
    if final:
        (hq_ref, hf_ref, hi_ref, lb_ref, e_ref, m_ref, ob_ref, hgate_ref, gn_ref,
         o_ref, st_ref) = refs
    else:
        hq_ref, hf_ref, hi_ref, lb_ref, e_ref, m_ref, o_ref, st_ref = refs
    C = HGRN_CHUNK
    nlev = m_ref.shape[0] - 1
    nt = (((1,), (1,)), ((), ()))

    @pl.when(pl.program_id(0) == 0)
    def _():
        st_ref[...] = jnp.zeros_like(st_ref)

    lbd = lb_ref[...]
    emat = e_ref[...]
    order = range(nchunk - 1, -1, -1) if backward else range(nchunk)
    for c in order:
        rows = pl.ds(c * C, C)
        z = hf_ref[rows, :]
        hq = hq_ref[rows, :]
        f = lbd + (1.0 - lbd) * jax.nn.sigmoid(z)
        kk = (1.0 - lbd) * jax.nn.sigmoid(-z)
        g = jnp.log(f)
        q = hq * jax.nn.sigmoid(hq)
        vb = hi_ref[rows, :].astype(BF16)
        ghi, gmid, glo = _split3(g)
        ex = (jnp.dot(emat, ghi, preferred_element_type=F32)
              + jnp.dot(emat, gmid, preferred_element_type=F32)
              + jnp.dot(emat, glo, preferred_element_type=F32))
        ee = jnp.exp(ex)
        qi = (q * ee[0:C]).astype(BF16)
        ki = (kk * ee[C:2 * C]).astype(BF16)
        dec = ee[(2 + nlev) * C:(2 + nlev) * C + 1]
        qb = q.astype(BF16)
        kb = kk.astype(BF16)
        qlev = [(q * ee[(2 + l) * C:(3 + l) * C]).astype(BF16) for l in range(nlev)]
        klev = [(kk * ee[(2 + l) * C:(3 + l) * C]).astype(BF16) for l in range(nlev)]
        outs = []
        for hd in range(HGRN_HEADS):
            ln = slice(hd * LANES, (hd + 1) * LANES)
            a = m_ref[nlev] * lax.dot_general(qb[:, ln], kb[:, ln], nt, preferred_element_type=F32)
            for l in range(nlev):
                a = a + m_ref[l] * lax.dot_general(qlev[l][:, ln], klev[l][:, ln], nt,
                                                   preferred_element_type=F32)
            st = st_ref[hd]
            o = jnp.dot(a.astype(BF16), vb[:, ln], preferred_element_type=F32)
            o = o + lax.dot_general(qi[:, ln], st.astype(BF16), nt, preferred_element_type=F32)
            vt = hi_ref[rows, ln].T.astype(BF16)
            st_ref[hd] = st * dec[:, ln] + jnp.dot(vt, ki[:, ln], preferred_element_type=F32)
            outs.append(o)
        o_all = jnp.concatenate(outs, axis=1)
        if final:
            o_all = o_all + ob_ref[rows, :]
            res = []
            for hd in range(HGRN_HEADS):
                ln = slice(hd * LANES, (hd + 1) * LANES)
                oh = o_all[:, ln]
                ms = jnp.mean(oh * oh, axis=-1, keepdims=True)
                res.append(oh * lax.rsqrt(ms + NORM_EPS) * gn_ref[...])
            hg = hgate_ref[rows, :]
            o_all = jnp.concatenate(res, axis=1) * (hg * jax.nn.sigmoid(hg))
        o_ref[rows, :] = o_all.astype(o_ref.dtype)


def _hgrn_pass(hgrn_in, lb_row, backward, o_b=None, gnorm_g=None, tb=256):
    T = hgrn_in.shape[0]
    nblk = T // tb
    final = o_b is not None
    E, M = _hgrn_consts(HGRN_CHUNK, backward)
    e_bf = jnp.asarray(E, BF16)
    m_f = jnp.asarray(M, F32)
    blk = (lambda i: nblk - 1 - i) if backward else (lambda i: i)
    f_part = 2 if backward else 1
    in_specs = [
        pl.BlockSpec((tb, 512), lambda i: (blk(i), 0)),
        pl.BlockSpec((tb, 512), lambda i: (blk(i), f_part)),
        pl.BlockSpec((tb, 512), lambda i: (blk(i), 3)),
        pl.BlockSpec((1, 512), lambda i: (0, 0)),
        pl.BlockSpec(e_bf.shape, lambda i: (0, 0)),
        pl.BlockSpec(m_f.shape, lambda i: (0, 0, 0)),
    ]
    args = [hgrn_in, hgrn_in, hgrn_in, lb_row, e_bf, m_f]
    if final:
        in_specs += [
            pl.BlockSpec((tb, 512), lambda i: (blk(i), 0)),
            pl.BlockSpec((tb, 512), lambda i: (blk(i), 4)),
            pl.BlockSpec((1, HGRN_VDIM), lambda i: (0, 0)),
        ]
        args += [o_b, hgrn_in, gnorm_g]
    return pl.pallas_call(
        functools.partial(_hgrn_kernel, backward=backward, nchunk=tb // HGRN_CHUNK, final=final),
        grid=(nblk,),
        in_specs=in_specs,
        out_specs=pl.BlockSpec((tb, 512), lambda i: (blk(i), 0)),
        out_shape=jax.ShapeDtypeStruct((T, HGRN_WIDTH), BF16 if final else F32),
        scratch_shapes=[pltpu.VMEM((HGRN_HEADS, HGRN_VDIM, HGRN_EXPAND), F32)],
        compiler_params=_cparams(("arbitrary",)),
        name="hgrn_bwd" if backward else "hgrn_fwd",
    )(*args)


def _outproj_kernel(att_ref, rec_ref, x_ref, wo_ref, g_ref, wr_hi_ref, wr_lo_ref, rb_ref,
                    hres_ref, xt_ref, comb_ref):
    mix = (jnp.dot(att_ref[...], wo_ref[0:ATT_WIDTH, :], preferred_element_type=F32)
           + jnp.dot(rec_ref[...], wo_ref[ATT_WIDTH:, :], preferred_element_type=F32))
    hres = x_ref[...] + mix
    hres_ref[...] = hres
    ms = jnp.mean(hres * hres, axis=-1, keepdims=True)
    xt = hres * lax.rsqrt(ms + NORM_EPS) * g_ref[...]
    xt_hi = xt.astype(BF16)
    xt_ref[...] = xt_hi
    xt_lo = (xt - xt_hi.astype(F32)).astype(BF16)
    logits = (jnp.dot(xt_hi, wr_hi_ref[...], preferred_element_type=F32)
              + jnp.dot(xt_hi, wr_lo_ref[...], preferred_element_type=F32)
              + jnp.dot(xt_lo, wr_hi_ref[...], preferred_element_type=F32)) + rb_ref[...]
    lane = lax.broadcasted_iota(jnp.int32, logits.shape, 1)
    neg = -jnp.inf
    big = jnp.int32(1 << 20)
    gmask = (lane >= N_EXPERTS) & (lane < N_EXPERTS + N_GROUPS)
    glog = jnp.where(gmask, logits, neg)
    gmax = jnp.max(glog, axis=-1, keepdims=True)
    gsum = jnp.sum(jnp.exp(glog - gmax), axis=-1, keepdims=True)
    p_g = 1.0 / gsum
    g_idx = jnp.min(jnp.where(glog == gmax, lane, big), axis=-1, keepdims=True) - N_EXPERTS
    emask = (lane < N_EXPERTS) & ((lane >> 3) == g_idx)
    elog = jnp.where(emask, logits, neg)
    e1 = jnp.max(elog, axis=-1, keepdims=True)
    i1 = jnp.min(jnp.where(elog == e1, lane, big), axis=-1, keepdims=True)
    elog2 = jnp.where(lane == i1, neg, elog)
    e2 = jnp.max(elog2, axis=-1, keepdims=True)
    i2 = jnp.min(jnp.where(elog2 == e2, lane, big), axis=-1, keepdims=True)
    r = jnp.exp(e2 - e1)
    w1 = p_g / (1.0 + r)
    w2 = p_g * r / (1.0 + r)
    comb_ref[...] = jnp.where(lane == 0, i1.astype(F32), jnp.where(
        lane == 1, i2.astype(F32), jnp.where(lane == 2, w1, jnp.where(lane == 3, w2, 0.0))))


def _out_proj(att, rec, x2, w_out_bf, g_ffn, wr_hi, wr_lo, rbias, tm=512):
    T = x2.shape[0]
    row = lambda i: (i, 0)
    fixed = lambda i: (0, 0)
    return pl.pallas_call(
        _outproj_kernel,
        grid=(T // tm,),
        in_specs=[
            pl.BlockSpec((tm, ATT_WIDTH), row),
            pl.BlockSpec((tm, HGRN_WIDTH), row),
            pl.BlockSpec((tm, D_MODEL), row),
            pl.BlockSpec((D_MODEL, D_MODEL), fixed),
            pl.BlockSpec((1, D_MODEL), fixed),
            pl.BlockSpec((D_MODEL, ROUTER_LANES), fixed),
            pl.BlockSpec((D_MODEL, ROUTER_LANES), fixed),
            pl.BlockSpec((1, ROUTER_LANES), fixed),
        ],
        out_specs=[
            pl.BlockSpec((tm, D_MODEL), row),
            pl.BlockSpec((tm, D_MODEL), row),
            pl.BlockSpec((tm, ROUTER_LANES), row),
        ],
        out_shape=[
            jax.ShapeDtypeStruct((T, D_MODEL), F32),
            jax.ShapeDtypeStruct((T, D_MODEL), BF16),
            jax.ShapeDtypeStruct((T, ROUTER_LANES), F32),
        ],
        compiler_params=_cparams(("parallel",)),
        name="out_proj_router",
    )(att, rec, x2, w_out_bf, g_ffn, wr_hi, wr_lo, rbias)


def _split3_f32(w):
    hi = w.astype(BF16).astype(F32)
    r1 = w - hi
    mid = r1.astype(BF16).astype(F32)
    lo = (r1 - mid).astype(BF16).astype(F32)
    return hi, mid, lo


def _route_sort_kernel(rt_ref, ltri_ref, utri_ref, lists_ref, meta_ref):
    rt = rt_ref[...]
    bt = rt.shape[0]
    lane = lax.broadcasted_iota(jnp.int32, rt.shape, 1)
    lane_f = lane.astype(F32)
    oh1 = lane_f == rt[:, 0:1]
    oh2 = lane_f == rt[:, 1:2]
    oh = jnp.where(oh1, 1.0, jnp.where(oh2, 1.0, 0.0))
    oh_b = oh.astype(BF16)
    rank = jnp.dot(ltri_ref[...], oh_b, preferred_element_type=F32)
    below = jnp.dot(oh_b, utri_ref[...], preferred_element_type=F32)
    offs = jnp.sum(below, axis=0, keepdims=True)
    cnt = jnp.sum(oh, axis=0, keepdims=True)
    posmat = rank + offs
    pos1 = jnp.sum(jnp.where(oh1, posmat, 0.0), axis=1, keepdims=True)
    pos2 = jnp.sum(jnp.where(oh2, posmat, 0.0), axis=1, keepdims=True)
    tok = lax.broadcasted_iota(jnp.int32, rt.shape, 0)
    tok_hi = (tok >> 5).astype(F32)
    tok_lo = (tok & 31).astype(F32)

    def record(w, slot):
        hi, mid, lo = _split3_f32(w)
        d = jnp.where(lane == 0, tok_hi, jnp.where(lane == 1, tok_lo, jnp.where(
            lane == 2, hi, jnp.where(lane == 3, mid, jnp.where(lane == 4, lo, jnp.where(
                lane == 5, slot, 0.0))))))
        return d.T.astype(BF16)

    d1 = record(rt[:, 2:3], 0.0)
    d2 = record(rt[:, 3:4], 1.0)
    cw = 512
    for c in range(2 * bt // cw):
        colp = (lax.broadcasted_iota(jnp.int32, (bt, cw), 1) + c * cw).astype(F32)
        o1 = jnp.where(colp == pos1, 1.0, 0.0).astype(BF16)
        o2 = jnp.where(colp == pos2, 1.0, 0.0).astype(BF16)
        srt = (jnp.dot(d1, o1, preferred_element_type=F32)
               + jnp.dot(d2, o2, preferred_element_type=F32))
        lists_ref[0, :, c * cw:(c + 1) * cw] = srt[0:8]
    row = lax.broadcasted_iota(jnp.int32, (8, LANES), 0)
    meta_ref[0] = jnp.where(row == 0, cnt, jnp.where(row == 1, offs, 0.0))


def _route_sort(route, bt):
    T = route.shape[0]
    nb = T // bt
    ltri = jnp.asarray(np.tril(np.ones((bt, bt), np.float32), -1), BF16)
    utri = jnp.asarray(np.triu(np.ones((LANES, LANES), np.float32), 1), BF16)
    return pl.pallas_call(
        _route_sort_kernel,
        grid=(nb,),
        in_specs=[
            pl.BlockSpec((bt, ROUTER_LANES), lambda b: (b, 0)),
            pl.BlockSpec((bt, bt), lambda b: (0, 0)),
            pl.BlockSpec((LANES, LANES), lambda b: (0, 0)),
        ],
        out_specs=[
            pl.BlockSpec((1, 8, 2 * bt), lambda b: (b, 0, 0)),
            pl.BlockSpec((1, 8, LANES), lambda b: (b, 0, 0)),
        ],
        out_shape=[
            jax.ShapeDtypeStruct((nb, 8, 2 * bt), F32),
            jax.ShapeDtypeStruct((nb, 8, LANES), F32),
        ],
        compiler_params=_cparams(("parallel",)),
        name="route_sort",
    )(route, ltri, utri)


def _moe_kernel(cnt_ref, off_ref, idx_ref, ws_ref, xt_ref, hres_ref, wg_ref, wu_ref, wd_ref, gf_ref,
                o_ref, yb_ref, xf_ref, xs_ref, y_ref):
    b = pl.program_id(0)
    e = pl.program_id(1)
    bt = xt_ref.shape[0]
    rows = xs_ref.shape[0]

    @pl.when(e == 0)
    def _():
        xf_ref[...] = xt_ref[...].astype(F32)
        xs_ref[...] = jnp.zeros_like(xs_ref)

    n = cnt_ref[b * N_EXPERTS + e]
    base = b * 2 * bt + off_ref[b * N_EXPERTS + e]

    def tile(r, carry):
        r0 = base + r * rows
        m = jnp.minimum(n - r * rows, rows)
        trips = (m + MOE_UNROLL - 1) // MOE_UNROLL

        def gather(i, c):
            for k in range(MOE_UNROLL):
                j = i * MOE_UNROLL + k
                t = idx_ref[r0 + j] & (bt - 1)
                xs_ref[pl.ds(j, 1), :] = xf_ref[pl.ds(t, 1), :]
            return c

        lax.fori_loop(0, trips, gather, 0)
        xs = xs_ref[...].astype(BF16)
        a = jnp.dot(xs, wg_ref[0], preferred_element_type=F32)
        u = jnp.dot(xs, wu_ref[0], preferred_element_type=F32)
        hid = (a * jax.nn.sigmoid(a)) * u
        y_ref[...] = jnp.dot(hid.astype(BF16), wd_ref[0], preferred_element_type=F32)

        def scatter(i, c):
            for k in range(MOE_UNROLL):
                j = i * MOE_UNROLL + k
                dst = jnp.where(j < m, idx_ref[r0 + j], 2 * bt + k)
                yb_ref[pl.ds(dst, 1), :] = ws_ref[r0 + j] * y_ref[pl.ds(j, 1), :]
            return c

        lax.fori_loop(0, trips, scatter, 0)
        return carry

    lax.fori_loop(0, (n + rows - 1) // rows, tile, 0)

    @pl.when(e == pl.num_programs(1) - 1)
    def _():
        y = hres_ref[...] + yb_ref[0:bt, :] + yb_ref[bt:2 * bt, :]
        ms = jnp.mean(y * y, axis=-1, keepdims=True)
        o_ref[...] = y * lax.rsqrt(ms + NORM_EPS) * gf_ref[...]


def _moe(cnt, off, idx, ws, xt, hres, wg, wu, wd, g_final, bt):
    T = xt.shape[0]
    row = lambda i, e, *_: (i, 0)
    grid_spec = pltpu.PrefetchScalarGridSpec(
        num_scalar_prefetch=4,
        grid=(T // bt, N_EXPERTS),
        in_specs=[
            pl.BlockSpec((bt, D_MODEL), row),
            pl.BlockSpec((bt, D_MODEL), row),
            pl.BlockSpec((1, D_MODEL, D_EXPERT), lambda i, e, *_: (e, 0, 0)),
            pl.BlockSpec((1, D_MODEL, D_EXPERT), lambda i, e, *_: (e, 0, 0)),
            pl.BlockSpec((1, D_EXPERT, D_MODEL), lambda i, e, *_: (e, 0, 0)),
            pl.BlockSpec((1, D_MODEL), lambda i, e, *_: (0, 0)),
        ],
        out_specs=pl.BlockSpec((bt, D_MODEL), row),
        scratch_shapes=[
            pltpu.VMEM((2 * bt + MOE_UNROLL, D_MODEL), F32),
            pltpu.VMEM((bt, D_MODEL), F32),
            pltpu.VMEM((MOE_ROWS, D_MODEL), F32),
            pltpu.VMEM((MOE_ROWS, D_MODEL), F32),
        ],
    )
    return pl.pallas_call(
        _moe_kernel,
        grid_spec=grid_spec,
        out_shape=jax.ShapeDtypeStruct((T, D_MODEL), F32),
        compiler_params=_cparams(("parallel", "arbitrary")),
        name="moe",
    )(cnt, off, idx, ws, xt, hres, wg, wu, wd, g_final)


def _rope_tables(T):
    inv = ROPE_THETA ** (-jnp.arange(0, ATT_QKDIM, 2, dtype=F32) / ATT_QKDIM)
    ang = jnp.arange(T, dtype=F32)[:, None] * inv[None, :]
    c, s = jnp.cos(ang), jnp.sin(ang)
    return (jnp.concatenate([c, c, c, c], axis=1), jnp.concatenate([-s, s, -s, s], axis=1),
            c.T, s.T)


def kernel(x, w_in, w_out, g_mix, lam_params, subln_g, hgrn_gnorm_g, hgrn_lb, g_ffn, w_gr, b_gr,
           w_er, b_er, w_gate, w_up, w_down, g_final):
    B, T, D = x.shape
    x2 = x.reshape(B * T, D)
    l = 0
    w_in_bf = w_in[l].astype(BF16)
    w_out_bf = w_out[l].astype(BF16)
    lp = lam_params[l].astype(F32)
    lam = (jnp.exp(jnp.sum(lp[0] * lp[1])) - jnp.exp(jnp.sum(lp[2] * lp[3])) + LAMBDA_INIT).reshape(1, 1)
    lb = jnp.cumsum(jax.nn.softmax(hgrn_lb.astype(F32), axis=1), axis=1)[:, l]
    w_qv_t = jnp.concatenate([w_in[l][:, 0:ATT_WIDTH], w_in[l][:, 2 * ATT_WIDTH:3 * ATT_WIDTH]],
                             axis=1).T.astype(BF16)
    cos_t, sin_t, cos_tt, sin_tt = _rope_tables(T)
    w_r = jnp.concatenate([jnp.transpose(w_er[l], (1, 0, 2)).reshape(D, N_EXPERTS), w_gr[l],
                           jnp.zeros((D, ROUTER_LANES - N_EXPERTS - N_GROUPS), F32)], axis=1)
    wr_hi = w_r.astype(BF16)
    wr_lo = (w_r - wr_hi.astype(F32)).astype(BF16)
    rbias = jnp.concatenate([b_er[l].reshape(-1), b_gr[l],
                             jnp.zeros((ROUTER_LANES - N_EXPERTS - N_GROUPS,), F32)]).reshape(1, -1)
    wg = w_gate[l].reshape(N_EXPERTS, D, D_EXPERT).astype(BF16)
    wu = w_up[l].reshape(N_EXPERTS, D, D_EXPERT).astype(BF16)
    wd = w_down[l].reshape(N_EXPERTS, D_EXPERT, D).astype(BF16)

    tk = min(ATT_TK, (B * T) // 2)
    qt, k, vt, hgrn_in = _in_proj(x2, g_mix[l].reshape(1, D), w_in_bf, w_qv_t, cos_t, sin_t,
                                  cos_tt, sin_tt, tk)
    att = _diff_attn(lam, qt, k, vt, subln_g[l].reshape(-1, 1))
    o_b = _hgrn_pass(hgrn_in, lb[1:2], backward=True)
    rec = _hgrn_pass(hgrn_in, lb[0:1], backward=False, o_b=o_b,
                     gnorm_g=hgrn_gnorm_g[l].reshape(1, -1))
    hres, xt, route = _out_proj(att, rec, x2, w_out_bf, g_ffn[l].reshape(1, D), wr_hi, wr_lo, rbias)
    bt = min(MOE_BT, B * T)
    lists, meta = _route_sort(route, bt)
    pad = jnp.zeros((MOE_UNROLL,), F32)
    idx = jnp.concatenate([(lists[:, 5] * bt + lists[:, 0] * 32.0 + lists[:, 1]).reshape(-1),
                           pad]).astype(jnp.int32)
    ws = jnp.concatenate([(lists[:, 2] + lists[:, 3] + lists[:, 4]).reshape(-1), pad])
    cnt = meta[:, 0, :N_EXPERTS].astype(jnp.int32).reshape(-1)
    off = meta[:, 1, :N_EXPERTS].astype(jnp.int32).reshape(-1)
    out = _moe(cnt, off, idx, ws, xt, hres, wg, wu, wd, g_final.reshape(1, D), bt)
    return out.reshape(B, T, D)
```

```python
import functools
import math

import numpy as np
import jax
import jax.numpy as jnp
from jax import lax
from jax.experimental import pallas as pl
from jax.experimental.pallas import tpu as pltpu

D_MODEL = 1024
ATT_WIDTH = 512
ATT_HEADS = 4
ATT_VDIM = 128
ATT_QKDIM = 64
HGRN_WIDTH = 512
HGRN_HEADS = 4
HGRN_VDIM = 128
HGRN_EXPAND = 128
HGRN_FDIM = 512
N_GROUPS = 4
EXPERTS_PER_GROUP = 8
N_EXPERTS = N_GROUPS * EXPERTS_PER_GROUP
D_EXPERT = 512
ROPE_THETA = 10000.0
NORM_EPS = 1e-6
SUBLN_EPS = 1e-5
LAMBDA_INIT = 0.8 - 0.6 * math.exp(-0.3 * 0)
LOG2E = math.log2(math.e)
IN_COLS = 3 * ATT_WIDTH + 3 * HGRN_FDIM + 2 * HGRN_WIDTH
HGRN_COLS = IN_COLS - 3 * ATT_WIDTH

LANES = 128
VMEM_LIMIT = 56 * 1024 * 1024
HGRN_CHUNK = 64
IN_TM = 256
ATT_TK = 1024
ATT_CB = 256
ATT_KT = 256
MOE_BT = 1024
MOE_ROWS = 128
MOE_UNROLL = 8
ATT_VPAD = ATT_VDIM + 16
ROUTER_LANES = LANES

BF16 = jnp.bfloat16
F32 = jnp.float32


def _cparams(sem):
    return pltpu.CompilerParams(dimension_semantics=sem, vmem_limit_bytes=VMEM_LIMIT)


def _rot_half_64(x):
    lane = lax.broadcasted_iota(jnp.int32, x.shape, 1)
    fwd = pltpu.roll(x, 32, axis=1)
    bwd = pltpu.roll(x, 96, axis=1)
    return jnp.where((lane & 63) < 32, bwd, fwd)


def _inproj_kernel(x_ref, g_ref, w_ref, wqvt_ref, cos_ref, sin_ref, cost_ref, sint_ref,
                   qt_ref, k_ref, vt_ref, hg_ref):
    x = x_ref[...]
    tm = x.shape[0]
    ms = jnp.mean(x * x, axis=-1, keepdims=True)
    h = (x * lax.rsqrt(ms + NORM_EPS) * g_ref[...]).astype(BF16)
    nt = (((1,), (1,)), ((), ()))
    half = ATT_QKDIM // 2
    ct = cost_ref[...] * (ATT_QKDIM ** -0.5 * LOG2E)
    st = sint_ref[...] * (ATT_QKDIM ** -0.5 * LOG2E)
    for j in range(ATT_WIDTH // LANES):
        a = lax.dot_general(wqvt_ref[j * LANES:(j + 1) * LANES, :], h, nt,
                            preferred_element_type=F32)
        for c in range(LANES // ATT_QKDIM):
            x1 = a[c * ATT_QKDIM:c * ATT_QKDIM + half]
            x2 = a[c * ATT_QKDIM + half:(c + 1) * ATT_QKDIM]
            lo = j * LANES + c * ATT_QKDIM
            qt_ref[lo:lo + half, :] = (x1 * ct - x2 * st).astype(BF16)
            qt_ref[lo + half:lo + ATT_QKDIM, :] = (x1 * st + x2 * ct).astype(BF16)
    ones = jnp.ones((ATT_VPAD - ATT_VDIM, tm), BF16)
    for j in range(ATT_HEADS):
        lo = ATT_WIDTH + j * ATT_VDIM
        vt = lax.dot_general(wqvt_ref[lo:lo + ATT_VDIM, :], h, nt, preferred_element_type=F32)
        vt_ref[0, j * ATT_VPAD:j * ATT_VPAD + ATT_VDIM, :] = vt.astype(BF16)
        vt_ref[0, j * ATT_VPAD + ATT_VDIM:(j + 1) * ATT_VPAD, :] = ones
    cosv = cos_ref[...]
    sinv = sin_ref[...]
    for j in range(ATT_WIDTH // LANES):
        lo = ATT_WIDTH + j * LANES
        a = jnp.dot(h, w_ref[:, lo:lo + LANES], preferred_element_type=F32)
        k_ref[:, j * LANES:(j + 1) * LANES] = (a * cosv + _rot_half_64(a) * sinv).astype(BF16)
    for j in range(HGRN_COLS // 512):
        lo = 3 * ATT_WIDTH + j * 512
        hg_ref[:, j * 512:(j + 1) * 512] = jnp.dot(h, w_ref[:, lo:lo + 512],
                                                   preferred_element_type=F32)


def _in_proj(x2, g_mix, w_in_bf, w_qv_t, cos_t, sin_t, cos_tt, sin_tt, tk):
    T = x2.shape[0]
    tm = IN_TM
    per = tk // tm
    return pl.pallas_call(
        _inproj_kernel,
        grid=(T // tm,),
        in_specs=[
            pl.BlockSpec((tm, D_MODEL), lambda i: (i, 0)),
            pl.BlockSpec((1, D_MODEL), lambda i: (0, 0)),
            pl.BlockSpec((D_MODEL, IN_COLS), lambda i: (0, 0)),
            pl.BlockSpec((2 * ATT_WIDTH, D_MODEL), lambda i: (0, 0)),
            pl.BlockSpec((tm, LANES), lambda i: (i, 0)),
            pl.BlockSpec((tm, LANES), lambda i: (i, 0)),
            pl.BlockSpec((ATT_QKDIM // 2, tm), lambda i: (0, i)),
            pl.BlockSpec((ATT_QKDIM // 2, tm), lambda i: (0, i)),
        ],
        out_specs=[
            pl.BlockSpec((ATT_WIDTH, tm), lambda i: (0, i)),
            pl.BlockSpec((tm, ATT_WIDTH), lambda i: (i, 0)),
            pl.BlockSpec((1, ATT_HEADS * ATT_VPAD, tm), lambda i: (i // per, 0, i % per)),
            pl.BlockSpec((tm, HGRN_COLS), lambda i: (i, 0)),
        ],
        out_shape=[
            jax.ShapeDtypeStruct((ATT_WIDTH, T), BF16),
            jax.ShapeDtypeStruct((T, ATT_WIDTH), BF16),
            jax.ShapeDtypeStruct((T // tk, ATT_HEADS * ATT_VPAD, tk), BF16),
            jax.ShapeDtypeStruct((T, HGRN_COLS), F32),
        ],
        compiler_params=_cparams(("parallel",)),
        name="in_proj",
    )(x2, g_mix, w_in_bf, w_qv_t, cos_t, sin_t, cos_tt, sin_tt)


def _attn_kernel(lam_ref, qt_ref, k_ref, vt_ref, g_ref, o_ref, qw_ref, sa_ref, sb_ref, acc_ref):
    tq = qt_ref.shape[1]
    nchunk, _, tk = vt_ref.shape
    qt = qt_ref[...].astype(F32)
    row = lax.broadcasted_iota(jnp.int32, qt.shape, 0)
    qw_ref[:, 0:tq] = jnp.where(row < ATT_QKDIM, qt, 0.0).astype(BF16)
    qw_ref[:, tq:] = jnp.where(row >= ATT_QKDIM, qt, 0.0).astype(BF16)
    acc_ref[...] = jnp.zeros_like(acc_ref)
    ncb = 2 * tq // ATT_CB

    def stage(j_acc, s_acc, m_cur, alpha, j_sc, s_sc, m_old):
        off = pl.multiple_of(j_sc * tk, tk)
        m_new, al_new = [], []
        for cb in range(ncb):
            cols = slice(cb * ATT_CB, (cb + 1) * ATT_CB)
            part, cmax = None, None
            for t in range(tk // ATT_KT):
                rows = slice(t * ATT_KT, (t + 1) * ATT_KT)
                kc = k_ref[pl.ds(off + t * ATT_KT, ATT_KT), :]
                s = jnp.dot(kc, qw_ref[:, cols], preferred_element_type=F32)
                s_sc[rows, cols] = s
                c = jnp.max(s, axis=0, keepdims=True)
                cmax = c if cmax is None else jnp.maximum(cmax, c)
                if s_acc is not None:
                    p = jnp.exp2(s_acc[rows, cols] - m_cur[cb]).astype(BF16)
                    d = jnp.dot(vt_ref[j_acc, :, rows], p, preferred_element_type=F32)
                    part = d if part is None else part + d
            if s_acc is not None:
                acc_ref[:, cols] = alpha[cb] * acc_ref[:, cols] + part
            mn = jnp.maximum(m_old[cb], cmax)
            m_new.append(mn)
            al_new.append(jnp.exp2(m_old[cb] - mn))
        return tuple(m_new), tuple(al_new)

    m0 = tuple(jnp.full((1, ATT_CB), -jnp.inf, F32) for _ in range(ncb))
    state = stage(0, None, None, None, 0, sa_ref, m0)

    def body(i, st):
        m_a, al_a = st
        j = 2 * i
        m_b, al_b = stage(j, sa_ref, m_a, al_a, j + 1, sb_ref, m_a)
        return stage(j + 1, sb_ref, m_b, al_b, jnp.minimum(j + 2, nchunk - 1), sa_ref, m_b)

    lax.fori_loop(0, nchunk // 2, body, state)
    lam = lam_ref[0, 0]
    acc = acc_ref[...]
    o1 = acc[0:ATT_VDIM, 0:tq] / acc[ATT_VDIM:ATT_VDIM + 1, 0:tq]
    o2 = acc[0:ATT_VDIM, tq:] / acc[ATT_VDIM:ATT_VDIM + 1, tq:]
    o = o1 - lam * o2
    ms = jnp.mean(o * o, axis=0, keepdims=True)
    o = o * lax.rsqrt(ms + SUBLN_EPS) * g_ref[...] * (1.0 - LAMBDA_INIT)
    o_ref[...] = o.T.astype(o_ref.dtype)


def _diff_attn(lam, qt, k, vt, subln_g, tq=512):
    T = k.shape[0]
    nchunk, _, tk = vt.shape
    return pl.pallas_call(
        _attn_kernel,
        grid=(ATT_HEADS, T // tq),
        in_specs=[
            pl.BlockSpec(memory_space=pltpu.SMEM),
            pl.BlockSpec((LANES, tq), lambda h, i: (h, i)),
            pl.BlockSpec((T, LANES), lambda h, i: (0, h)),
            pl.BlockSpec((nchunk, ATT_VPAD, tk), lambda h, i: (0, h, 0)),
            pl.BlockSpec((ATT_VDIM, 1), lambda h, i: (0, 0)),
        ],
        out_specs=pl.BlockSpec((tq, LANES), lambda h, i: (i, h)),
        out_shape=jax.ShapeDtypeStruct((T, ATT_WIDTH), BF16),
        scratch_shapes=[
            pltpu.VMEM((LANES, 2 * tq), BF16),
            pltpu.VMEM((tk, 2 * tq), F32),
            pltpu.VMEM((tk, 2 * tq), F32),
            pltpu.VMEM((ATT_VPAD, 2 * tq), F32),
        ],
        compiler_params=_cparams(("parallel", "parallel")),
        name="diff_attn",
    )(lam, qt, k, vt, subln_g)


def _hgrn_consts(C, backward):
    halves = []
    h = C // 2
    while h >= 1:
        halves.append(h)
        h //= 2
    E = np.zeros((2 + len(halves), C, C), np.float32)
    M = np.zeros((len(halves) + 1, C, C), np.float32)
    idx = np.arange(C)
    for t in range(C):
        if not backward:
            E[0, t, idx <= t] = 1.0
            E[1, t, idx > t] = 1.0
        else:
            E[0, t, idx >= t] = 1.0
            E[1, t, idx < t] = 1.0
    for li, h in enumerate(halves):
        for t in range(C):
            mid = (t // (2 * h)) * 2 * h + h
            upper = t >= mid
            if not backward:
                if upper:
                    E[2 + li, t, (idx >= mid) & (idx <= t)] = 1.0
                else:
                    E[2 + li, t, (idx > t) & (idx <= mid - 1)] = 1.0
            else:
                if not upper:
                    E[2 + li, t, (idx >= t) & (idx <= mid - 1)] = 1.0
                else:
                    E[2 + li, t, (idx >= mid) & (idx <= t - 1)] = 1.0
            for s in range(C):
                same = (s // (2 * h)) == (t // (2 * h))
                s_upper = s >= mid
                if same and ((not backward and upper and not s_upper)
                             or (backward and not upper and s_upper)):
                    M[li, t, s] = 1.0
    M[-1] = np.eye(C, dtype=np.float32)
    E = np.concatenate([E.reshape(-1, C), np.ones((8, C), np.float32)], axis=0)
    return E, M


def _split3(g):
    hi = g.astype(BF16)
    r1 = g - hi.astype(F32)
    mid = r1.astype(BF16)
    lo = (r1 - mid.astype(F32)).astype(BF16)
    return hi, mid, lo


def _hgrn_kernel(*refs, backward, nchunk, final):
    if final:
        (hq_ref, hf_ref, hi_ref, lb_ref, e_ref, m_ref, ob_ref, hgate_ref, gn_ref,
         o_ref, st_ref) = refs
    else:
        hq_ref, hf_ref, hi_ref, lb_ref, e_ref, m_ref, o_ref, st_ref = refs
    C = HGRN_CHUNK
    nlev = m_ref.shape[0] - 1
    nt = (((1,), (1,)), ((), ()))

    @pl.when(pl.program_id(0) == 0)
    def _():
        st_ref[...] = jnp.zeros_like(st_ref)

    lbd = lb_ref[...]
    emat = e_ref[...]
    order = range(nchunk - 1, -1, -1) if backward else range(nchunk)
    for c in order:
        rows = pl.ds(c * C, C)
        z = hf_ref[rows, :]
        hq = hq_ref[rows, :]
        f = lbd + (1.0 - lbd) * jax.nn.sigmoid(z)
        kk = (1.0 - lbd) * jax.nn.sigmoid(-z)
        g = jnp.log(f)
        q = hq * jax.nn.sigmoid(hq)
        vb = hi_ref[rows, :].astype(BF16)
        ghi, gmid, glo = _split3(g)
        ex = (jnp.dot(emat, ghi, preferred_element_type=F32)
              + jnp.dot(emat, gmid, preferred_element_type=F32)
              + jnp.dot(emat, glo, preferred_element_type=F32))
        ee = jnp.exp(ex)
        qi = (q * ee[0:C]).astype(BF16)
        ki = (kk * ee[C:2 * C]).astype(BF16)
        dec = ee[(2 + nlev) * C:(2 + nlev) * C + 1]
        qb = q.astype(BF16)
        kb = kk.astype(BF16)
        qlev = [(q * ee[(2 + l) * C:(3 + l) * C]).astype(BF16) for l in range(nlev)]
        klev = [(kk * ee[(2 + l) * C:(3 + l) * C]).astype(BF16) for l in range(nlev)]
        outs = []
        for hd in range(HGRN_HEADS):
            ln = slice(hd * LANES, (hd + 1) * LANES)
            a = m_ref[nlev] * lax.dot_general(qb[:, ln], kb[:, ln], nt, preferred_element_type=F32)
            for l in range(nlev):
                a = a + m_ref[l] * lax.dot_general(qlev[l][:, ln], klev[l][:, ln], nt,
                                                   preferred_element_type=F32)
            st = st_ref[hd]
            o = jnp.dot(a.astype(BF16), vb[:, ln], preferred_element_type=F32)
            o = o + lax.dot_general(qi[:, ln], st.astype(BF16), nt, preferred_element_type=F32)
            vt = hi_ref[rows, ln].T.astype(BF16)
            st_ref[hd] = st * dec[:, ln] + jnp.dot(vt, ki[:, ln], preferred_element_type=F32)
            outs.append(o)
        o_all = jnp.concatenate(outs, axis=1)
        if final:
            o_all = o_all + ob_ref[rows, :]
            res = []
            for hd in range(HGRN_HEADS):
                ln = slice(hd * LANES, (hd + 1) * LANES)
                oh = o_all[:, ln]
                ms = jnp.mean(oh * oh, axis=-1, keepdims=True)
                res.append(oh * lax.rsqrt(ms + NORM_EPS) * gn_ref[...])
            hg = hgate_ref[rows, :]
            o_all = jnp.concatenate(res, axis=1) * (hg * jax.nn.sigmoid(hg))
        o_ref[rows, :] = o_all.astype(o_ref.dtype)


def _hgrn_pass(hgrn_in, lb_row, backward, o_b=None, gnorm_g=None, tb=256):
    T = hgrn_in.shape[0]
    nblk = T // tb
    final = o_b is not None
    E, M = _hgrn_consts(HGRN_CHUNK, backward)
    e_bf = jnp.asarray(E, BF16)
    m_f = jnp.asarray(M, F32)
    blk = (lambda i: nblk - 1 - i) if backward else (lambda i: i)
    f_part = 2 if backward else 1
    in_specs = [
        pl.BlockSpec((tb, 512), lambda i: (blk(i), 0)),
        pl.BlockSpec((tb, 512), lambda i: (blk(i), f_part)),
        pl.BlockSpec((tb, 512), lambda i: (blk(i), 3)),
        pl.BlockSpec((1, 512), lambda i: (0, 0)),
        pl.BlockSpec(e_bf.shape, lambda i: (0, 0)),
        pl.BlockSpec(m_f.shape, lambda i: (0, 0, 0)),
    ]
    args = [hgrn_in, hgrn_in, hgrn_in, lb_row, e_bf, m_f]
    if final:
        in_specs += [
            pl.BlockSpec((tb, 512), lambda i: (blk(i), 0)),
            pl.BlockSpec((tb, 512), lambda i: (blk(i), 4)),
            pl.BlockSpec((1, HGRN_VDIM), lambda i: (0, 0)),
        ]
        args += [o_b, hgrn_in, gnorm_g]
    return pl.pallas_call(
        functools.partial(_hgrn_kernel, backward=backward, nchunk=tb // HGRN_CHUNK, final=final),
        grid=(nblk,),
        in_specs=in_specs,
        out_specs=pl.BlockSpec((tb, 512), lambda i: (blk(i), 0)),
        out_shape=jax.ShapeDtypeStruct((T, HGRN_WIDTH), BF16 if final else F32),
        scratch_shapes=[pltpu.VMEM((HGRN_HEADS, HGRN_VDIM, HGRN_EXPAND), F32)],
        compiler_params=_cparams(("arbitrary",)),
        name="hgrn_bwd" if backward else "hgrn_fwd",
    )(*args)


def _outproj_kernel(att_ref, rec_ref, x_ref, wo_ref, g_ref, wr_hi_ref, wr_lo_ref, rb_ref,
                    hres_ref, xt_ref, comb_ref):
    mix = (jnp.dot(att_ref[...], wo_ref[0:ATT_WIDTH, :], preferred_element_type=F32)
           + jnp.dot(rec_ref[...], wo_ref[ATT_WIDTH:, :], preferred_element_type=F32))
    hres = x_ref[...] + mix
    hres_ref[...] = hres
    ms = jnp.mean(hres * hres, axis=-1, keepdims=True)
    xt = hres * lax.rsqrt(ms + NORM_EPS) * g_ref[...]
    xt_hi = xt.astype(BF16)
    xt_ref[...] = xt_hi
    xt_lo = (xt - xt_hi.astype(F32)).astype(BF16)
    logits = (jnp.dot(xt_hi, wr_hi_ref[...], preferred_element_type=F32)
              + jnp.dot(xt_hi, wr_lo_ref[...], preferred_element_type=F32)
              + jnp.dot(xt_lo, wr_hi_ref[...], preferred_element_type=F32)) + rb_ref[...]
    lane = lax.broadcasted_iota(jnp.int32, logits.shape, 1)
    neg = -jnp.inf
    big = jnp.int32(1 << 20)
    gmask = (lane >= N_EXPERTS) & (lane < N_EXPERTS + N_GROUPS)
    glog = jnp.where(gmask, logits, neg)
    gmax = jnp.max(glog, axis=-1, keepdims=True)
    gsum = jnp.sum(jnp.exp(glog - gmax), axis=-1, keepdims=True)
    p_g = 1.0 / gsum
    g_idx = jnp.min(jnp.where(glog == gmax, lane, big), axis=-1, keepdims=True) - N_EXPERTS
    emask = (lane < N_EXPERTS) & ((lane >> 3) == g_idx)
    elog = jnp.where(emask, logits, neg)
    e1 = jnp.max(elog, axis=-1, keepdims=True)
    i1 = jnp.min(jnp.where(elog == e1, lane, big), axis=-1, keepdims=True)
    elog2 = jnp.where(lane == i1, neg, elog)
    e2 = jnp.max(elog2, axis=-1, keepdims=True)
    i2 = jnp.min(jnp.where(elog2 == e2, lane, big), axis=-1, keepdims=True)
    r = jnp.exp(e2 - e1)
    w1 = p_g / (1.0 + r)
    w2 = p_g * r / (1.0 + r)
    comb_ref[...] = jnp.where(lane == 0, i1.astype(F32), jnp.where(
        lane == 1, i2.astype(F32), jnp.where(lane == 2, w1, jnp.where(lane == 3, w2, 0.0))))


def _out_proj(att, rec, x2, w_out_bf, g_ffn, wr_hi, wr_lo, rbias, tm=512):
    T = x2.shape[0]
    row = lambda i: (i, 0)
    fixed = lambda i: (0, 0)
    return pl.pallas_call(
        _outproj_kernel,
        grid=(T // tm,),
        in_specs=[
            pl.BlockSpec((tm, ATT_WIDTH), row),
            pl.BlockSpec((tm, HGRN_WIDTH), row),
            pl.BlockSpec((tm, D_MODEL), row),
            pl.BlockSpec((D_MODEL, D_MODEL), fixed),
            pl.BlockSpec((1, D_MODEL), fixed),
            pl.BlockSpec((D_MODEL, ROUTER_LANES), fixed),
            pl.BlockSpec((D_MODEL, ROUTER_LANES), fixed),
            pl.BlockSpec((1, ROUTER_LANES), fixed),
        ],
        out_specs=[
            pl.BlockSpec((tm, D_MODEL), row),
            pl.BlockSpec((tm, D_MODEL), row),
            pl.BlockSpec((tm, ROUTER_LANES), row),
        ],
        out_shape=[
            jax.ShapeDtypeStruct((T, D_MODEL), F32),
            jax.ShapeDtypeStruct((T, D_MODEL), BF16),
            jax.ShapeDtypeStruct((T, ROUTER_LANES), F32),
        ],
        compiler_params=_cparams(("parallel",)),
        name="out_proj_router",
    )(att, rec, x2, w_out_bf, g_ffn, wr_hi, wr_lo, rbias)


def _split3_f32(w):
    hi = w.astype(BF16).astype(F32)
    r1 = w - hi
    mid = r1.astype(BF16).astype(F32)
    lo = (r1 - mid).astype(BF16).astype(F32)
    return hi, mid, lo


def _route_sort_kernel(rt_ref, ltri_ref, utri_ref, lists_ref, meta_ref):
    rt = rt_ref[...]
    bt = rt.shape[0]
    lane = lax.broadcasted_iota(jnp.int32, rt.shape, 1)
    lane_f = lane.astype(F32)
    oh1 = lane_f == rt[:, 0:1]
    oh2 = lane_f == rt[:, 1:2]
    oh = jnp.where(oh1, 1.0, jnp.where(oh2, 1.0, 0.0))
    oh_b = oh.astype(BF16)
    rank = jnp.dot(ltri_ref[...], oh_b, preferred_element_type=F32)
    below = jnp.dot(oh_b, utri_ref[...], preferred_element_type=F32)
    offs = jnp.sum(below, axis=0, keepdims=True)
    cnt = jnp.sum(oh, axis=0, keepdims=True)
    posmat = rank + offs
    pos1 = jnp.sum(jnp.where(oh1, posmat, 0.0), axis=1, keepdims=True)
    pos2 = jnp.sum(jnp.where(oh2, posmat, 0.0), axis=1, keepdims=True)
    tok = lax.broadcasted_iota(jnp.int32, rt.shape, 0)
    tok_hi = (tok >> 5).astype(F32)
    tok_lo = (tok & 31).astype(F32)

    def record(w, slot):
        hi, mid, lo = _split3_f32(w)
        d = jnp.where(lane == 0, tok_hi, jnp.where(lane == 1, tok_lo, jnp.where(
            lane == 2, hi, jnp.where(lane == 3, mid, jnp.where(lane == 4, lo, jnp.where(
                lane == 5, slot, 0.0))))))
        return d.T.astype(BF16)

    d1 = record(rt[:, 2:3], 0.0)
    d2 = record(rt[:, 3:4], 1.0)
    cw = 512
    for c in range(2 * bt // cw):
        colp = (lax.broadcasted_iota(jnp.int32, (bt, cw), 1) + c * cw).astype(F32)
        o1 = jnp.where(colp == pos1, 1.0, 0.0).astype(BF16)
        o2 = jnp.where(colp == pos2, 1.0, 0.0).astype(BF16)
        srt = (jnp.dot(d1, o1, preferred_element_type=F32)
               + jnp.dot(d2, o2, preferred_element_type=F32))
        lists_ref[0, :, c * cw:(c + 1) * cw] = srt[0:8]
    row = lax.broadcasted_iota(jnp.int32, (8, LANES), 0)
    meta_ref[0] = jnp.where(row == 0, cnt, jnp.where(row == 1, offs, 0.0))


def _route_sort(route, bt):
    T = route.shape[0]
    nb = T // bt
    ltri = jnp.asarray(np.tril(np.ones((bt, bt), np.float32), -1), BF16)
    utri = jnp.asarray(np.triu(np.ones((LANES, LANES), np.float32), 1), BF16)
    return pl.pallas_call(
        _route_sort_kernel,
        grid=(nb,),
        in_specs=[
            pl.BlockSpec((bt, ROUTER_LANES), lambda b: (b, 0)),
            pl.BlockSpec((bt, bt), lambda b: (0, 0)),
            pl.BlockSpec((LANES, LANES), lambda b: (0, 0)),
        ],
        out_specs=[
            pl.BlockSpec((1, 8, 2 * bt), lambda b: (b, 0, 0)),
            pl.BlockSpec((1, 8, LANES), lambda b: (b, 0, 0)),
        ],
        out_shape=[
            jax.ShapeDtypeStruct((nb, 8, 2 * bt), F32),
            jax.ShapeDtypeStruct((nb, 8, LANES), F32),
        ],
        compiler_params=_cparams(("parallel",)),
        name="route_sort",
    )(route, ltri, utri)


def _moe_kernel(cnt_ref, off_ref, idx_ref, ws_ref, xt_ref, hres_ref, wg_ref, wu_ref, wd_ref, gf_ref,
                o_ref, yb_ref, xf_ref, xs_ref, y_ref):
    b = pl.program_id(0)
    e = pl.program_id(1)
    bt = xt_ref.shape[0]
    rows = xs_ref.shape[0]

    @pl.when(e == 0)
    def _():
        xf_ref[...] = xt_ref[...].astype(F32)
        xs_ref[...] = jnp.zeros_like(xs_ref)

    n = cnt_ref[b * N_EXPERTS + e]
    base = b * 2 * bt + off_ref[b * N_EXPERTS + e]

    def tile(r, carry):
        r0 = base + r * rows
        m = jnp.minimum(n - r * rows, rows)
        trips = (m + MOE_UNROLL - 1) // MOE_UNROLL

        def gather(i, c):
            for k in range(MOE_UNROLL):
                j = i * MOE_UNROLL + k
                t = idx_ref[r0 + j] & (bt - 1)
                xs_ref[pl.ds(j, 1), :] = xf_ref[pl.ds(t, 1), :]
            return c

        lax.fori_loop(0, trips, gather, 0)
        xs = xs_ref[...].astype(BF16)
        a = jnp.dot(xs, wg_ref[0], preferred_element_type=F32)
        u = jnp.dot(xs, wu_ref[0], preferred_element_type=F32)
        hid = (a * jax.nn.sigmoid(a)) * u
        y_ref[...] = jnp.dot(hid.astype(BF16), wd_ref[0], preferred_element_type=F32)

        def scatter(i, c):
            for k in range(MOE_UNROLL):
                j = i * MOE_UNROLL + k
                dst = jnp.where(j < m, idx_ref[r0 + j], 2 * bt + k)
                yb_ref[pl.ds(dst, 1), :] = ws_ref[r0 + j] * y_ref[pl.ds(j, 1), :]
            return c

        lax.fori_loop(0, trips, scatter, 0)
        return carry

    lax.fori_loop(0, (n + rows - 1) // rows, tile, 0)

    @pl.when(e == pl.num_programs(1) - 1)
    def _():
        y = hres_ref[...] + yb_ref[0:bt, :] + yb_ref[bt:2 * bt, :]
        ms = jnp.mean(y * y, axis=-1, keepdims=True)
        o_ref[...] = y * lax.rsqrt(ms + NORM_EPS) * gf_ref[...]


def _moe(cnt, off, idx, ws, xt, hres, wg, wu, wd, g_final, bt):
    T = xt.shape[0]
    row = lambda i, e, *_: (i, 0)
    grid_spec = pltpu.PrefetchScalarGridSpec(
        num_scalar_prefetch=4,
        grid=(T // bt, N_EXPERTS),
        in_specs=[
            pl.BlockSpec((bt, D_MODEL), row),
            pl.BlockSpec((bt, D_MODEL), row),
            pl.BlockSpec((1, D_MODEL, D_EXPERT), lambda i, e, *_: (e, 0, 0)),
            pl.BlockSpec((1, D_MODEL, D_EXPERT), lambda i, e, *_: (e, 0, 0)),
            pl.BlockSpec((1, D_EXPERT, D_MODEL), lambda i, e, *_: (e, 0, 0)),
            pl.BlockSpec((1, D_MODEL), lambda i, e, *_: (0, 0)),
        ],
        out_specs=pl.BlockSpec((bt, D_MODEL), row),
        scratch_shapes=[
            pltpu.VMEM((2 * bt + MOE_UNROLL, D_MODEL), F32),
            pltpu.VMEM((bt, D_MODEL), F32),
            pltpu.VMEM((MOE_ROWS, D_MODEL), F32),
            pltpu.VMEM((MOE_ROWS, D_MODEL), F32),
        ],
    )
    return pl.pallas_call(
        _moe_kernel,
        grid_spec=grid_spec,
        out_shape=jax.ShapeDtypeStruct((T, D_MODEL), F32),
        compiler_params=_cparams(("parallel", "arbitrary")),
        name="moe",
    )(cnt, off, idx, ws, xt, hres, wg, wu, wd, g_final)


def _rope_tables(T):
    inv = ROPE_THETA ** (-jnp.arange(0, ATT_QKDIM, 2, dtype=F32) / ATT_QKDIM)
    ang = jnp.arange(T, dtype=F32)[:, None] * inv[None, :]
    c, s = jnp.cos(ang), jnp.sin(ang)
    return (jnp.concatenate([c, c, c, c], axis=1), jnp.concatenate([-s, s, -s, s], axis=1),
            c.T, s.T)


def kernel(x, w_in, w_out, g_mix, lam_params, subln_g, hgrn_gnorm_g, hgrn_lb, g_ffn, w_gr, b_gr,
           w_er, b_er, w_gate, w_up, w_down, g_final):
    B, T, D = x.shape
    x2 = x.reshape(B * T, D)
    l = 0
    w_in_bf = w_in[l].astype(BF16)
    w_out_bf = w_out[l].astype(BF16)
    lp = lam_params[l].astype(F32)
    lam = (jnp.exp(jnp.sum(lp[0] * lp[1])) - jnp.exp(jnp.sum(lp[2] * lp[3])) + LAMBDA_INIT).reshape(1, 1)
    lb = jnp.cumsum(jax.nn.softmax(hgrn_lb.astype(F32), axis=1), axis=1)[:, l]
    w_qv_t = jnp.concatenate([w_in[l][:, 0:ATT_WIDTH], w_in[l][:, 2 * ATT_WIDTH:3 * ATT_WIDTH]],
                             axis=1).T.astype(BF16)
    cos_t, sin_t, cos_tt, sin_tt = _rope_tables(T)
    w_r = jnp.concatenate([jnp.transpose(w_er[l], (1, 0, 2)).reshape(D, N_EXPERTS), w_gr[l],
                           jnp.zeros((D, ROUTER_LANES - N_EXPERTS - N_GROUPS), F32)], axis=1)
    wr_hi = w_r.astype(BF16)
    wr_lo = (w_r - wr_hi.astype(F32)).astype(BF16)
    rbias = jnp.concatenate([b_er[l].reshape(-1), b_gr[l],
                             jnp.zeros((ROUTER_LANES - N_EXPERTS - N_GROUPS,), F32)]).reshape(1, -1)
    wg = w_gate[l].reshape(N_EXPERTS, D, D_EXPERT).astype(BF16)
    wu = w_up[l].reshape(N_EXPERTS, D, D_EXPERT).astype(BF16)
    wd = w_down[l].reshape(N_EXPERTS, D_EXPERT, D).astype(BF16)

    tk = min(ATT_TK, (B * T) // 2)
    qt, k, vt, hgrn_in = _in_proj(x2, g_mix[l].reshape(1, D), w_in_bf, w_qv_t, cos_t, sin_t,
                                  cos_tt, sin_tt, tk)
    att = _diff_attn(lam, qt, k, vt, subln_g[l].reshape(-1, 1))
    o_b = _hgrn_pass(hgrn_in, lb[1:2], backward=True)
    rec = _hgrn_pass(hgrn_in, lb[0:1], backward=False, o_b=o_b,
                     gnorm_g=hgrn_gnorm_g[l].reshape(1, -1))
    hres, xt, route = _out_proj(att, rec, x2, w_out_bf, g_ffn[l].reshape(1, D), wr_hi, wr_lo, rbias)
    bt = min(MOE_BT, B * T)
    lists, meta = _route_sort(route, bt)
    pad = jnp.zeros((MOE_UNROLL,), F32)
    idx = jnp.concatenate([(lists[:, 5] * bt + lists[:, 0] * 32.0 + lists[:, 1]).reshape(-1),
                           pad]).astype(jnp.int32)
    ws = jnp.concatenate([(lists[:, 2] + lists[:, 3] + lists[:, 4]).reshape(-1), pad])
    cnt = meta[:, 0, :N_EXPERTS].astype(jnp.int32).reshape(-1)
    off = meta[:, 1, :N_EXPERTS].astype(jnp.int32).reshape(-1)
    out = _moe(cnt, off, idx, ws, xt, hres, wg, wu, wd, g_final.reshape(1, D), bt)
    return out.reshape(B, T, D)
```

```python
import functools
import math

import numpy as np
import jax
import jax.numpy as jnp
from jax import lax
from jax.experimental import pallas as pl
from jax.experimental.pallas import tpu as pltpu

D_MODEL = 1024
ATT_WIDTH = 512
ATT_HEADS = 4
ATT_VDIM = 128
ATT_QKDIM = 64
HGRN_WIDTH = 512
HGRN_HEADS = 4
HGRN_VDIM = 128
HGRN_EXPAND = 128
HGRN_FDIM = 512
N_GROUPS = 4
EXPERTS_PER_GROUP = 8
N_EXPERTS = N_GROUPS * EXPERTS_PER_GROUP
D_EXPERT = 512
ROPE_THETA = 10000.0
NORM_EPS = 1e-6
SUBLN_EPS = 1e-5
LAMBDA_INIT = 0.8 - 0.6 * math.exp(-0.3 * 0)
LOG2E = math.log2(math.e)
IN_COLS = 3 * ATT_WIDTH + 3 * HGRN_FDIM + 2 * HGRN_WIDTH
HGRN_COLS = IN_COLS - 3 * ATT_WIDTH

LANES = 128
VMEM_LIMIT = 56 * 1024 * 1024
HGRN_CHUNK = 64
IN_TM = 256
ATT_TK = 1024
ATT_CB = 256
ATT_KT = 256
MOE_BT = 2048
MOE_ROWS = 256
MOE_UNROLL = 8
ATT_VPAD = ATT_VDIM + 16
ROUTER_LANES = LANES

BF16 = jnp.bfloat16
F32 = jnp.float32


def _cparams(sem):
    return pltpu.CompilerParams(dimension_semantics=sem, vmem_limit_bytes=VMEM_LIMIT)


def _rot_half_64(x):
    lane = lax.broadcasted_iota(jnp.int32, x.shape, 1)
    fwd = pltpu.roll(x, 32, axis=1)
    bwd = pltpu.roll(x, 96, axis=1)
    return jnp.where((lane & 63) < 32, bwd, fwd)


def _inproj_kernel(x_ref, g_ref, w_ref, wqvt_ref, cos_ref, sin_ref, cost_ref, sint_ref,
                   qt_ref, k_ref, vt_ref, hg_ref):
    x = x_ref[...]
    tm = x.shape[0]
    ms = jnp.mean(x * x, axis=-1, keepdims=True)
    h = (x * lax.rsqrt(ms + NORM_EPS) * g_ref[...]).astype(BF16)
    nt = (((1,), (1,)), ((), ()))
    half = ATT_QKDIM // 2
    ct = cost_ref[...] * (ATT_QKDIM ** -0.5 * LOG2E)
    st = sint_ref[...] * (ATT_QKDIM ** -0.5 * LOG2E)
    for j in range(ATT_WIDTH // LANES):
        a = lax.dot_general(wqvt_ref[j * LANES:(j + 1) * LANES, :], h, nt,
                            preferred_element_type=F32)
        for c in range(LANES // ATT_QKDIM):
            x1 = a[c * ATT_QKDIM:c * ATT_QKDIM + half]
            x2 = a[c * ATT_QKDIM + half:(c + 1) * ATT_QKDIM]
            lo = j * LANES + c * ATT_QKDIM
            qt_ref[lo:lo + half, :] = (x1 * ct - x2 * st).astype(BF16)
            qt_ref[lo + half:lo + ATT_QKDIM, :] = (x1 * st + x2 * ct).astype(BF16)
    ones = jnp.ones((ATT_VPAD - ATT_VDIM, tm), BF16)
    for j in range(ATT_HEADS):
        lo = ATT_WIDTH + j * ATT_VDIM
        vt = lax.dot_general(wqvt_ref[lo:lo + ATT_VDIM, :], h, nt, preferred_element_type=F32)
        vt_ref[0, j * ATT_VPAD:j * ATT_VPAD + ATT_VDIM, :] = vt.astype(BF16)
        vt_ref[0, j * ATT_VPAD + ATT_VDIM:(j + 1) * ATT_VPAD, :] = ones
    cosv = cos_ref[...]
    sinv = sin_ref[...]
    for j in range(ATT_WIDTH // LANES):
        lo = ATT_WIDTH + j * LANES
        a = jnp.dot(h, w_ref[:, lo:lo + LANES], preferred_element_type=F32)
        k_ref[:, j * LANES:(j + 1) * LANES] = (a * cosv + _rot_half_64(a) * sinv).astype(BF16)
    for j in range(HGRN_COLS // 512):
        lo = 3 * ATT_WIDTH + j * 512
        hg_ref[:, j * 512:(j + 1) * 512] = jnp.dot(h, w_ref[:, lo:lo + 512],
                                                   preferred_element_type=F32)


def _in_proj(x2, g_mix, w_in_bf, w_qv_t, cos_t, sin_t, cos_tt, sin_tt, tk):
    T = x2.shape[0]
    tm = IN_TM
    per = tk // tm
    return pl.pallas_call(
        _inproj_kernel,
        grid=(T // tm,),
        in_specs=[
            pl.BlockSpec((tm, D_MODEL), lambda i: (i, 0)),
            pl.BlockSpec((1, D_MODEL), lambda i: (0, 0)),
            pl.BlockSpec((D_MODEL, IN_COLS), lambda i: (0, 0)),
            pl.BlockSpec((2 * ATT_WIDTH, D_MODEL), lambda i: (0, 0)),
            pl.BlockSpec((tm, LANES), lambda i: (i, 0)),
            pl.BlockSpec((tm, LANES), lambda i: (i, 0)),
            pl.BlockSpec((ATT_QKDIM // 2, tm), lambda i: (0, i)),
            pl.BlockSpec((ATT_QKDIM // 2, tm), lambda i: (0, i)),
        ],
        out_specs=[
            pl.BlockSpec((ATT_WIDTH, tm), lambda i: (0, i)),
            pl.BlockSpec((tm, ATT_WIDTH), lambda i: (i, 0)),
            pl.BlockSpec((1, ATT_HEADS * ATT_VPAD, tm), lambda i: (i // per, 0, i % per)),
            pl.BlockSpec((tm, HGRN_COLS), lambda i: (i, 0)),
        ],
        out_shape=[
            jax.ShapeDtypeStruct((ATT_WIDTH, T), BF16),
            jax.ShapeDtypeStruct((T, ATT_WIDTH), BF16),
            jax.ShapeDtypeStruct((T // tk, ATT_HEADS * ATT_VPAD, tk), BF16),
            jax.ShapeDtypeStruct((T, HGRN_COLS), F32),
        ],
        compiler_params=_cparams(("parallel",)),
        name="in_proj",
    )(x2, g_mix, w_in_bf, w_qv_t, cos_t, sin_t, cos_tt, sin_tt)


def _attn_kernel(lam_ref, qt_ref, k_ref, vt_ref, g_ref, o_ref, qw_ref, sa_ref, sb_ref, acc_ref):
    tq = qt_ref.shape[1]
    nchunk, _, tk = vt_ref.shape
    qt = qt_ref[...].astype(F32)
    row = lax.broadcasted_iota(jnp.int32, qt.shape, 0)
    qw_ref[:, 0:tq] = jnp.where(row < ATT_QKDIM, qt, 0.0).astype(BF16)
    qw_ref[:, tq:] = jnp.where(row >= ATT_QKDIM, qt, 0.0).astype(BF16)
    acc_ref[...] = jnp.zeros_like(acc_ref)
    ncb = 2 * tq // ATT_CB

    def stage(j_acc, s_acc, m_cur, alpha, j_sc, s_sc, m_old):
        off = pl.multiple_of(j_sc * tk, tk)
        m_new, al_new = [], []
        for cb in range(ncb):
            cols = slice(cb * ATT_CB, (cb + 1) * ATT_CB)
            part, cmax = None, None
            for t in range(tk // ATT_KT):
                rows = slice(t * ATT_KT, (t + 1) * ATT_KT)
                kc = k_ref[pl.ds(off + t * ATT_KT, ATT_KT), :]
                s = jnp.dot(kc, qw_ref[:, cols], preferred_element_type=F32)
                s_sc[rows, cols] = s
                c = jnp.max(s, axis=0, keepdims=True)
                cmax = c if cmax is None else jnp.maximum(cmax, c)
                if s_acc is not None:
                    p = jnp.exp2(s_acc[rows, cols] - m_cur[cb]).astype(BF16)
                    d = jnp.dot(vt_ref[j_acc, :, rows], p, preferred_element_type=F32)
                    part = d if part is None else part + d
            if s_acc is not None:
                acc_ref[:, cols] = alpha[cb] * acc_ref[:, cols] + part
            mn = jnp.maximum(m_old[cb], cmax)
            m_new.append(mn)
            al_new.append(jnp.exp2(m_old[cb] - mn))
        return tuple(m_new), tuple(al_new)

    m0 = tuple(jnp.full((1, ATT_CB), -jnp.inf, F32) for _ in range(ncb))
    state = stage(0, None, None, None, 0, sa_ref, m0)

    per_trip = 4 if nchunk % 4 == 0 else 2
    bufs = (sa_ref, sb_ref)

    def body(i, st):
        m_cur, al_cur = st
        for u in range(per_trip):
            j = per_trip * i + u
            m_cur, al_cur = stage(j, bufs[u % 2], m_cur, al_cur,
                                  jnp.minimum(j + 1, nchunk - 1), bufs[(u + 1) % 2], m_cur)
        return m_cur, al_cur

    lax.fori_loop(0, nchunk // per_trip, body, state)
    lam = lam_ref[0, 0]
    acc = acc_ref[...]
    o1 = acc[0:ATT_VDIM, 0:tq] / acc[ATT_VDIM:ATT_VDIM + 1, 0:tq]
    o2 = acc[0:ATT_VDIM, tq:] / acc[ATT_VDIM:ATT_VDIM + 1, tq:]
    o = o1 - lam * o2
    ms = jnp.mean(o * o, axis=0, keepdims=True)
    o = o * lax.rsqrt(ms + SUBLN_EPS) * g_ref[...] * (1.0 - LAMBDA_INIT)
    o_ref[...] = o.T.astype(o_ref.dtype)


def _diff_attn(lam, qt, k, vt, subln_g, tq=512):
    T = k.shape[0]
    nchunk, _, tk = vt.shape
    return pl.pallas_call(
        _attn_kernel,
        grid=(ATT_HEADS, T // tq),
        in_specs=[
            pl.BlockSpec(memory_space=pltpu.SMEM),
            pl.BlockSpec((LANES, tq), lambda h, i: (h, i)),
            pl.BlockSpec((T, LANES), lambda h, i: (0, h)),
            pl.BlockSpec((nchunk, ATT_VPAD, tk), lambda h, i: (0, h, 0)),
            pl.BlockSpec((ATT_VDIM, 1), lambda h, i: (0, 0)),
        ],
        out_specs=pl.BlockSpec((tq, LANES), lambda h, i: (i, h)),
        out_shape=jax.ShapeDtypeStruct((T, ATT_WIDTH), BF16),
        scratch_shapes=[
            pltpu.VMEM((LANES, 2 * tq), BF16),
            pltpu.VMEM((tk, 2 * tq), F32),
            pltpu.VMEM((tk, 2 * tq), F32),
            pltpu.VMEM((ATT_VPAD, 2 * tq), F32),
        ],
        compiler_params=_cparams(("parallel", "parallel")),
        name="diff_attn",
    )(lam, qt, k, vt, subln_g)


def _hgrn_consts(C, backward):
    halves = []
    h = C // 2
    while h >= 1:
        halves.append(h)
        h //= 2
    E = np.zeros((2 + len(halves), C, C), np.float32)
    M = np.zeros((len(halves) + 1, C, C), np.float32)
    idx = np.arange(C)
    for t in range(C):
        if not backward:
            E[0, t, idx <= t] = 1.0
            E[1, t, idx > t] = 1.0
        else:
            E[0, t, idx >= t] = 1.0
            E[1, t, idx < t] = 1.0
    for li, h in enumerate(halves):
        for t in range(C):
            mid = (t // (2 * h)) * 2 * h + h
            upper = t >= mid
            if not backward:
                if upper:
                    E[2 + li, t, (idx >= mid) & (idx <= t)] = 1.0
                else:
                    E[2 + li, t, (idx > t) & (idx <= mid - 1)] = 1.0
            else:
                if not upper:
                    E[2 + li, t, (idx >= t) & (idx <= mid - 1)] = 1.0
                else:
                    E[2 + li, t, (idx >= mid) & (idx <= t - 1)] = 1.0
            for s in range(C):
                same = (s // (2 * h)) == (t // (2 * h))
                s_upper = s >= mid
                if same and ((not backward and upper and not s_upper)
                             or (backward and not upper and s_upper)):
                    M[li, t, s] = 1.0
    M[-1] = np.eye(C, dtype=np.float32)
    E = np.concatenate([E.reshape(-1, C), np.ones((8, C), np.float32)], axis=0)
    return E, M


def _split2(g):
    hi = g.astype(BF16)
    lo = (g - hi.astype(F32)).astype(BF16)
    return hi, lo


def _hgrn_kernel(*refs, backward, nchunk, final):
    if final:
        (hq_ref, hf_ref, hi_ref, lb_ref, e_ref, m_ref, ob_ref, hgate_ref, gn_ref,
         o_ref, st_ref) = refs
    else:
        hq_ref, hf_ref, hi_ref, lb_ref, e_ref, m_ref, o_ref, st_ref = refs
    C = HGRN_CHUNK
    nlev = m_ref.shape[0] - 1
    nt = (((1,), (1,)), ((), ()))

    @pl.when(pl.program_id(0) == 0)
    def _():
        st_ref[...] = jnp.zeros_like(st_ref)

    lbd = lb_ref[...]
    emat = e_ref[...]
    order = range(nchunk - 1, -1, -1) if backward else range(nchunk)
    for c in order:
        rows = pl.ds(c * C, C)
        z = hf_ref[rows, :]
        hq = hq_ref[rows, :]
        f = lbd + (1.0 - lbd) * jax.nn.sigmoid(z)
        kk = (1.0 - lbd) * jax.nn.sigmoid(-z)
        g = jnp.log(f)
        q = hq * jax.nn.sigmoid(hq)
        vb = hi_ref[rows, :].astype(BF16)
        ghi, glo = _split2(g)
        ex = (jnp.dot(emat, ghi, preferred_element_type=F32)
              + jnp.dot(emat, glo, preferred_element_type=F32))
        ee = jnp.exp(ex)
        qi = (q * ee[0:C]).astype(BF16)
        ki = (kk * ee[C:2 * C]).astype(BF16)
        dec = ee[(2 + nlev) * C:(2 + nlev) * C + 1]
        qb = q.astype(BF16)
        kb = kk.astype(BF16)
        qlev = [(q * ee[(2 + l) * C:(3 + l) * C]).astype(BF16) for l in range(nlev)]
        klev = [(kk * ee[(2 + l) * C:(3 + l) * C]).astype(BF16) for l in range(nlev)]
        outs = []
        for hd in range(HGRN_HEADS):
            ln = slice(hd * LANES, (hd + 1) * LANES)
            a = m_ref[nlev] * lax.dot_general(qb[:, ln], kb[:, ln], nt, preferred_element_type=F32)
            for l in range(nlev):
                a = a + m_ref[l] * lax.dot_general(qlev[l][:, ln], klev[l][:, ln], nt,
                                                   preferred_element_type=F32)
            st = st_ref[hd]
            o = jnp.dot(a.astype(BF16), vb[:, ln], preferred_element_type=F32)
            o = o + lax.dot_general(qi[:, ln], st.astype(BF16), nt, preferred_element_type=F32)
            vt = hi_ref[rows, ln].T.astype(BF16)
            st_ref[hd] = st * dec[:, ln] + jnp.dot(vt, ki[:, ln], preferred_element_type=F32)
            outs.append(o)
        o_all = jnp.concatenate(outs, axis=1)
        if final:
            o_all = o_all + ob_ref[rows, :]
            res = []
            for hd in range(HGRN_HEADS):
                ln = slice(hd * LANES, (hd + 1) * LANES)
                oh = o_all[:, ln]
                ms = jnp.mean(oh * oh, axis=-1, keepdims=True)
                res.append(oh * lax.rsqrt(ms + NORM_EPS) * gn_ref[...])
            hg = hgate_ref[rows, :]
            o_all = jnp.concatenate(res, axis=1) * (hg * jax.nn.sigmoid(hg))
        o_ref[rows, :] = o_all.astype(o_ref.dtype)


def _hgrn_pass(hgrn_in, lb_row, backward, o_b=None, gnorm_g=None, tb=256):
    T = hgrn_in.shape[0]
    nblk = T // tb
    final = o_b is not None
    E, M = _hgrn_consts(HGRN_CHUNK, backward)
    e_bf = jnp.asarray(E, BF16)
    m_f = jnp.asarray(M, F32)
    blk = (lambda i: nblk - 1 - i) if backward else (lambda i: i)
    f_part = 2 if backward else 1
    in_specs = [
        pl.BlockSpec((tb, 512), lambda i: (blk(i), 0)),
        pl.BlockSpec((tb, 512), lambda i: (blk(i), f_part)),
        pl.BlockSpec((tb, 512), lambda i: (blk(i), 3)),
        pl.BlockSpec((1, 512), lambda i: (0, 0)),
        pl.BlockSpec(e_bf.shape, lambda i: (0, 0)),
        pl.BlockSpec(m_f.shape, lambda i: (0, 0, 0)),
    ]
    args = [hgrn_in, hgrn_in, hgrn_in, lb_row, e_bf, m_f]
    if final:
        in_specs += [
            pl.BlockSpec((tb, 512), lambda i: (blk(i), 0)),
            pl.BlockSpec((tb, 512), lambda i: (blk(i), 4)),
            pl.BlockSpec((1, HGRN_VDIM), lambda i: (0, 0)),
        ]
        args += [o_b, hgrn_in, gnorm_g]
    return pl.pallas_call(
        functools.partial(_hgrn_kernel, backward=backward, nchunk=tb // HGRN_CHUNK, final=final),
        grid=(nblk,),
        in_specs=in_specs,
        out_specs=pl.BlockSpec((tb, 512), lambda i: (blk(i), 0)),
        out_shape=jax.ShapeDtypeStruct((T, HGRN_WIDTH), BF16 if final else F32),
        scratch_shapes=[pltpu.VMEM((HGRN_HEADS, HGRN_VDIM, HGRN_EXPAND), F32)],
        compiler_params=_cparams(("arbitrary",)),
        name="hgrn_bwd" if backward else "hgrn_fwd",
    )(*args)


def _outproj_kernel(att_ref, rec_ref, x_ref, wo_ref, g_ref, wr_hi_ref, wr_lo_ref, rb_ref,
                    hres_ref, xt_ref, comb_ref):
    mix = (jnp.dot(att_ref[...], wo_ref[0:ATT_WIDTH, :], preferred_element_type=F32)
           + jnp.dot(rec_ref[...], wo_ref[ATT_WIDTH:, :], preferred_element_type=F32))
    hres = x_ref[...] + mix
    hres_ref[...] = hres
    ms = jnp.mean(hres * hres, axis=-1, keepdims=True)
    xt = hres * lax.rsqrt(ms + NORM_EPS) * g_ref[...]
    xt_ref[...] = xt
    xt_hi = xt.astype(BF16)
    xt_lo = (xt - xt_hi.astype(F32)).astype(BF16)
    logits = (jnp.dot(xt_hi, wr_hi_ref[...], preferred_element_type=F32)
              + jnp.dot(xt_hi, wr_lo_ref[...], preferred_element_type=F32)
              + jnp.dot(xt_lo, wr_hi_ref[...], preferred_element_type=F32)) + rb_ref[...]
    lane = lax.broadcasted_iota(jnp.int32, logits.shape, 1)
    neg = -jnp.inf
    big = jnp.int32(1 << 20)
    gmask = (lane >= N_EXPERTS) & (lane < N_EXPERTS + N_GROUPS)
    glog = jnp.where(gmask, logits, neg)
    gmax = jnp.max(glog, axis=-1, keepdims=True)
    gsum = jnp.sum(jnp.exp(glog - gmax), axis=-1, keepdims=True)
    p_g = 1.0 / gsum
    g_idx = jnp.min(jnp.where(glog == gmax, lane, big), axis=-1, keepdims=True) - N_EXPERTS
    emask = (lane < N_EXPERTS) & ((lane >> 3) == g_idx)
    elog = jnp.where(emask, logits, neg)
    e1 = jnp.max(elog, axis=-1, keepdims=True)
    i1 = jnp.min(jnp.where(elog == e1, lane, big), axis=-1, keepdims=True)
    elog2 = jnp.where(lane == i1, neg, elog)
    e2 = jnp.max(elog2, axis=-1, keepdims=True)
    i2 = jnp.min(jnp.where(elog2 == e2, lane, big), axis=-1, keepdims=True)
    r = jnp.exp(e2 - e1)
    w1 = p_g / (1.0 + r)
    w2 = p_g * r / (1.0 + r)
    comb_ref[...] = jnp.where(lane == 0, i1.astype(F32), jnp.where(
        lane == 1, i2.astype(F32), jnp.where(lane == 2, w1, jnp.where(lane == 3, w2, 0.0))))


def _out_proj(att, rec, x2, w_out_bf, g_ffn, wr_hi, wr_lo, rbias, tm=512):
    T = x2.shape[0]
    row = lambda i: (i, 0)
    fixed = lambda i: (0, 0)
    return pl.pallas_call(
        _outproj_kernel,
        grid=(T // tm,),
        in_specs=[
            pl.BlockSpec((tm, ATT_WIDTH), row),
            pl.BlockSpec((tm, HGRN_WIDTH), row),
            pl.BlockSpec((tm, D_MODEL), row),
            pl.BlockSpec((D_MODEL, D_MODEL), fixed),
            pl.BlockSpec((1, D_MODEL), fixed),
            pl.BlockSpec((D_MODEL, ROUTER_LANES), fixed),
            pl.BlockSpec((D_MODEL, ROUTER_LANES), fixed),
            pl.BlockSpec((1, ROUTER_LANES), fixed),
        ],
        out_specs=[
            pl.BlockSpec((tm, D_MODEL), row),
            pl.BlockSpec((tm, D_MODEL), row),
            pl.BlockSpec((tm, ROUTER_LANES), row),
        ],
        out_shape=[
            jax.ShapeDtypeStruct((T, D_MODEL), F32),
            jax.ShapeDtypeStruct((T, D_MODEL), F32),
            jax.ShapeDtypeStruct((T, ROUTER_LANES), F32),
        ],
        compiler_params=_cparams(("parallel",)),
        name="out_proj_router",
    )(att, rec, x2, w_out_bf, g_ffn, wr_hi, wr_lo, rbias)


def _split3_f32(w):
    hi = w.astype(BF16).astype(F32)
    r1 = w - hi
    mid = r1.astype(BF16).astype(F32)
    lo = (r1 - mid).astype(BF16).astype(F32)
    return hi, mid, lo


def _route_sort_kernel(rt_ref, ltri_ref, utri_ref, lists_ref, meta_ref):
    rt = rt_ref[...]
    bt = rt.shape[0]
    lane = lax.broadcasted_iota(jnp.int32, rt.shape, 1)
    lane_f = lane.astype(F32)
    oh1 = lane_f == rt[:, 0:1]
    oh2 = lane_f == rt[:, 1:2]
    oh = jnp.where(oh1, 1.0, jnp.where(oh2, 1.0, 0.0))
    oh_b = oh.astype(BF16)
    rank = jnp.dot(ltri_ref[...], oh_b, preferred_element_type=F32)
    below = jnp.dot(oh_b, utri_ref[...], preferred_element_type=F32)
    offs = jnp.sum(below, axis=0, keepdims=True)
    cnt = jnp.sum(oh, axis=0, keepdims=True)
    posmat = rank + offs
    pos1 = jnp.sum(jnp.where(oh1, posmat, 0.0), axis=1, keepdims=True)
    pos2 = jnp.sum(jnp.where(oh2, posmat, 0.0), axis=1, keepdims=True)
    tok = lax.broadcasted_iota(jnp.int32, rt.shape, 0)
    tok_hi = (tok >> 5).astype(F32)
    tok_lo = (tok & 31).astype(F32)

    def record(w, slot):
        hi, mid, lo = _split3_f32(w)
        d = jnp.where(lane == 0, tok_hi, jnp.where(lane == 1, tok_lo, jnp.where(
            lane == 2, hi, jnp.where(lane == 3, mid, jnp.where(lane == 4, lo, jnp.where(
                lane == 5, slot, 0.0))))))
        return d.T.astype(BF16)

    d1 = record(rt[:, 2:3], 0.0)
    d2 = record(rt[:, 3:4], 1.0)
    cw = 512
    for c in range(2 * bt // cw):
        colp = (lax.broadcasted_iota(jnp.int32, (bt, cw), 1) + c * cw).astype(F32)
        o1 = jnp.where(colp == pos1, 1.0, 0.0).astype(BF16)
        o2 = jnp.where(colp == pos2, 1.0, 0.0).astype(BF16)
        srt = (jnp.dot(d1, o1, preferred_element_type=F32)
               + jnp.dot(d2, o2, preferred_element_type=F32))
        lists_ref[0, :, c * cw:(c + 1) * cw] = srt[0:8]
    row = lax.broadcasted_iota(jnp.int32, (8, LANES), 0)
    meta_ref[0] = jnp.where(row == 0, cnt, jnp.where(row == 1, offs, 0.0))


def _route_sort(route, bt):
    T = route.shape[0]
    nb = T // bt
    ltri = jnp.asarray(np.tril(np.ones((bt, bt), np.float32), -1), BF16)
    utri = jnp.asarray(np.triu(np.ones((LANES, LANES), np.float32), 1), BF16)
    return pl.pallas_call(
        _route_sort_kernel,
        grid=(nb,),
        in_specs=[
            pl.BlockSpec((bt, ROUTER_LANES), lambda b: (b, 0)),
            pl.BlockSpec((bt, bt), lambda b: (0, 0)),
            pl.BlockSpec((LANES, LANES), lambda b: (0, 0)),
        ],
        out_specs=[
            pl.BlockSpec((1, 8, 2 * bt), lambda b: (b, 0, 0)),
            pl.BlockSpec((1, 8, LANES), lambda b: (b, 0, 0)),
        ],
        out_shape=[
            jax.ShapeDtypeStruct((nb, 8, 2 * bt), F32),
            jax.ShapeDtypeStruct((nb, 8, LANES), F32),
        ],
        compiler_params=_cparams(("parallel",)),
        name="route_sort",
    )(route, ltri, utri)


def _moe_kernel(cnt_ref, off_ref, idx_ref, ws_ref, xt_ref, hres_ref, wg_ref, wu_ref, wd_ref, gf_ref,
                o_ref, yb_ref, xs_ref, y_ref):
    b = pl.program_id(0)
    e = pl.program_id(1)
    bt = xt_ref.shape[0]
    rows = xs_ref.shape[0]

    @pl.when(e == 0)
    def _():
        xs_ref[...] = jnp.zeros_like(xs_ref)

    n = cnt_ref[b * N_EXPERTS + e]
    base = b * 2 * bt + off_ref[b * N_EXPERTS + e]

    def tile(r, carry):
        r0 = base + r * rows
        m = jnp.minimum(n - r * rows, rows)
        trips = (m + MOE_UNROLL - 1) // MOE_UNROLL

        def gather(i, c):
            for k in range(MOE_UNROLL):
                j = i * MOE_UNROLL + k
                t = idx_ref[r0 + j] & (bt - 1)
                xs_ref[pl.ds(j, 1), :] = xt_ref[pl.ds(t, 1), :]
            return c

        lax.fori_loop(0, trips, gather, 0)
        xs = xs_ref[...].astype(BF16)
        a = jnp.dot(xs, wg_ref[0], preferred_element_type=F32)
        u = jnp.dot(xs, wu_ref[0], preferred_element_type=F32)
        hid = (a * jax.nn.sigmoid(a)) * u
        y_ref[...] = jnp.dot(hid.astype(BF16), wd_ref[0], preferred_element_type=F32)

        def scatter(i, c):
            for k in range(MOE_UNROLL):
                j = i * MOE_UNROLL + k
                dst = jnp.where(j < m, idx_ref[r0 + j], 2 * bt + k)
                yb_ref[pl.ds(dst, 1), :] = ws_ref[r0 + j] * y_ref[pl.ds(j, 1), :]
            return c

        lax.fori_loop(0, trips, scatter, 0)
        return carry

    lax.fori_loop(0, (n + rows - 1) // rows, tile, 0)

    @pl.when(e == pl.num_programs(1) - 1)
    def _():
        y = hres_ref[...] + yb_ref[0:bt, :] + yb_ref[bt:2 * bt, :]
        ms = jnp.mean(y * y, axis=-1, keepdims=True)
        o_ref[...] = y * lax.rsqrt(ms + NORM_EPS) * gf_ref[...]


def _moe(cnt, off, idx, ws, xt, hres, wg, wu, wd, g_final, bt):
    T = xt.shape[0]
    row = lambda i, e, *_: (i, 0)
    grid_spec = pltpu.PrefetchScalarGridSpec(
        num_scalar_prefetch=4,
        grid=(T // bt, N_EXPERTS),
        in_specs=[
            pl.BlockSpec((bt, D_MODEL), row, pipeline_mode=pl.Buffered(1)),
            pl.BlockSpec((bt, D_MODEL), row, pipeline_mode=pl.Buffered(1)),
            pl.BlockSpec((1, D_MODEL, D_EXPERT), lambda i, e, *_: (e, 0, 0)),
            pl.BlockSpec((1, D_MODEL, D_EXPERT), lambda i, e, *_: (e, 0, 0)),
            pl.BlockSpec((1, D_EXPERT, D_MODEL), lambda i, e, *_: (e, 0, 0)),
            pl.BlockSpec((1, D_MODEL), lambda i, e, *_: (0, 0)),
        ],
        out_specs=pl.BlockSpec((bt, D_MODEL), row, pipeline_mode=pl.Buffered(1)),
        scratch_shapes=[
            pltpu.VMEM((2 * bt + MOE_UNROLL, D_MODEL), F32),
            pltpu.VMEM((MOE_ROWS, D_MODEL), F32),
            pltpu.VMEM((MOE_ROWS, D_MODEL), F32),
        ],
    )
    return pl.pallas_call(
        _moe_kernel,
        grid_spec=grid_spec,
        out_shape=jax.ShapeDtypeStruct((T, D_MODEL), F32),
        compiler_params=_cparams(("parallel", "arbitrary")),
        name="moe",
    )(cnt, off, idx, ws, xt, hres, wg, wu, wd, g_final)


def _rope_tables(T):
    inv = ROPE_THETA ** (-jnp.arange(0, ATT_QKDIM, 2, dtype=F32) / ATT_QKDIM)
    ang = jnp.arange(T, dtype=F32)[:, None] * inv[None, :]
    c, s = jnp.cos(ang), jnp.sin(ang)
    return (jnp.concatenate([c, c, c, c], axis=1), jnp.concatenate([-s, s, -s, s], axis=1),
            c.T, s.T)


def kernel(x, w_in, w_out, g_mix, lam_params, subln_g, hgrn_gnorm_g, hgrn_lb, g_ffn, w_gr, b_gr,
           w_er, b_er, w_gate, w_up, w_down, g_final):
    B, T, D = x.shape
    x2 = x.reshape(B * T, D)
    l = 0
    w_in_bf = w_in[l].astype(BF16)
    w_out_bf = w_out[l].astype(BF16)
    lp = lam_params[l].astype(F32)
    lam = (jnp.exp(jnp.sum(lp[0] * lp[1])) - jnp.exp(jnp.sum(lp[2] * lp[3])) + LAMBDA_INIT).reshape(1, 1)
    lb = jnp.cumsum(jax.nn.softmax(hgrn_lb.astype(F32), axis=1), axis=1)[:, l]
    w_qv_t = jnp.concatenate([w_in[l][:, 0:ATT_WIDTH], w_in[l][:, 2 * ATT_WIDTH:3 * ATT_WIDTH]],
                             axis=1).T.astype(BF16)
    cos_t, sin_t, cos_tt, sin_tt = _rope_tables(T)
    w_r = jnp.concatenate([jnp.transpose(w_er[l], (1, 0, 2)).reshape(D, N_EXPERTS), w_gr[l],
                           jnp.zeros((D, ROUTER_LANES - N_EXPERTS - N_GROUPS), F32)], axis=1)
    wr_hi = w_r.astype(BF16)
    wr_lo = (w_r - wr_hi.astype(F32)).astype(BF16)
    rbias = jnp.concatenate([b_er[l].reshape(-1), b_gr[l],
                             jnp.zeros((ROUTER_LANES - N_EXPERTS - N_GROUPS,), F32)]).reshape(1, -1)
    wg = w_gate[l].reshape(N_EXPERTS, D, D_EXPERT).astype(BF16)
    wu = w_up[l].reshape(N_EXPERTS, D, D_EXPERT).astype(BF16)
    wd = w_down[l].reshape(N_EXPERTS, D_EXPERT, D).astype(BF16)

    tk = min(ATT_TK, (B * T) // 2)
    qt, k, vt, hgrn_in = _in_proj(x2, g_mix[l].reshape(1, D), w_in_bf, w_qv_t, cos_t, sin_t,
                                  cos_tt, sin_tt, tk)
    att = _diff_attn(lam, qt, k, vt, subln_g[l].reshape(-1, 1))
    o_b = _hgrn_pass(hgrn_in, lb[1:2], backward=True)
    rec = _hgrn_pass(hgrn_in, lb[0:1], backward=False, o_b=o_b,
                     gnorm_g=hgrn_gnorm_g[l].reshape(1, -1))
    hres, xt, route = _out_proj(att, rec, x2, w_out_bf, g_ffn[l].reshape(1, D), wr_hi, wr_lo, rbias)
    bt = min(MOE_BT, B * T)
    lists, meta = _route_sort(route, bt)
    pad = jnp.zeros((MOE_UNROLL,), F32)
    idx = jnp.concatenate([(lists[:, 5] * bt + lists[:, 0] * 32.0 + lists[:, 1]).reshape(-1),
                           pad]).astype(jnp.int32)
    ws = jnp.concatenate([(lists[:, 2] + lists[:, 3] + lists[:, 4]).reshape(-1), pad])
    cnt = meta[:, 0, :N_EXPERTS].astype(jnp.int32).reshape(-1)
    off = meta[:, 1, :N_EXPERTS].astype(jnp.int32).reshape(-1)
    out = _moe(cnt, off, idx, ws, xt, hres, wg, wu, wd, g_final.reshape(1, D), bt)
    return out.reshape(B, T, D)
```

```python
import functools
import math

import numpy as np
import jax
import jax.numpy as jnp
from jax import lax
from jax.experimental import pallas as pl
from jax.experimental.pallas import tpu as pltpu

D_MODEL = 1024
ATT_WIDTH = 512
ATT_HEADS = 4
ATT_VDIM = 128
ATT_QKDIM = 64
HGRN_WIDTH = 512
HGRN_HEADS = 4
HGRN_VDIM = 128
HGRN_EXPAND = 128
HGRN_FDIM = 512
N_GROUPS = 4
EXPERTS_PER_GROUP = 8
N_EXPERTS = N_GROUPS * EXPERTS_PER_GROUP
D_EXPERT = 512
ROPE_THETA = 10000.0
NORM_EPS = 1e-6
SUBLN_EPS = 1e-5
LAMBDA_INIT = 0.8 - 0.6 * math.exp(-0.3 * 0)
LOG2E = math.log2(math.e)
ROPE_SPLIT = 256
IN_COLS = 3 * ATT_WIDTH + 3 * HGRN_FDIM + 2 * HGRN_WIDTH
HGRN_COLS = IN_COLS - 3 * ATT_WIDTH

LANES = 128
VMEM_LIMIT = 56 * 1024 * 1024
HGRN_CHUNK = 64
IN_TM = 256
ATT_TK = 1024
ATT_CB = 256
ATT_KT = 256
ATT_QPER = 1
MOE_BT = 2048
MOE_ROWS = 256
MOE_UNROLL = 8
ATT_VPAD = ATT_VDIM + 16
ROUTER_LANES = LANES

BF16 = jnp.bfloat16
F32 = jnp.float32


def _cparams(sem):
    return pltpu.CompilerParams(dimension_semantics=sem, vmem_limit_bytes=VMEM_LIMIT)


def _rot_half_64(x):
    lane = lax.broadcasted_iota(jnp.int32, x.shape, 1)
    fwd = pltpu.roll(x, 32, axis=1)
    bwd = pltpu.roll(x, 96, axis=1)
    return jnp.where((lane & 63) < 32, bwd, fwd)


def _inproj_kernel(x_ref, g_ref, w_ref, wqvt_ref, cos_ref, sin_ref, cost_ref, sint_ref,
                   qt_ref, k_ref, vt_ref, hg_ref):
    x = x_ref[...]
    tm = x.shape[0]
    ms = jnp.mean(x * x, axis=-1, keepdims=True)
    h = (x * lax.rsqrt(ms + NORM_EPS) * g_ref[...]).astype(BF16)
    nt = (((1,), (1,)), ((), ()))
    half = ATT_QKDIM // 2
    ct = cost_ref[...] * (ATT_QKDIM ** -0.5 * LOG2E)
    st = sint_ref[...] * (ATT_QKDIM ** -0.5 * LOG2E)
    for j in range(ATT_WIDTH // LANES):
        a = lax.dot_general(wqvt_ref[j * LANES:(j + 1) * LANES, :], h, nt,
                            preferred_element_type=F32)
        for c in range(LANES // ATT_QKDIM):
            x1 = a[c * ATT_QKDIM:c * ATT_QKDIM + half]
            x2 = a[c * ATT_QKDIM + half:(c + 1) * ATT_QKDIM]
            lo = j * LANES + c * ATT_QKDIM
            qt_ref[lo:lo + half, :] = (x1 * ct - x2 * st).astype(BF16)
            qt_ref[lo + half:lo + ATT_QKDIM, :] = (x1 * st + x2 * ct).astype(BF16)
    ones = jnp.ones((ATT_VPAD - ATT_VDIM, tm), BF16)
    for j in range(ATT_HEADS):
        lo = ATT_WIDTH + j * ATT_VDIM
        vt = lax.dot_general(wqvt_ref[lo:lo + ATT_VDIM, :], h, nt, preferred_element_type=F32)
        vt_ref[0, j * ATT_VPAD:j * ATT_VPAD + ATT_VDIM, :] = vt.astype(BF16)
        vt_ref[0, j * ATT_VPAD + ATT_VDIM:(j + 1) * ATT_VPAD, :] = ones
    cosv = cos_ref[...]
    sinv = sin_ref[...]
    for j in range(ATT_WIDTH // LANES):
        lo = ATT_WIDTH + j * LANES
        a = jnp.dot(h, w_ref[:, lo:lo + LANES], preferred_element_type=F32)
        k_ref[:, j * LANES:(j + 1) * LANES] = (a * cosv + _rot_half_64(a) * sinv).astype(BF16)
    for j in range(HGRN_COLS // 512):
        lo = 3 * ATT_WIDTH + j * 512
        hg_ref[:, j * 512:(j + 1) * 512] = jnp.dot(h, w_ref[:, lo:lo + 512],
                                                   preferred_element_type=F32)


def _in_proj(x2, g_mix, w_in_bf, w_qv_t, cos_t, sin_t, cos_tt, sin_tt, tk):
    T = x2.shape[0]
    tm = IN_TM
    per = tk // tm
    return pl.pallas_call(
        _inproj_kernel,
        grid=(T // tm,),
        in_specs=[
            pl.BlockSpec((tm, D_MODEL), lambda i: (i, 0)),
            pl.BlockSpec((1, D_MODEL), lambda i: (0, 0)),
            pl.BlockSpec((D_MODEL, IN_COLS), lambda i: (0, 0)),
            pl.BlockSpec((2 * ATT_WIDTH, D_MODEL), lambda i: (0, 0)),
            pl.BlockSpec((tm, LANES), lambda i: (i, 0)),
            pl.BlockSpec((tm, LANES), lambda i: (i, 0)),
            pl.BlockSpec((ATT_QKDIM // 2, tm), lambda i: (0, i)),
            pl.BlockSpec((ATT_QKDIM // 2, tm), lambda i: (0, i)),
        ],
        out_specs=[
            pl.BlockSpec((ATT_WIDTH, tm), lambda i: (0, i)),
            pl.BlockSpec((tm, ATT_WIDTH), lambda i: (i, 0)),
            pl.BlockSpec((1, ATT_HEADS * ATT_VPAD, tm), lambda i: (i // per, 0, i % per)),
            pl.BlockSpec((tm, HGRN_COLS), lambda i: (i, 0)),
        ],
        out_shape=[
            jax.ShapeDtypeStruct((ATT_WIDTH, T), BF16),
            jax.ShapeDtypeStruct((T, ATT_WIDTH), BF16),
            jax.ShapeDtypeStruct((T // tk, ATT_HEADS * ATT_VPAD, tk), BF16),
            jax.ShapeDtypeStruct((T, HGRN_COLS), F32),
        ],
        compiler_params=_cparams(("parallel",)),
        name="in_proj",
    )(x2, g_mix, w_in_bf, w_qv_t, cos_t, sin_t, cos_tt, sin_tt)


def _attn_kernel(lam_ref, qt_ref, k_ref, vt_ref, g_ref, o_ref, qw_ref, sa_ref, sb_ref, acc_ref):
    tq = qt_ref.shape[1]
    nchunk, _, tk = vt_ref.shape
    qt = qt_ref[...].astype(F32)
    row = lax.broadcasted_iota(jnp.int32, qt.shape, 0)
    qw_ref[:, 0:tq] = jnp.where(row < ATT_QKDIM, qt, 0.0).astype(BF16)
    qw_ref[:, tq:] = jnp.where(row >= ATT_QKDIM, qt, 0.0).astype(BF16)
    acc_ref[...] = jnp.zeros_like(acc_ref)
    ncb = 2 * tq // ATT_CB

    def stage(j_acc, s_acc, m_cur, alpha, j_sc, s_sc, m_old):
        off = pl.multiple_of(j_sc * tk, tk)
        m_new, al_new = [], []
        for cb in range(ncb):
            cols = slice(cb * ATT_CB, (cb + 1) * ATT_CB)
            part, cmax = None, None
            for t in range(tk // ATT_KT):
                rows = slice(t * ATT_KT, (t + 1) * ATT_KT)
                if t % ATT_QPER == 0:
                    qrows = slice(t * ATT_KT, (t + ATT_QPER) * ATT_KT)
                    kc = k_ref[pl.ds(off + t * ATT_KT, ATT_QPER * ATT_KT), :]
                    s = jnp.dot(kc, qw_ref[:, cols], preferred_element_type=F32)
                    s_sc[qrows, cols] = s
                    c = jnp.max(s, axis=0, keepdims=True)
                    cmax = c if cmax is None else jnp.maximum(cmax, c)
                if s_acc is not None:
                    p = jnp.exp2(s_acc[rows, cols] - m_cur[cb]).astype(BF16)
                    d = jnp.dot(vt_ref[j_acc, :, rows], p, preferred_element_type=F32)
                    part = d if part is None else part + d
            if s_acc is not None:
                acc_ref[:, cols] = alpha[cb] * acc_ref[:, cols] + part
            mn = jnp.maximum(m_old[cb], cmax)
            m_new.append(mn)
            al_new.append(jnp.exp2(m_old[cb] - mn))
        return tuple(m_new), tuple(al_new)

    m0 = tuple(jnp.full((1, ATT_CB), -jnp.inf, F32) for _ in range(ncb))
    state = stage(0, None, None, None, 0, sa_ref, m0)

    per_trip = 4 if nchunk % 4 == 0 else 2
    bufs = (sa_ref, sb_ref)

    def body(i, st):
        m_cur, al_cur = st
        for u in range(per_trip):
            j = per_trip * i + u
            m_cur, al_cur = stage(j, bufs[u % 2], m_cur, al_cur,
                                  jnp.minimum(j + 1, nchunk - 1), bufs[(u + 1) % 2], m_cur)
        return m_cur, al_cur

    lax.fori_loop(0, nchunk // per_trip, body, state)
    lam = lam_ref[0, 0]
    acc = acc_ref[...]
    o1 = acc[0:ATT_VDIM, 0:tq] / acc[ATT_VDIM:ATT_VDIM + 1, 0:tq]
    o2 = acc[0:ATT_VDIM, tq:] / acc[ATT_VDIM:ATT_VDIM + 1, tq:]
    o = o1 - lam * o2
    ms = jnp.mean(o * o, axis=0, keepdims=True)
    o = o * lax.rsqrt(ms + SUBLN_EPS) * g_ref[...] * (1.0 - LAMBDA_INIT)
    o_ref[...] = o.T.astype(o_ref.dtype)


def _diff_attn(lam, qt, k, vt, subln_g, tq=512):
    T = k.shape[0]
    tq = min(tq, T)
    nchunk, _, tk = vt.shape
    return pl.pallas_call(
        _attn_kernel,
        grid=(ATT_HEADS, T // tq),
        in_specs=[
            pl.BlockSpec(memory_space=pltpu.SMEM),
            pl.BlockSpec((LANES, tq), lambda h, i: (h, i)),
            pl.BlockSpec((T, LANES), lambda h, i: (0, h)),
            pl.BlockSpec((nchunk, ATT_VPAD, tk), lambda h, i: (0, h, 0)),
            pl.BlockSpec((ATT_VDIM, 1), lambda h, i: (0, 0)),
        ],
        out_specs=pl.BlockSpec((tq, LANES), lambda h, i: (i, h)),
        out_shape=jax.ShapeDtypeStruct((T, ATT_WIDTH), BF16),
        scratch_shapes=[
            pltpu.VMEM((LANES, 2 * tq), BF16),
            pltpu.VMEM((tk, 2 * tq), F32),
            pltpu.VMEM((tk, 2 * tq), F32),
            pltpu.VMEM((ATT_VPAD, 2 * tq), F32),
        ],
        compiler_params=_cparams(("parallel", "parallel")),
        name="diff_attn",
    )(lam, qt, k, vt, subln_g)


def _hgrn_consts(C, backward):
    halves = []
    h = C // 2
    while h >= 1:
        halves.append(h)
        h //= 2
    E = np.zeros((2 + len(halves), C, C), np.float32)
    M = np.zeros((len(halves) + 1, C, C), np.float32)
    idx = np.arange(C)
    for t in range(C):
        if not backward:
            E[0, t, idx <= t] = 1.0
            E[1, t, idx > t] = 1.0
        else:
            E[0, t, idx >= t] = 1.0
            E[1, t, idx < t] = 1.0
    for li, h in enumerate(halves):
        for t in range(C):
            mid = (t // (2 * h)) * 2 * h + h
            upper = t >= mid
            if not backward:
                if upper:
                    E[2 + li, t, (idx >= mid) & (idx <= t)] = 1.0
                else:
                    E[2 + li, t, (idx > t) & (idx <= mid - 1)] = 1.0
            else:
                if not upper:
                    E[2 + li, t, (idx >= t) & (idx <= mid - 1)] = 1.0
                else:
                    E[2 + li, t, (idx >= mid) & (idx <= t - 1)] = 1.0
            for s in range(C):
                same = (s // (2 * h)) == (t // (2 * h))
                s_upper = s >= mid
                if same and ((not backward and upper and not s_upper)
                             or (backward and not upper and s_upper)):
                    M[li, t, s] = 1.0
    M[-1] = np.eye(C, dtype=np.float32)
    E = np.concatenate([E.reshape(-1, C), np.ones((8, C), np.float32)], axis=0)
    return E, M


def _split3_rows(g):
    hi = g.astype(BF16)
    r1 = g - hi.astype(F32)
    mid = r1.astype(BF16)
    lo = (r1 - mid.astype(F32)).astype(BF16)
    return jnp.concatenate([hi, mid, lo], axis=0)


def _hgrn_kernel(*refs, backward, nchunk, final):
    if final:
        (hq_ref, hf_ref, hi_ref, lb_ref, e_ref, m_ref, ob_ref, hgate_ref, gn_ref,
         o_ref, st_ref) = refs
    else:
        hq_ref, hf_ref, hi_ref, lb_ref, e_ref, m_ref, o_ref, st_ref = refs
    C = HGRN_CHUNK
    nlev = m_ref.shape[0] - 1
    nt = (((1,), (1,)), ((), ()))

    @pl.when(pl.program_id(0) == 0)
    def _():
        st_ref[...] = jnp.zeros_like(st_ref)

    lbd = lb_ref[...]
    emat = e_ref[...]
    order = range(nchunk - 1, -1, -1) if backward else range(nchunk)
    for c in order:
        rows = pl.ds(c * C, C)
        z = hf_ref[rows, :]
        hq = hq_ref[rows, :]
        f = lbd + (1.0 - lbd) * jax.nn.sigmoid(z)
        kk = (1.0 - lbd) * jax.nn.sigmoid(-z)
        g = jnp.log(f)
        q = hq * jax.nn.sigmoid(hq)
        vb = hi_ref[rows, :].astype(BF16)
        ex = jnp.dot(emat, _split3_rows(g), preferred_element_type=F32)
        ee = jnp.exp(ex)
        qi = (q * ee[0:C]).astype(BF16)
        ki = (kk * ee[C:2 * C]).astype(BF16)
        dec = ee[(2 + nlev) * C:(2 + nlev) * C + 1]
        qb = q.astype(BF16)
        kb = kk.astype(BF16)
        qlev = [(q * ee[(2 + l) * C:(3 + l) * C]).astype(BF16) for l in range(nlev)]
        klev = [(kk * ee[(2 + l) * C:(3 + l) * C]).astype(BF16) for l in range(nlev)]
        outs = []
        for hd in range(HGRN_HEADS):
            ln = slice(hd * LANES, (hd + 1) * LANES)
            a = m_ref[nlev] * lax.dot_general(qb[:, ln], kb[:, ln], nt, preferred_element_type=F32)
            for l in range(nlev):
                a = a + m_ref[l] * lax.dot_general(qlev[l][:, ln], klev[l][:, ln], nt,
                                                   preferred_element_type=F32)
            st = st_ref[hd]
            o = jnp.dot(a.astype(BF16), vb[:, ln], preferred_element_type=F32)
            o = o + lax.dot_general(qi[:, ln], st.astype(BF16), nt, preferred_element_type=F32)
            vt = hi_ref[rows, ln].T.astype(BF16)
            st_ref[hd] = st * dec[:, ln] + jnp.dot(vt, ki[:, ln], preferred_element_type=F32)
            outs.append(o)
        o_all = jnp.concatenate(outs, axis=1)
        if final:
            o_all = o_all + ob_ref[rows, :]
            res = []
            for hd in range(HGRN_HEADS):
                ln = slice(hd * LANES, (hd + 1) * LANES)
                oh = o_all[:, ln]
                ms = jnp.mean(oh * oh, axis=-1, keepdims=True)
                res.append(oh * lax.rsqrt(ms + NORM_EPS) * gn_ref[...])
            hg = hgate_ref[rows, :]
            o_all = jnp.concatenate(res, axis=1) * (hg * jax.nn.sigmoid(hg))
        o_ref[rows, :] = o_all.astype(o_ref.dtype)


def _hgrn_pass(hgrn_in, lb_row, backward, o_b=None, gnorm_g=None, tb=256):
    T = hgrn_in.shape[0]
    nblk = T // tb
    final = o_b is not None
    E, M = _hgrn_consts(HGRN_CHUNK, backward)
    e_bf = jnp.asarray(np.concatenate([E, E, E], axis=1), BF16)
    m_f = jnp.asarray(M, F32)
    blk = (lambda i: nblk - 1 - i) if backward else (lambda i: i)
    f_part = 2 if backward else 1
    in_specs = [
        pl.BlockSpec((tb, 512), lambda i: (blk(i), 0)),
        pl.BlockSpec((tb, 512), lambda i: (blk(i), f_part)),
        pl.BlockSpec((tb, 512), lambda i: (blk(i), 3)),
        pl.BlockSpec((1, 512), lambda i: (0, 0)),
        pl.BlockSpec(e_bf.shape, lambda i: (0, 0)),
        pl.BlockSpec(m_f.shape, lambda i: (0, 0, 0)),
    ]
    args = [hgrn_in, hgrn_in, hgrn_in, lb_row, e_bf, m_f]
    if final:
        in_specs += [
            pl.BlockSpec((tb, 512), lambda i: (blk(i), 0)),
            pl.BlockSpec((tb, 512), lambda i: (blk(i), 4)),
            pl.BlockSpec((1, HGRN_VDIM), lambda i: (0, 0)),
        ]
        args += [o_b, hgrn_in, gnorm_g]
    return pl.pallas_call(
        functools.partial(_hgrn_kernel, backward=backward, nchunk=tb // HGRN_CHUNK, final=final),
        grid=(nblk,),
        in_specs=in_specs,
        out_specs=pl.BlockSpec((tb, 512), lambda i: (blk(i), 0)),
        out_shape=jax.ShapeDtypeStruct((T, HGRN_WIDTH), BF16 if final else F32),
        scratch_shapes=[pltpu.VMEM((HGRN_HEADS, HGRN_VDIM, HGRN_EXPAND), F32)],
        compiler_params=_cparams(("arbitrary",)),
        name="hgrn_bwd" if backward else "hgrn_fwd",
    )(*args)


def _outproj_kernel(att_ref, rec_ref, x_ref, wo_ref, g_ref, wr_hi_ref, wr_lo_ref, rb_ref,
                    hres_ref, xt_ref, comb_ref):
    mix = (jnp.dot(att_ref[...], wo_ref[0:ATT_WIDTH, :], preferred_element_type=F32)
           + jnp.dot(rec_ref[...], wo_ref[ATT_WIDTH:, :], preferred_element_type=F32))
    hres = x_ref[...] + mix
    hres_ref[...] = hres
    ms = jnp.mean(hres * hres, axis=-1, keepdims=True)
    xt = hres * lax.rsqrt(ms + NORM_EPS) * g_ref[...]
    xt_ref[...] = xt
    xt_hi = xt.astype(BF16)
    xt_lo = (xt - xt_hi.astype(F32)).astype(BF16)
    logits = (jnp.dot(xt_hi, wr_hi_ref[...], preferred_element_type=F32)
              + jnp.dot(xt_hi, wr_lo_ref[...], preferred_element_type=F32)
              + jnp.dot(xt_lo, wr_hi_ref[...], preferred_element_type=F32)) + rb_ref[...]
    lane = lax.broadcasted_iota(jnp.int32, logits.shape, 1)
    neg = -jnp.inf
    big = jnp.int32(1 << 20)
    gmask = (lane >= N_EXPERTS) & (lane < N_EXPERTS + N_GROUPS)
    glog = jnp.where(gmask, logits, neg)
    gmax = jnp.max(glog, axis=-1, keepdims=True)
    gsum = jnp.sum(jnp.exp(glog - gmax), axis=-1, keepdims=True)
    p_g = 1.0 / gsum
    g_idx = jnp.min(jnp.where(glog == gmax, lane, big), axis=-1, keepdims=True) - N_EXPERTS
    emask = (lane < N_EXPERTS) & ((lane >> 3) == g_idx)
    elog = jnp.where(emask, logits, neg)
    e1 = jnp.max(elog, axis=-1, keepdims=True)
    i1 = jnp.min(jnp.where(elog == e1, lane, big), axis=-1, keepdims=True)
    elog2 = jnp.where(lane == i1, neg, elog)
    e2 = jnp.max(elog2, axis=-1, keepdims=True)
    i2 = jnp.min(jnp.where(elog2 == e2, lane, big), axis=-1, keepdims=True)
    r = jnp.exp(e2 - e1)
    w1 = p_g / (1.0 + r)
    w2 = p_g * r / (1.0 + r)
    comb_ref[...] = jnp.where(lane == 0, i1.astype(F32), jnp.where(
        lane == 1, i2.astype(F32), jnp.where(lane == 2, w1, jnp.where(lane == 3, w2, 0.0))))


def _out_proj(att, rec, x2, w_out_bf, g_ffn, wr_hi, wr_lo, rbias, tm=512):
    T = x2.shape[0]
    row = lambda i: (i, 0)
    fixed = lambda i: (0, 0)
    return pl.pallas_call(
        _outproj_kernel,
        grid=(T // tm,),
        in_specs=[
            pl.BlockSpec((tm, ATT_WIDTH), row),
            pl.BlockSpec((tm, HGRN_WIDTH), row),
            pl.BlockSpec((tm, D_MODEL), row),
            pl.BlockSpec((D_MODEL, D_MODEL), fixed),
            pl.BlockSpec((1, D_MODEL), fixed),
            pl.BlockSpec((D_MODEL, ROUTER_LANES), fixed),
            pl.BlockSpec((D_MODEL, ROUTER_LANES), fixed),
            pl.BlockSpec((1, ROUTER_LANES), fixed),
        ],
        out_specs=[
            pl.BlockSpec((tm, D_MODEL), row),
            pl.BlockSpec((tm, D_MODEL), row),
            pl.BlockSpec((tm, ROUTER_LANES), row),
        ],
        out_shape=[
            jax.ShapeDtypeStruct((T, D_MODEL), F32),
            jax.ShapeDtypeStruct((T, D_MODEL), F32),
            jax.ShapeDtypeStruct((T, ROUTER_LANES), F32),
        ],
        compiler_params=_cparams(("parallel",)),
        name="out_proj_router",
    )(att, rec, x2, w_out_bf, g_ffn, wr_hi, wr_lo, rbias)


def _split3_f32(w):
    hi = w.astype(BF16).astype(F32)
    r1 = w - hi
    mid = r1.astype(BF16).astype(F32)
    lo = (r1 - mid).astype(BF16).astype(F32)
    return hi, mid, lo


def _route_sort_kernel(rt_ref, ltri_ref, utri_ref, lists_ref, meta_ref):
    rt = rt_ref[...]
    bt = rt.shape[0]
    lane = lax.broadcasted_iota(jnp.int32, rt.shape, 1)
    lane_f = lane.astype(F32)
    oh1 = lane_f == rt[:, 0:1]
    oh2 = lane_f == rt[:, 1:2]
    oh = jnp.where(oh1, 1.0, jnp.where(oh2, 1.0, 0.0))
    oh_b = oh.astype(BF16)
    rank = jnp.dot(ltri_ref[...], oh_b, preferred_element_type=F32)
    below = jnp.dot(oh_b, utri_ref[...], preferred_element_type=F32)
    offs = jnp.sum(below, axis=0, keepdims=True)
    cnt = jnp.sum(oh, axis=0, keepdims=True)
    posmat = rank + offs
    pos1 = jnp.sum(jnp.where(oh1, posmat, 0.0), axis=1, keepdims=True)
    pos2 = jnp.sum(jnp.where(oh2, posmat, 0.0), axis=1, keepdims=True)
    tok = lax.broadcasted_iota(jnp.int32, rt.shape, 0)
    tok_hi = (tok >> 5).astype(F32)
    tok_lo = (tok & 31).astype(F32)

    def record(w, slot):
        hi, mid, lo = _split3_f32(w)
        d = jnp.where(lane == 0, tok_hi, jnp.where(lane == 1, tok_lo, jnp.where(
            lane == 2, hi, jnp.where(lane == 3, mid, jnp.where(lane == 4, lo, jnp.where(
                lane == 5, slot, 0.0))))))
        return d.T.astype(BF16)

    d1 = record(rt[:, 2:3], 0.0)
    d2 = record(rt[:, 3:4], 1.0)
    cw = 512
    for c in range(2 * bt // cw):
        colp = (lax.broadcasted_iota(jnp.int32, (bt, cw), 1) + c * cw).astype(F32)
        o1 = jnp.where(colp == pos1, 1.0, 0.0).astype(BF16)
        o2 = jnp.where(colp == pos2, 1.0, 0.0).astype(BF16)
        srt = (jnp.dot(d1, o1, preferred_element_type=F32)
               + jnp.dot(d2, o2, preferred_element_type=F32))
        lists_ref[0, :, c * cw:(c + 1) * cw] = srt[0:8]
    row = lax.broadcasted_iota(jnp.int32, (8, LANES), 0)
    meta_ref[0] = jnp.where(row == 0, cnt, jnp.where(row == 1, offs, 0.0))


def _route_sort(route, bt):
    T = route.shape[0]
    nb = T // bt
    ltri = jnp.asarray(np.tril(np.ones((bt, bt), np.float32), -1), BF16)
    utri = jnp.asarray(np.triu(np.ones((LANES, LANES), np.float32), 1), BF16)
    return pl.pallas_call(
        _route_sort_kernel,
        grid=(nb,),
        in_specs=[
            pl.BlockSpec((bt, ROUTER_LANES), lambda b: (b, 0)),
            pl.BlockSpec((bt, bt), lambda b: (0, 0)),
            pl.BlockSpec((LANES, LANES), lambda b: (0, 0)),
        ],
        out_specs=[
            pl.BlockSpec((1, 8, 2 * bt), lambda b: (b, 0, 0)),
            pl.BlockSpec((1, 8, LANES), lambda b: (b, 0, 0)),
        ],
        out_shape=[
            jax.ShapeDtypeStruct((nb, 8, 2 * bt), F32),
            jax.ShapeDtypeStruct((nb, 8, LANES), F32),
        ],
        compiler_params=_cparams(("parallel",)),
        name="route_sort",
    )(route, ltri, utri)


def _moe_kernel(cnt_ref, off_ref, idx_ref, ws_ref, xt_ref, hres_ref, wg_ref, wu_ref, wd_ref, gf_ref,
                o_ref, yb_ref, xs_ref, y_ref):
    b = pl.program_id(0)
    e = pl.program_id(1)
    bt = xt_ref.shape[0]
    rows = xs_ref.shape[0]

    @pl.when(e == 0)
    def _():
        xs_ref[...] = jnp.zeros_like(xs_ref)

    n = cnt_ref[b * N_EXPERTS + e]
    base = b * 2 * bt + off_ref[b * N_EXPERTS + e]

    def tile(r, carry):
        r0 = base + r * rows
        m = jnp.minimum(n - r * rows, rows)
        trips = (m + MOE_UNROLL - 1) // MOE_UNROLL

        def gather(i, c):
            src = []
            for k in range(MOE_UNROLL):
                t = idx_ref[r0 + i * MOE_UNROLL + k] & (bt - 1)
                src.append(xt_ref[pl.ds(t, 1), :])
            xs_ref[pl.ds(pl.multiple_of(i * MOE_UNROLL, MOE_UNROLL), MOE_UNROLL), :] = (
                jnp.concatenate(src, axis=0))
            return c

        lax.fori_loop(0, trips, gather, 0)
        xs = xs_ref[...].astype(BF16)
        a = jnp.dot(xs, wg_ref[0], preferred_element_type=F32)
        u = jnp.dot(xs, wu_ref[0], preferred_element_type=F32)
        hid = (a * jax.nn.sigmoid(a)) * u
        y_ref[...] = jnp.dot(hid.astype(BF16), wd_ref[0], preferred_element_type=F32)

        def scatter(i, c):
            for k in range(MOE_UNROLL):
                j = i * MOE_UNROLL + k
                dst = jnp.where(j < m, idx_ref[r0 + j], 2 * bt + k)
                yb_ref[pl.ds(dst, 1), :] = ws_ref[r0 + j] * y_ref[pl.ds(j, 1), :]
            return c

        lax.fori_loop(0, trips, scatter, 0)
        return carry

    lax.fori_loop(0, (n + rows - 1) // rows, tile, 0)

    @pl.when(e == pl.num_programs(1) - 1)
    def _():
        y = hres_ref[...] + yb_ref[0:bt, :] + yb_ref[bt:2 * bt, :]
        ms = jnp.mean(y * y, axis=-1, keepdims=True)
        o_ref[...] = y * lax.rsqrt(ms + NORM_EPS) * gf_ref[...]


def _moe(cnt, off, idx, ws, xt, hres, wg, wu, wd, g_final, bt):
    T = xt.shape[0]
    row = lambda i, e, *_: (i, 0)
    grid_spec = pltpu.PrefetchScalarGridSpec(
        num_scalar_prefetch=4,
        grid=(T // bt, N_EXPERTS),
        in_specs=[
            pl.BlockSpec((bt, D_MODEL), row, pipeline_mode=pl.Buffered(1)),
            pl.BlockSpec((bt, D_MODEL), row, pipeline_mode=pl.Buffered(1)),
            pl.BlockSpec((1, D_MODEL, D_EXPERT), lambda i, e, *_: (e, 0, 0)),
            pl.BlockSpec((1, D_MODEL, D_EXPERT), lambda i, e, *_: (e, 0, 0)),
            pl.BlockSpec((1, D_EXPERT, D_MODEL), lambda i, e, *_: (e, 0, 0)),
            pl.BlockSpec((1, D_MODEL), lambda i, e, *_: (0, 0)),
        ],
        out_specs=pl.BlockSpec((bt, D_MODEL), row, pipeline_mode=pl.Buffered(1)),
        scratch_shapes=[
            pltpu.VMEM((2 * bt + MOE_UNROLL, D_MODEL), F32),
            pltpu.VMEM((MOE_ROWS, D_MODEL), F32),
            pltpu.VMEM((MOE_ROWS, D_MODEL), F32),
        ],
    )
    return pl.pallas_call(
        _moe_kernel,
        grid_spec=grid_spec,
        out_shape=jax.ShapeDtypeStruct((T, D_MODEL), F32),
        compiler_params=_cparams(("parallel", "arbitrary")),
        name="moe",
    )(cnt, off, idx, ws, xt, hres, wg, wu, wd, g_final)


def _rope_tables(T):
    inv = ROPE_THETA ** (-np.arange(0, ATT_QKDIM, 2, dtype=np.float64) / ATT_QKDIM)
    na = -(-T // ROPE_SPLIT)
    ang_a = (np.arange(na, dtype=np.float64) * ROPE_SPLIT)[:, None] * inv[None, :]
    ang_b = np.arange(ROPE_SPLIT, dtype=np.float64)[:, None] * inv[None, :]
    ca, sa = jnp.asarray(np.cos(ang_a), F32)[:, None, :], jnp.asarray(np.sin(ang_a), F32)[:, None, :]
    cb, sb = jnp.asarray(np.cos(ang_b), F32)[None, :, :], jnp.asarray(np.sin(ang_b), F32)[None, :, :]
    c = (ca * cb - sa * sb).reshape(na * ROPE_SPLIT, -1)[:T]
    s = (sa * cb + ca * sb).reshape(na * ROPE_SPLIT, -1)[:T]
    return (jnp.concatenate([c, c, c, c], axis=1), jnp.concatenate([-s, s, -s, s], axis=1),
            c.T, s.T)


def kernel(x, w_in, w_out, g_mix, lam_params, subln_g, hgrn_gnorm_g, hgrn_lb, g_ffn, w_gr, b_gr,
           w_er, b_er, w_gate, w_up, w_down, g_final):
    B, T, D = x.shape
    x2 = x.reshape(B * T, D)
    l = 0
    w_in_bf = w_in[l].astype(BF16)
    w_out_bf = w_out[l].astype(BF16)
    lp = lam_params[l].astype(F32)
    lam = (jnp.exp(jnp.sum(lp[0] * lp[1])) - jnp.exp(jnp.sum(lp[2] * lp[3])) + LAMBDA_INIT).reshape(1, 1)
    lb = jnp.cumsum(jax.nn.softmax(hgrn_lb.astype(F32), axis=1), axis=1)[:, l]
    w_qv_t = jnp.concatenate([w_in[l][:, 0:ATT_WIDTH], w_in[l][:, 2 * ATT_WIDTH:3 * ATT_WIDTH]],
                             axis=1).T.astype(BF16)
    cos_t, sin_t, cos_tt, sin_tt = _rope_tables(T)
    w_r = jnp.concatenate([jnp.transpose(w_er[l], (1, 0, 2)).reshape(D, N_EXPERTS), w_gr[l],
                           jnp.zeros((D, ROUTER_LANES - N_EXPERTS - N_GROUPS), F32)], axis=1)
    wr_hi = w_r.astype(BF16)
    wr_lo = (w_r - wr_hi.astype(F32)).astype(BF16)
    rbias = jnp.concatenate([b_er[l].reshape(-1), b_gr[l],
                             jnp.zeros((ROUTER_LANES - N_EXPERTS - N_GROUPS,), F32)]).reshape(1, -1)
    wg = w_gate[l].reshape(N_EXPERTS, D, D_EXPERT).astype(BF16)
    wu = w_up[l].reshape(N_EXPERTS, D, D_EXPERT).astype(BF16)
    wd = w_down[l].reshape(N_EXPERTS, D_EXPERT, D).astype(BF16)

    tk = min(ATT_TK, (B * T) // 2)
    qt, k, vt, hgrn_in = _in_proj(x2, g_mix[l].reshape(1, D), w_in_bf, w_qv_t, cos_t, sin_t,
                                  cos_tt, sin_tt, tk)
    att = _diff_attn(lam, qt, k, vt, subln_g[l].reshape(-1, 1))
    o_b = _hgrn_pass(hgrn_in, lb[1:2], backward=True)
    rec = _hgrn_pass(hgrn_in, lb[0:1], backward=False, o_b=o_b,
                     gnorm_g=hgrn_gnorm_g[l].reshape(1, -1))
    hres, xt, route = _out_proj(att, rec, x2, w_out_bf, g_ffn[l].reshape(1, D), wr_hi, wr_lo, rbias)
    bt = min(MOE_BT, B * T)
    lists, meta = _route_sort(route, bt)
    pad = jnp.zeros((MOE_UNROLL,), F32)
    idx = jnp.concatenate([(lists[:, 5] * bt + lists[:, 0] * 32.0 + lists[:, 1]).reshape(-1),
                           pad]).astype(jnp.int32)
    ws = jnp.concatenate([(lists[:, 2] + lists[:, 3] + lists[:, 4]).reshape(-1), pad])
    cnt = meta[:, 0, :N_EXPERTS].astype(jnp.int32).reshape(-1)
    off = meta[:, 1, :N_EXPERTS].astype(jnp.int32).reshape(-1)
    out = _moe(cnt, off, idx, ws, xt, hres, wg, wu, wd, g_final.reshape(1, D), bt)
    return out.reshape(B, T, D)
```

```python
import functools
import math

import numpy as np
import jax
import jax.numpy as jnp
from jax import lax
from jax.experimental import pallas as pl
from jax.experimental.pallas import tpu as pltpu

D_MODEL = 1024
ATT_WIDTH = 512
ATT_HEADS = 4
ATT_VDIM = 128
ATT_QKDIM = 64
HGRN_WIDTH = 512
HGRN_HEADS = 4
HGRN_VDIM = 128
HGRN_EXPAND = 128
HGRN_FDIM = 512
N_GROUPS = 4
EXPERTS_PER_GROUP = 8
N_EXPERTS = N_GROUPS * EXPERTS_PER_GROUP
D_EXPERT = 512
ROPE_THETA = 10000.0
NORM_EPS = 1e-6
SUBLN_EPS = 1e-5
LAMBDA_INIT = 0.8 - 0.6 * math.exp(-0.3 * 0)
LOG2E = math.log2(math.e)
ROPE_SPLIT = 256
IN_COLS = 3 * ATT_WIDTH + 3 * HGRN_FDIM + 2 * HGRN_WIDTH
HGRN_COLS = IN_COLS - 3 * ATT_WIDTH

LANES = 128
VMEM_LIMIT = 56 * 1024 * 1024
HGRN_CHUNK = 64
IN_TM = 512
ATT_TK = 1024
ATT_CB = 256
ATT_KT = 256
MOE_BT = 2048
MOE_ROWS = 256
MOE_UNROLL = 8
ATT_VPAD = ATT_VDIM + 16
ROUTER_LANES = LANES

BF16 = jnp.bfloat16
F32 = jnp.float32


def _cparams(sem):
    return pltpu.CompilerParams(dimension_semantics=sem, vmem_limit_bytes=VMEM_LIMIT)


def _rot_half_64(x):
    lane = lax.broadcasted_iota(jnp.int32, x.shape, 1)
    fwd = pltpu.roll(x, 32, axis=1)
    bwd = pltpu.roll(x, 96, axis=1)
    return jnp.where((lane & 63) < 32, bwd, fwd)


def _inproj_kernel(x_ref, g_ref, w_ref, wqvt_ref, cos_ref, sin_ref, cost_ref, sint_ref,
                   qt_ref, k_ref, vt_ref, hg_ref):
    x = x_ref[...]
    tm = x.shape[0]
    ms = jnp.mean(x * x, axis=-1, keepdims=True)
    h = (x * lax.rsqrt(ms + NORM_EPS) * g_ref[...]).astype(BF16)
    nt = (((1,), (1,)), ((), ()))
    half = ATT_QKDIM // 2
    ct = cost_ref[...] * (ATT_QKDIM ** -0.5 * LOG2E)
    st = sint_ref[...] * (ATT_QKDIM ** -0.5 * LOG2E)
    for j in range(ATT_WIDTH // LANES):
        a = lax.dot_general(wqvt_ref[j * LANES:(j + 1) * LANES, :], h, nt,
                            preferred_element_type=F32)
        for c in range(LANES // ATT_QKDIM):
            x1 = a[c * ATT_QKDIM:c * ATT_QKDIM + half]
            x2 = a[c * ATT_QKDIM + half:(c + 1) * ATT_QKDIM]
            lo = j * LANES + c * ATT_QKDIM
            qt_ref[lo:lo + half, :] = (x1 * ct - x2 * st).astype(BF16)
            qt_ref[lo + half:lo + ATT_QKDIM, :] = (x1 * st + x2 * ct).astype(BF16)
    ones = jnp.ones((ATT_VPAD - ATT_VDIM, tm), BF16)
    for j in range(ATT_HEADS):
        lo = ATT_WIDTH + j * ATT_VDIM
        vt = lax.dot_general(wqvt_ref[lo:lo + ATT_VDIM, :], h, nt, preferred_element_type=F32)
        vt_ref[0, j * ATT_VPAD:j * ATT_VPAD + ATT_VDIM, :] = vt.astype(BF16)
        vt_ref[0, j * ATT_VPAD + ATT_VDIM:(j + 1) * ATT_VPAD, :] = ones
    cosv = cos_ref[...]
    sinv = sin_ref[...]
    for j in range(ATT_WIDTH // LANES):
        lo = ATT_WIDTH + j * LANES
        a = jnp.dot(h, w_ref[:, lo:lo + LANES], preferred_element_type=F32)
        k_ref[:, j * LANES:(j + 1) * LANES] = (a * cosv + _rot_half_64(a) * sinv).astype(BF16)
    for j in range(HGRN_COLS // 512):
        lo = 3 * ATT_WIDTH + j * 512
        hg_ref[:, j * 512:(j + 1) * 512] = jnp.dot(h, w_ref[:, lo:lo + 512],
                                                   preferred_element_type=F32)


def _in_proj(x2, g_mix, w_in_bf, w_qv_t, cos_t, sin_t, cos_tt, sin_tt, tk):
    T = x2.shape[0]
    tm = IN_TM
    per = tk // tm
    return pl.pallas_call(
        _inproj_kernel,
        grid=(T // tm,),
        in_specs=[
            pl.BlockSpec((tm, D_MODEL), lambda i: (i, 0)),
            pl.BlockSpec((1, D_MODEL), lambda i: (0, 0)),
            pl.BlockSpec((D_MODEL, IN_COLS), lambda i: (0, 0)),
            pl.BlockSpec((2 * ATT_WIDTH, D_MODEL), lambda i: (0, 0)),
            pl.BlockSpec((tm, LANES), lambda i: (i, 0)),
            pl.BlockSpec((tm, LANES), lambda i: (i, 0)),
            pl.BlockSpec((ATT_QKDIM // 2, tm), lambda i: (0, i)),
            pl.BlockSpec((ATT_QKDIM // 2, tm), lambda i: (0, i)),
        ],
        out_specs=[
            pl.BlockSpec((ATT_WIDTH, tm), lambda i: (0, i)),
            pl.BlockSpec((tm, ATT_WIDTH), lambda i: (i, 0)),
            pl.BlockSpec((1, ATT_HEADS * ATT_VPAD, tm), lambda i: (i // per, 0, i % per)),
            pl.BlockSpec((tm, HGRN_COLS), lambda i: (i, 0)),
        ],
        out_shape=[
            jax.ShapeDtypeStruct((ATT_WIDTH, T), BF16),
            jax.ShapeDtypeStruct((T, ATT_WIDTH), BF16),
            jax.ShapeDtypeStruct((T // tk, ATT_HEADS * ATT_VPAD, tk), BF16),
            jax.ShapeDtypeStruct((T, HGRN_COLS), F32),
        ],
        compiler_params=_cparams(("parallel",)),
        name="in_proj",
    )(x2, g_mix, w_in_bf, w_qv_t, cos_t, sin_t, cos_tt, sin_tt)


def _attn_kernel(lam_ref, qt_ref, qtn_ref, k_ref, vt_ref, g_ref, o_ref,
                 qw_ref, qwn_ref, sa_ref, sb_ref, acc_ref, st_ref):
    tq = qt_ref.shape[1]
    nchunk, _, tk = vt_ref.shape
    ncb = 2 * tq // ATT_CB

    def latch(dst_ref, src_ref):
        qt = src_ref[...].astype(F32)
        row = lax.broadcasted_iota(jnp.int32, qt.shape, 0)
        dst_ref[:, 0:tq] = jnp.where(row < ATT_QKDIM, qt, 0.0).astype(BF16)
        dst_ref[:, tq:] = jnp.where(row >= ATT_QKDIM, qt, 0.0).astype(BF16)

    latch(qw_ref, qt_ref)
    latch(qwn_ref, qtn_ref)
    acc_ref[...] = jnp.zeros_like(acc_ref)

    def stage(j_acc, s_acc, m_cur, alpha, j_sc, s_sc, m_old, q_ref):
        off = j_sc * tk
        m_new, al_new = [], []
        for cb in range(ncb):
            cols = slice(cb * ATT_CB, (cb + 1) * ATT_CB)
            part, cmax = None, None
            for t in range(tk // ATT_KT):
                rows = slice(t * ATT_KT, (t + 1) * ATT_KT)
                kc = k_ref[pl.ds(pl.multiple_of(off + t * ATT_KT, ATT_KT), ATT_KT), :]
                s = jnp.dot(kc, q_ref[:, cols], preferred_element_type=F32)
                s_sc[rows, cols] = s
                c = jnp.max(s, axis=0, keepdims=True)
                cmax = c if cmax is None else jnp.maximum(cmax, c)
                if s_acc is not None:
                    p = jnp.exp2(s_acc[rows, cols] - m_cur[cb]).astype(BF16)
                    d = jnp.dot(vt_ref[j_acc, :, rows], p, preferred_element_type=F32)
                    part = d if part is None else part + d
            if s_acc is not None:
                acc_ref[:, cols] = alpha[cb] * acc_ref[:, cols] + part
            mn = jnp.maximum(m_old[cb], cmax)
            m_new.append(mn)
            al_new.append(jnp.exp2(m_old[cb] - mn))
        return tuple(m_new), tuple(al_new)

    def save(st):
        for cb in range(ncb):
            cols = slice(cb * ATT_CB, (cb + 1) * ATT_CB)
            st_ref[0:1, cols] = st[0][cb]
            st_ref[1:2, cols] = st[1][cb]

    m0 = tuple(jnp.full((1, ATT_CB), -jnp.inf, F32) for _ in range(ncb))

    @pl.when(pl.program_id(1) == 0)
    def _():
        save(stage(0, None, None, None, 0, sa_ref, m0, qw_ref))

    state = (tuple(st_ref[0:1, cb * ATT_CB:(cb + 1) * ATT_CB] for cb in range(ncb)),
             tuple(st_ref[1:2, cb * ATT_CB:(cb + 1) * ATT_CB] for cb in range(ncb)))

    per_trip = 4 if nchunk % 4 == 0 else 2
    ntrip = nchunk // per_trip
    bufs = (sa_ref, sb_ref)

    def trip(tr, st, last):
        m_cur, al_cur = st
        for u in range(per_trip):
            j = per_trip * tr + u
            if last and u == per_trip - 1:
                m_cur, al_cur = stage(j, bufs[u % 2], m_cur, al_cur, 0, bufs[(u + 1) % 2], m0,
                                      qwn_ref)
            else:
                m_cur, al_cur = stage(j, bufs[u % 2], m_cur, al_cur, j + 1, bufs[(u + 1) % 2],
                                      m_cur, qw_ref)
        return m_cur, al_cur

    state = lax.fori_loop(0, ntrip - 1, lambda tr, st: trip(tr, st, False), state)
    save(trip(ntrip - 1, state, True))

    lam = lam_ref[0, 0]
    acc = acc_ref[...]
    o1 = acc[0:ATT_VDIM, 0:tq] / acc[ATT_VDIM:ATT_VDIM + 1, 0:tq]
    o2 = acc[0:ATT_VDIM, tq:] / acc[ATT_VDIM:ATT_VDIM + 1, tq:]
    o = o1 - lam * o2
    ms = jnp.mean(o * o, axis=0, keepdims=True)
    o = o * lax.rsqrt(ms + SUBLN_EPS) * g_ref[...] * (1.0 - LAMBDA_INIT)
    o_ref[...] = o.T.astype(o_ref.dtype)


def _diff_attn(lam, qt, k, vt, subln_g, tq=512):
    T = k.shape[0]
    tq = min(tq, T)
    nq = T // tq
    nchunk, _, tk = vt.shape
    return pl.pallas_call(
        _attn_kernel,
        grid=(ATT_HEADS, nq),
        in_specs=[
            pl.BlockSpec(memory_space=pltpu.SMEM),
            pl.BlockSpec((LANES, tq), lambda h, i: (h, i)),
            pl.BlockSpec((LANES, tq), lambda h, i: (h, jnp.minimum(i + 1, nq - 1))),
            pl.BlockSpec((T, LANES), lambda h, i: (0, h)),
            pl.BlockSpec((nchunk, ATT_VPAD, tk), lambda h, i: (0, h, 0)),
            pl.BlockSpec((ATT_VDIM, 1), lambda h, i: (0, 0)),
        ],
        out_specs=pl.BlockSpec((tq, LANES), lambda h, i: (i, h)),
        out_shape=jax.ShapeDtypeStruct((T, ATT_WIDTH), BF16),
        scratch_shapes=[
            pltpu.VMEM((LANES, 2 * tq), BF16),
            pltpu.VMEM((LANES, 2 * tq), BF16),
            pltpu.VMEM((tk, 2 * tq), F32),
            pltpu.VMEM((tk, 2 * tq), F32),
            pltpu.VMEM((ATT_VPAD, 2 * tq), F32),
            pltpu.VMEM((8, 2 * tq), F32),
        ],
        compiler_params=_cparams(("arbitrary", "arbitrary")),
        name="diff_attn",
    )(lam, qt, qt, k, vt, subln_g)


def _hgrn_consts(C, backward):
    halves = []
    h = C // 2
    while h >= 1:
        halves.append(h)
        h //= 2
    E = np.zeros((2 + len(halves), C, C), np.float32)
    M = np.zeros((len(halves) + 1, C, C), np.float32)
    idx = np.arange(C)
    for t in range(C):
        if not backward:
            E[0, t, idx <= t] = 1.0
            E[1, t, idx > t] = 1.0
        else:
            E[0, t, idx >= t] = 1.0
            E[1, t, idx < t] = 1.0
    for li, h in enumerate(halves):
        for t in range(C):
            mid = (t // (2 * h)) * 2 * h + h
            upper = t >= mid
            if not backward:
                if upper:
                    E[2 + li, t, (idx >= mid) & (idx <= t)] = 1.0
                else:
                    E[2 + li, t, (idx > t) & (idx <= mid - 1)] = 1.0
            else:
                if not upper:
                    E[2 + li, t, (idx >= t) & (idx <= mid - 1)] = 1.0
                else:
                    E[2 + li, t, (idx >= mid) & (idx <= t - 1)] = 1.0
            for s in range(C):
                same = (s // (2 * h)) == (t // (2 * h))
                s_upper = s >= mid
                if same and ((not backward and upper and not s_upper)
                             or (backward and not upper and s_upper)):
                    M[li, t, s] = 1.0
    M[-1] = np.eye(C, dtype=np.float32)
    E = np.concatenate([E.reshape(-1, C), np.ones((8, C), np.float32)], axis=0)
    return E, M


def _split3_rows(g):
    hi = g.astype(BF16)
    r1 = g - hi.astype(F32)
    mid = r1.astype(BF16)
    lo = (r1 - mid.astype(F32)).astype(BF16)
    return jnp.concatenate([hi, mid, lo], axis=0)


def _hgrn_kernel(*refs, backward, nchunk, final):
    if final:
        (hq_ref, hf_ref, hi_ref, lb_ref, e_ref, m_ref, ob_ref, hgate_ref, gn_ref,
         o_ref, st_ref) = refs
    else:
        hq_ref, hf_ref, hi_ref, lb_ref, e_ref, m_ref, o_ref, st_ref = refs
    C = HGRN_CHUNK
    nlev = m_ref.shape[0] - 1
    nt = (((1,), (1,)), ((), ()))

    @pl.when(pl.program_id(0) == 0)
    def _():
        st_ref[...] = jnp.zeros_like(st_ref)

    lbd = lb_ref[...]
    emat = e_ref[...]
    order = range(nchunk - 1, -1, -1) if backward else range(nchunk)
    for c in order:
        rows = pl.ds(c * C, C)
        z = hf_ref[rows, :]
        hq = hq_ref[rows, :]
        f = lbd + (1.0 - lbd) * jax.nn.sigmoid(z)
        kk = (1.0 - lbd) * jax.nn.sigmoid(-z)
        g = jnp.log(f)
        q = hq * jax.nn.sigmoid(hq)
        vb = hi_ref[rows, :].astype(BF16)
        ex = jnp.dot(emat, _split3_rows(g), preferred_element_type=F32)
        ee = jnp.exp(ex)
        qi = (q * ee[0:C]).astype(BF16)
        ki = (kk * ee[C:2 * C]).astype(BF16)
        dec = ee[(2 + nlev) * C:(2 + nlev) * C + 1]
        qb = q.astype(BF16)
        kb = kk.astype(BF16)
        qlev = [(q * ee[(2 + l) * C:(3 + l) * C]).astype(BF16) for l in range(nlev)]
        klev = [(kk * ee[(2 + l) * C:(3 + l) * C]).astype(BF16) for l in range(nlev)]
        outs = []
        for hd in range(HGRN_HEADS):
            ln = slice(hd * LANES, (hd + 1) * LANES)
            a = m_ref[nlev] * lax.dot_general(qb[:, ln], kb[:, ln], nt, preferred_element_type=F32)
            for l in range(nlev):
                a = a + m_ref[l] * lax.dot_general(qlev[l][:, ln], klev[l][:, ln], nt,
                                                   preferred_element_type=F32)
            st = st_ref[hd]
            o = jnp.dot(a.astype(BF16), vb[:, ln], preferred_element_type=F32)
            o = o + lax.dot_general(qi[:, ln], st.astype(BF16), nt, preferred_element_type=F32)
            vt = hi_ref[rows, ln].T.astype(BF16)
            st_ref[hd] = st * dec[:, ln] + jnp.dot(vt, ki[:, ln], preferred_element_type=F32)
            outs.append(o)
        o_all = jnp.concatenate(outs, axis=1)
        if final:
            o_all = o_all + ob_ref[rows, :]
            res = []
            for hd in range(HGRN_HEADS):
                ln = slice(hd * LANES, (hd + 1) * LANES)
                oh = o_all[:, ln]
                ms = jnp.mean(oh * oh, axis=-1, keepdims=True)
                res.append(oh * lax.rsqrt(ms + NORM_EPS) * gn_ref[...])
            hg = hgate_ref[rows, :]
            o_all = jnp.concatenate(res, axis=1) * (hg * jax.nn.sigmoid(hg))
        o_ref[rows, :] = o_all.astype(o_ref.dtype)


def _hgrn_pass(hgrn_in, lb_row, backward, o_b=None, gnorm_g=None, tb=512):
    T = hgrn_in.shape[0]
    nblk = T // tb
    final = o_b is not None
    E, M = _hgrn_consts(HGRN_CHUNK, backward)
    e_bf = jnp.asarray(np.concatenate([E, E, E], axis=1), BF16)
    m_f = jnp.asarray(M, F32)
    blk = (lambda i: nblk - 1 - i) if backward else (lambda i: i)
    f_part = 2 if backward else 1
    in_specs = [
        pl.BlockSpec((tb, 512), lambda i: (blk(i), 0)),
        pl.BlockSpec((tb, 512), lambda i: (blk(i), f_part)),
        pl.BlockSpec((tb, 512), lambda i: (blk(i), 3)),
        pl.BlockSpec((1, 512), lambda i: (0, 0)),
        pl.BlockSpec(e_bf.shape, lambda i: (0, 0)),
        pl.BlockSpec(m_f.shape, lambda i: (0, 0, 0)),
    ]
    args = [hgrn_in, hgrn_in, hgrn_in, lb_row, e_bf, m_f]
    if final:
        in_specs += [
            pl.BlockSpec((tb, 512), lambda i: (blk(i), 0)),
            pl.BlockSpec((tb, 512), lambda i: (blk(i), 4)),
            pl.BlockSpec((1, HGRN_VDIM), lambda i: (0, 0)),
        ]
        args += [o_b, hgrn_in, gnorm_g]
    return pl.pallas_call(
        functools.partial(_hgrn_kernel, backward=backward, nchunk=tb // HGRN_CHUNK, final=final),
        grid=(nblk,),
        in_specs=in_specs,
        out_specs=pl.BlockSpec((tb, 512), lambda i: (blk(i), 0)),
        out_shape=jax.ShapeDtypeStruct((T, HGRN_WIDTH), BF16 if final else F32),
        scratch_shapes=[pltpu.VMEM((HGRN_HEADS, HGRN_VDIM, HGRN_EXPAND), F32)],
        compiler_params=_cparams(("arbitrary",)),
        name="hgrn_bwd" if backward else "hgrn_fwd",
    )(*args)


def _outproj_kernel(att_ref, rec_ref, x_ref, wo_ref, g_ref, wr_hi_ref, wr_lo_ref, rb_ref,
                    hres_ref, xt_ref, comb_ref):
    mix = (jnp.dot(att_ref[...], wo_ref[0:ATT_WIDTH, :], preferred_element_type=F32)
           + jnp.dot(rec_ref[...], wo_ref[ATT_WIDTH:, :], preferred_element_type=F32))
    hres = x_ref[...] + mix
    hres_ref[...] = hres
    ms = jnp.mean(hres * hres, axis=-1, keepdims=True)
    xt = hres * lax.rsqrt(ms + NORM_EPS) * g_ref[...]
    xt_ref[...] = xt
    xt_hi = xt.astype(BF16)
    xt_lo = (xt - xt_hi.astype(F32)).astype(BF16)
    logits = (jnp.dot(xt_hi, wr_hi_ref[...], preferred_element_type=F32)
              + jnp.dot(xt_hi, wr_lo_ref[...], preferred_element_type=F32)
              + jnp.dot(xt_lo, wr_hi_ref[...], preferred_element_type=F32)) + rb_ref[...]
    lane = lax.broadcasted_iota(jnp.int32, logits.shape, 1)
    neg = -jnp.inf
    big = jnp.int32(1 << 20)
    gmask = (lane >= N_EXPERTS) & (lane < N_EXPERTS + N_GROUPS)
    glog = jnp.where(gmask, logits, neg)
    gmax = jnp.max(glog, axis=-1, keepdims=True)
    gsum = jnp.sum(jnp.exp(glog - gmax), axis=-1, keepdims=True)
    p_g = 1.0 / gsum
    g_idx = jnp.min(jnp.where(glog == gmax, lane, big), axis=-1, keepdims=True) - N_EXPERTS
    emask = (lane < N_EXPERTS) & ((lane >> 3) == g_idx)
    elog = jnp.where(emask, logits, neg)
    e1 = jnp.max(elog, axis=-1, keepdims=True)
    i1 = jnp.min(jnp.where(elog == e1, lane, big), axis=-1, keepdims=True)
    elog2 = jnp.where(lane == i1, neg, elog)
    e2 = jnp.max(elog2, axis=-1, keepdims=True)
    i2 = jnp.min(jnp.where(elog2 == e2, lane, big), axis=-1, keepdims=True)
    r = jnp.exp(e2 - e1)
    w1 = p_g / (1.0 + r)
    w2 = p_g * r / (1.0 + r)
    comb_ref[...] = jnp.where(lane == 0, i1.astype(F32), jnp.where(
        lane == 1, i2.astype(F32), jnp.where(lane == 2, w1, jnp.where(lane == 3, w2, 0.0))))


def _out_proj(att, rec, x2, w_out_bf, g_ffn, wr_hi, wr_lo, rbias, tm=512):
    T = x2.shape[0]
    row = lambda i: (i, 0)
    fixed = lambda i: (0, 0)
    return pl.pallas_call(
        _outproj_kernel,
        grid=(T // tm,),
        in_specs=[
            pl.BlockSpec((tm, ATT_WIDTH), row),
            pl.BlockSpec((tm, HGRN_WIDTH), row),
            pl.BlockSpec((tm, D_MODEL), row),
            pl.BlockSpec((D_MODEL, D_MODEL), fixed),
            pl.BlockSpec((1, D_MODEL), fixed),
            pl.BlockSpec((D_MODEL, ROUTER_LANES), fixed),
            pl.BlockSpec((D_MODEL, ROUTER_LANES), fixed),
            pl.BlockSpec((1, ROUTER_LANES), fixed),
        ],
        out_specs=[
            pl.BlockSpec((tm, D_MODEL), row),
            pl.BlockSpec((tm, D_MODEL), row),
            pl.BlockSpec((tm, ROUTER_LANES), row),
        ],
        out_shape=[
            jax.ShapeDtypeStruct((T, D_MODEL), F32),
            jax.ShapeDtypeStruct((T, D_MODEL), F32),
            jax.ShapeDtypeStruct((T, ROUTER_LANES), F32),
        ],
        compiler_params=_cparams(("parallel",)),
        name="out_proj_router",
    )(att, rec, x2, w_out_bf, g_ffn, wr_hi, wr_lo, rbias)


def _split3_f32(w):
    hi = w.astype(BF16).astype(F32)
    r1 = w - hi
    mid = r1.astype(BF16).astype(F32)
    lo = (r1 - mid).astype(BF16).astype(F32)
    return hi, mid, lo


def _route_sort_kernel(rt_ref, ltri_ref, utri_ref, lists_ref, meta_ref):
    rt = rt_ref[...]
    bt = rt.shape[0]
    lane = lax.broadcasted_iota(jnp.int32, rt.shape, 1)
    lane_f = lane.astype(F32)
    oh1 = lane_f == rt[:, 0:1]
    oh2 = lane_f == rt[:, 1:2]
    oh = jnp.where(oh1, 1.0, jnp.where(oh2, 1.0, 0.0))
    oh_b = oh.astype(BF16)
    rank = jnp.dot(ltri_ref[...], oh_b, preferred_element_type=F32)
    below = jnp.dot(oh_b, utri_ref[...], preferred_element_type=F32)
    offs = jnp.sum(below, axis=0, keepdims=True)
    cnt = jnp.sum(oh, axis=0, keepdims=True)
    posmat = rank + offs
    pos1 = jnp.sum(jnp.where(oh1, posmat, 0.0), axis=1, keepdims=True)
    pos2 = jnp.sum(jnp.where(oh2, posmat, 0.0), axis=1, keepdims=True)
    tok = lax.broadcasted_iota(jnp.int32, rt.shape, 0)
    tok_hi = (tok >> 5).astype(F32)
    tok_lo = (tok & 31).astype(F32)

    def record(w, slot):
        hi, mid, lo = _split3_f32(w)
        d = jnp.where(lane == 0, tok_hi, jnp.where(lane == 1, tok_lo, jnp.where(
            lane == 2, hi, jnp.where(lane == 3, mid, jnp.where(lane == 4, lo, jnp.where(
                lane == 5, slot, 0.0))))))
        return d.T.astype(BF16)

    d1 = record(rt[:, 2:3], 0.0)
    d2 = record(rt[:, 3:4], 1.0)
    cw = 512
    for c in range(2 * bt // cw):
        colp = (lax.broadcasted_iota(jnp.int32, (bt, cw), 1) + c * cw).astype(F32)
        o1 = jnp.where(colp == pos1, 1.0, 0.0).astype(BF16)
        o2 = jnp.where(colp == pos2, 1.0, 0.0).astype(BF16)
        srt = (jnp.dot(d1, o1, preferred_element_type=F32)
               + jnp.dot(d2, o2, preferred_element_type=F32))
        lists_ref[0, :, c * cw:(c + 1) * cw] = srt[0:8]
    row = lax.broadcasted_iota(jnp.int32, (8, LANES), 0)
    meta_ref[0] = jnp.where(row == 0, cnt, jnp.where(row == 1, offs, 0.0))


def _route_sort(route, bt):
    T = route.shape[0]
    nb = T // bt
    ltri = jnp.asarray(np.tril(np.ones((bt, bt), np.float32), -1), BF16)
    utri = jnp.asarray(np.triu(np.ones((LANES, LANES), np.float32), 1), BF16)
    return pl.pallas_call(
        _route_sort_kernel,
        grid=(nb,),
        in_specs=[
            pl.BlockSpec((bt, ROUTER_LANES), lambda b: (b, 0)),
            pl.BlockSpec((bt, bt), lambda b: (0, 0)),
            pl.BlockSpec((LANES, LANES), lambda b: (0, 0)),
        ],
        out_specs=[
            pl.BlockSpec((1, 8, 2 * bt), lambda b: (b, 0, 0)),
            pl.BlockSpec((1, 8, LANES), lambda b: (b, 0, 0)),
        ],
        out_shape=[
            jax.ShapeDtypeStruct((nb, 8, 2 * bt), F32),
            jax.ShapeDtypeStruct((nb, 8, LANES), F32),
        ],
        compiler_params=_cparams(("parallel",)),
        name="route_sort",
    )(route, ltri, utri)


def _moe_kernel(cnt_ref, off_ref, idx_ref, ws_ref, xt_ref, hres_ref, wg_ref, wu_ref, wd_ref, gf_ref,
                o_ref, yb_ref, xs_ref, y_ref):
    b = pl.program_id(0)
    e = pl.program_id(1)
    bt = xt_ref.shape[0]
    rows = xs_ref.shape[0]

    @pl.when(e == 0)
    def _():
        xs_ref[...] = jnp.zeros_like(xs_ref)

    n = cnt_ref[b * N_EXPERTS + e]
    base = b * 2 * bt + off_ref[b * N_EXPERTS + e]

    def tile(r, carry):
        r0 = base + r * rows
        m = jnp.minimum(n - r * rows, rows)
        trips = (m + MOE_UNROLL - 1) // MOE_UNROLL

        def gather(i, c):
            src = []
            for k in range(MOE_UNROLL):
                t = idx_ref[r0 + i * MOE_UNROLL + k] & (bt - 1)
                src.append(xt_ref[pl.ds(t, 1), :])
            xs_ref[pl.ds(pl.multiple_of(i * MOE_UNROLL, MOE_UNROLL), MOE_UNROLL), :] = (
                jnp.concatenate(src, axis=0))
            return c

        lax.fori_loop(0, trips, gather, 0)
        xs = xs_ref[...].astype(BF16)
        a = jnp.dot(xs, wg_ref[0], preferred_element_type=F32)
        u = jnp.dot(xs, wu_ref[0], preferred_element_type=F32)
        hid = (a * jax.nn.sigmoid(a)) * u
        y_ref[...] = jnp.dot(hid.astype(BF16), wd_ref[0], preferred_element_type=F32)

        def scatter(i, c):
            for k in range(MOE_UNROLL):
                j = i * MOE_UNROLL + k
                dst = jnp.where(j < m, idx_ref[r0 + j], 2 * bt + k)
                yb_ref[pl.ds(dst, 1), :] = ws_ref[r0 + j] * y_ref[pl.ds(j, 1), :]
            return c

        lax.fori_loop(0, trips, scatter, 0)
        return carry

    lax.fori_loop(0, (n + rows - 1) // rows, tile, 0)

    @pl.when(e == pl.num_programs(1) - 1)
    def _():
        y = hres_ref[...] + yb_ref[0:bt, :] + yb_ref[bt:2 * bt, :]
        ms = jnp.mean(y * y, axis=-1, keepdims=True)
        o_ref[...] = y * lax.rsqrt(ms + NORM_EPS) * gf_ref[...]


def _moe(cnt, off, idx, ws, xt, hres, wg, wu, wd, g_final, bt):
    T = xt.shape[0]
    row = lambda i, e, *_: (i, 0)
    grid_spec = pltpu.PrefetchScalarGridSpec(
        num_scalar_prefetch=4,
        grid=(T // bt, N_EXPERTS),
        in_specs=[
            pl.BlockSpec((bt, D_MODEL), row, pipeline_mode=pl.Buffered(1)),
            pl.BlockSpec((bt, D_MODEL), row, pipeline_mode=pl.Buffered(1)),
            pl.BlockSpec((1, D_MODEL, D_EXPERT), lambda i, e, *_: (e, 0, 0)),
            pl.BlockSpec((1, D_MODEL, D_EXPERT), lambda i, e, *_: (e, 0, 0)),
            pl.BlockSpec((1, D_EXPERT, D_MODEL), lambda i, e, *_: (e, 0, 0)),
            pl.BlockSpec((1, D_MODEL), lambda i, e, *_: (0, 0)),
        ],
        out_specs=pl.BlockSpec((bt, D_MODEL), row, pipeline_mode=pl.Buffered(1)),
        scratch_shapes=[
            pltpu.VMEM((2 * bt + MOE_UNROLL, D_MODEL), F32),
            pltpu.VMEM((MOE_ROWS, D_MODEL), F32),
            pltpu.VMEM((MOE_ROWS, D_MODEL), F32),
        ],
    )
    return pl.pallas_call(
        _moe_kernel,
        grid_spec=grid_spec,
        out_shape=jax.ShapeDtypeStruct((T, D_MODEL), F32),
        compiler_params=_cparams(("parallel", "arbitrary")),
        name="moe",
    )(cnt, off, idx, ws, xt, hres, wg, wu, wd, g_final)


def _rope_tables(T):
    inv = ROPE_THETA ** (-np.arange(0, ATT_QKDIM, 2, dtype=np.float64) / ATT_QKDIM)
    na = -(-T // ROPE_SPLIT)
    ang_a = (np.arange(na, dtype=np.float64) * ROPE_SPLIT)[:, None] * inv[None, :]
    ang_b = np.arange(ROPE_SPLIT, dtype=np.float64)[:, None] * inv[None, :]
    ca, sa = jnp.asarray(np.cos(ang_a), F32)[:, None, :], jnp.asarray(np.sin(ang_a), F32)[:, None, :]
    cb, sb = jnp.asarray(np.cos(ang_b), F32)[None, :, :], jnp.asarray(np.sin(ang_b), F32)[None, :, :]
    c = (ca * cb - sa * sb).reshape(na * ROPE_SPLIT, -1)[:T]
    s = (sa * cb + ca * sb).reshape(na * ROPE_SPLIT, -1)[:T]
    return (jnp.concatenate([c, c, c, c], axis=1), jnp.concatenate([-s, s, -s, s], axis=1),
            c.T, s.T)


def kernel(x, w_in, w_out, g_mix, lam_params, subln_g, hgrn_gnorm_g, hgrn_lb, g_ffn, w_gr, b_gr,
           w_er, b_er, w_gate, w_up, w_down, g_final):
    B, T, D = x.shape
    x2 = x.reshape(B * T, D)
    l = 0
    w_in_bf = w_in[l].astype(BF16)
    w_out_bf = w_out[l].astype(BF16)
    lp = lam_params[l].astype(F32)
    lam = (jnp.exp(jnp.sum(lp[0] * lp[1])) - jnp.exp(jnp.sum(lp[2] * lp[3])) + LAMBDA_INIT).reshape(1, 1)
    lb = jnp.cumsum(jax.nn.softmax(hgrn_lb.astype(F32), axis=1), axis=1)[:, l]
    w_qv_t = jnp.concatenate([w_in[l][:, 0:ATT_WIDTH], w_in[l][:, 2 * ATT_WIDTH:3 * ATT_WIDTH]],
                             axis=1).T.astype(BF16)
    cos_t, sin_t, cos_tt, sin_tt = _rope_tables(T)
    w_r = jnp.concatenate([jnp.transpose(w_er[l], (1, 0, 2)).reshape(D, N_EXPERTS), w_gr[l],
                           jnp.zeros((D, ROUTER_LANES - N_EXPERTS - N_GROUPS), F32)], axis=1)
    wr_hi = w_r.astype(BF16)
    wr_lo = (w_r - wr_hi.astype(F32)).astype(BF16)
    rbias = jnp.concatenate([b_er[l].reshape(-1), b_gr[l],
                             jnp.zeros((ROUTER_LANES - N_EXPERTS - N_GROUPS,), F32)]).reshape(1, -1)
    wg = w_gate[l].reshape(N_EXPERTS, D, D_EXPERT).astype(BF16)
    wu = w_up[l].reshape(N_EXPERTS, D, D_EXPERT).astype(BF16)
    wd = w_down[l].reshape(N_EXPERTS, D_EXPERT, D).astype(BF16)

    tk = min(ATT_TK, (B * T) // 2)
    qt, k, vt, hgrn_in = _in_proj(x2, g_mix[l].reshape(1, D), w_in_bf, w_qv_t, cos_t, sin_t,
                                  cos_tt, sin_tt, tk)
    att = _diff_attn(lam, qt, k, vt, subln_g[l].reshape(-1, 1))
    o_b = _hgrn_pass(hgrn_in, lb[1:2], backward=True)
    rec = _hgrn_pass(hgrn_in, lb[0:1], backward=False, o_b=o_b,
                     gnorm_g=hgrn_gnorm_g[l].reshape(1, -1))
    hres, xt, route = _out_proj(att, rec, x2, w_out_bf, g_ffn[l].reshape(1, D), wr_hi, wr_lo, rbias)
    bt = min(MOE_BT, B * T)
    lists, meta = _route_sort(route, bt)
    pad = jnp.zeros((MOE_UNROLL,), F32)
    idx = jnp.concatenate([(lists[:, 5] * bt + lists[:, 0] * 32.0 + lists[:, 1]).reshape(-1),
                           pad]).astype(jnp.int32)
    ws = jnp.concatenate([(lists[:, 2] + lists[:, 3] + lists[:, 4]).reshape(-1), pad])
    cnt = meta[:, 0, :N_EXPERTS].astype(jnp.int32).reshape(-1)
    off = meta[:, 1, :N_EXPERTS].astype(jnp.int32).reshape(-1)
    out = _moe(cnt, off, idx, ws, xt, hres, wg, wu, wd, g_final.reshape(1, D), bt)
    return out.reshape(B, T, D)
```

```python
import functools
import math

import numpy as np
import jax
import jax.numpy as jnp
from jax import lax
from jax.experimental import pallas as pl
from jax.experimental.pallas import tpu as pltpu

D_MODEL = 1024
ATT_WIDTH = 512
ATT_HEADS = 4
ATT_VDIM = 128
ATT_QKDIM = 64
HGRN_WIDTH = 512
HGRN_HEADS = 4
HGRN_VDIM = 128
HGRN_EXPAND = 128
HGRN_FDIM = 512
N_GROUPS = 4
EXPERTS_PER_GROUP = 8
N_EXPERTS = N_GROUPS * EXPERTS_PER_GROUP
D_EXPERT = 512
ROPE_THETA = 10000.0
NORM_EPS = 1e-6
SUBLN_EPS = 1e-5
LAMBDA_INIT = 0.8 - 0.6 * math.exp(-0.3 * 0)
LOG2E = math.log2(math.e)
ROPE_SPLIT = 256
IN_COLS = 3 * ATT_WIDTH + 3 * HGRN_FDIM + 2 * HGRN_WIDTH
HGRN_COLS = IN_COLS - 3 * ATT_WIDTH

LANES = 128
VMEM_LIMIT = 56 * 1024 * 1024
HGRN_CHUNK = 64
IN_TM = 512
ATT_TK = 1024
ATT_CB = 256
ATT_KT = 256
MOE_BT = 2048
MOE_ROWS = 256
MOE_UNROLL = 8
ATT_VPAD = ATT_VDIM + 16
ROUTER_LANES = LANES

BF16 = jnp.bfloat16
F32 = jnp.float32


def _cparams(sem):
    return pltpu.CompilerParams(dimension_semantics=sem, vmem_limit_bytes=VMEM_LIMIT)


def _rot_half_64(x):
    lane = lax.broadcasted_iota(jnp.int32, x.shape, 1)
    fwd = pltpu.roll(x, 32, axis=1)
    bwd = pltpu.roll(x, 96, axis=1)
    return jnp.where((lane & 63) < 32, bwd, fwd)


def _inproj_kernel(x_ref, g_ref, w_ref, wqvt_ref, cos_ref, sin_ref, cost_ref, sint_ref,
                   qt_ref, k_ref, vt_ref, hg_ref):
    x = x_ref[...]
    tm = x.shape[0]
    ms = jnp.mean(x * x, axis=-1, keepdims=True)
    h = (x * lax.rsqrt(ms + NORM_EPS) * g_ref[...]).astype(BF16)
    nt = (((1,), (1,)), ((), ()))
    half = ATT_QKDIM // 2
    ct = cost_ref[...] * (ATT_QKDIM ** -0.5 * LOG2E)
    st = sint_ref[...] * (ATT_QKDIM ** -0.5 * LOG2E)
    for j in range(ATT_WIDTH // LANES):
        a = lax.dot_general(wqvt_ref[j * LANES:(j + 1) * LANES, :], h, nt,
                            preferred_element_type=F32)
        for c in range(LANES // ATT_QKDIM):
            x1 = a[c * ATT_QKDIM:c * ATT_QKDIM + half]
            x2 = a[c * ATT_QKDIM + half:(c + 1) * ATT_QKDIM]
            lo = j * LANES + c * ATT_QKDIM
            qt_ref[lo:lo + half, :] = (x1 * ct - x2 * st).astype(BF16)
            qt_ref[lo + half:lo + ATT_QKDIM, :] = (x1 * st + x2 * ct).astype(BF16)
    ones = jnp.ones((ATT_VPAD - ATT_VDIM, tm), BF16)
    for j in range(ATT_HEADS):
        lo = ATT_WIDTH + j * ATT_VDIM
        vt = lax.dot_general(wqvt_ref[lo:lo + ATT_VDIM, :], h, nt, preferred_element_type=F32)
        vt_ref[0, j * ATT_VPAD:j * ATT_VPAD + ATT_VDIM, :] = vt.astype(BF16)
        vt_ref[0, j * ATT_VPAD + ATT_VDIM:(j + 1) * ATT_VPAD, :] = ones
    cosv = cos_ref[...]
    sinv = sin_ref[...]
    for j in range(ATT_WIDTH // LANES):
        lo = ATT_WIDTH + j * LANES
        a = jnp.dot(h, w_ref[:, lo:lo + LANES], preferred_element_type=F32)
        k_ref[:, j * LANES:(j + 1) * LANES] = (a * cosv + _rot_half_64(a) * sinv).astype(BF16)
    for j in range(HGRN_COLS // 512):
        lo = 3 * ATT_WIDTH + j * 512
        hg_ref[:, j * 512:(j + 1) * 512] = jnp.dot(h, w_ref[:, lo:lo + 512],
                                                   preferred_element_type=F32)


def _in_proj(x2, g_mix, w_in_bf, w_qv_t, cos_t, sin_t, cos_tt, sin_tt, tk):
    T = x2.shape[0]
    tm = IN_TM
    per = tk // tm
    return pl.pallas_call(
        _inproj_kernel,
        grid=(T // tm,),
        in_specs=[
            pl.BlockSpec((tm, D_MODEL), lambda i: (i, 0)),
            pl.BlockSpec((1, D_MODEL), lambda i: (0, 0)),
            pl.BlockSpec((D_MODEL, IN_COLS), lambda i: (0, 0)),
            pl.BlockSpec((2 * ATT_WIDTH, D_MODEL), lambda i: (0, 0)),
            pl.BlockSpec((tm, LANES), lambda i: (i, 0)),
            pl.BlockSpec((tm, LANES), lambda i: (i, 0)),
            pl.BlockSpec((ATT_QKDIM // 2, tm), lambda i: (0, i)),
            pl.BlockSpec((ATT_QKDIM // 2, tm), lambda i: (0, i)),
        ],
        out_specs=[
            pl.BlockSpec((ATT_WIDTH, tm), lambda i: (0, i)),
            pl.BlockSpec((tm, ATT_WIDTH), lambda i: (i, 0)),
            pl.BlockSpec((1, ATT_HEADS * ATT_VPAD, tm), lambda i: (i // per, 0, i % per)),
            pl.BlockSpec((tm, HGRN_COLS), lambda i: (i, 0)),
        ],
        out_shape=[
            jax.ShapeDtypeStruct((ATT_WIDTH, T), BF16),
            jax.ShapeDtypeStruct((T, ATT_WIDTH), BF16),
            jax.ShapeDtypeStruct((T // tk, ATT_HEADS * ATT_VPAD, tk), BF16),
            jax.ShapeDtypeStruct((T, HGRN_COLS), F32),
        ],
        compiler_params=_cparams(("parallel",)),
        name="in_proj",
    )(x2, g_mix, w_in_bf, w_qv_t, cos_t, sin_t, cos_tt, sin_tt)


def _attn_kernel(lam_ref, qt_ref, qtn_ref, k_ref, vt_ref, g_ref, o_ref,
                 qw_ref, qwn_ref, sa_ref, sb_ref, acc_ref, st_ref):
    tq = qt_ref.shape[1]
    nchunk, _, tk = vt_ref.shape
    ncb = 2 * tq // ATT_CB

    def latch(dst_ref, src_ref):
        qt = src_ref[...].astype(F32)
        row = lax.broadcasted_iota(jnp.int32, qt.shape, 0)
        dst_ref[:, 0:tq] = jnp.where(row < ATT_QKDIM, qt, 0.0).astype(BF16)
        dst_ref[:, tq:] = jnp.where(row >= ATT_QKDIM, qt, 0.0).astype(BF16)

    latch(qw_ref, qt_ref)
    latch(qwn_ref, qtn_ref)
    acc_ref[...] = jnp.zeros_like(acc_ref)

    def stage(j_acc, s_acc, m_cur, alpha, j_sc, s_sc, m_old, q_ref):
        off = j_sc * tk
        m_new, al_new = [], []
        for cb in range(ncb):
            cols = slice(cb * ATT_CB, (cb + 1) * ATT_CB)
            part, cmax = None, None
            for t in range(tk // ATT_KT):
                rows = slice(t * ATT_KT, (t + 1) * ATT_KT)
                kc = k_ref[pl.ds(pl.multiple_of(off + t * ATT_KT, ATT_KT), ATT_KT), :]
                s = jnp.dot(kc, q_ref[:, cols], preferred_element_type=F32)
                s_sc[rows, cols] = s
                c = jnp.max(s, axis=0, keepdims=True)
                cmax = c if cmax is None else jnp.maximum(cmax, c)
                if s_acc is not None:
                    p = jnp.exp2(s_acc[rows, cols] - m_cur[cb]).astype(BF16)
                    d = jnp.dot(vt_ref[j_acc, :, rows], p, preferred_element_type=F32)
                    part = d if part is None else part + d
            if s_acc is not None:
                acc_ref[:, cols] = alpha[cb] * acc_ref[:, cols] + part
            mn = jnp.maximum(m_old[cb], cmax)
            m_new.append(mn)
            al_new.append(jnp.exp2(m_old[cb] - mn))
        return tuple(m_new), tuple(al_new)

    def save(st):
        for cb in range(ncb):
            cols = slice(cb * ATT_CB, (cb + 1) * ATT_CB)
            st_ref[0:1, cols] = st[0][cb]
            st_ref[1:2, cols] = st[1][cb]

    m0 = tuple(jnp.full((1, ATT_CB), -jnp.inf, F32) for _ in range(ncb))

    @pl.when(pl.program_id(1) == 0)
    def _():
        save(stage(0, None, None, None, 0, sa_ref, m0, qw_ref))

    state = (tuple(st_ref[0:1, cb * ATT_CB:(cb + 1) * ATT_CB] for cb in range(ncb)),
             tuple(st_ref[1:2, cb * ATT_CB:(cb + 1) * ATT_CB] for cb in range(ncb)))

    per_trip = 4 if nchunk % 4 == 0 else 2
    ntrip = nchunk // per_trip
    bufs = (sa_ref, sb_ref)

    def trip(tr, st, last):
        m_cur, al_cur = st
        for u in range(per_trip):
            j = per_trip * tr + u
            if last and u == per_trip - 1:
                m_cur, al_cur = stage(j, bufs[u % 2], m_cur, al_cur, 0, bufs[(u + 1) % 2], m0,
                                      qwn_ref)
            else:
                m_cur, al_cur = stage(j, bufs[u % 2], m_cur, al_cur, j + 1, bufs[(u + 1) % 2],
                                      m_cur, qw_ref)
        return m_cur, al_cur

    state = lax.fori_loop(0, ntrip - 1, lambda tr, st: trip(tr, st, False), state)
    save(trip(ntrip - 1, state, True))

    lam = lam_ref[0, 0]
    acc = acc_ref[...]
    o1 = acc[0:ATT_VDIM, 0:tq] / acc[ATT_VDIM:ATT_VDIM + 1, 0:tq]
    o2 = acc[0:ATT_VDIM, tq:] / acc[ATT_VDIM:ATT_VDIM + 1, tq:]
    o = o1 - lam * o2
    ms = jnp.mean(o * o, axis=0, keepdims=True)
    o = o * lax.rsqrt(ms + SUBLN_EPS) * g_ref[...] * (1.0 - LAMBDA_INIT)
    o_ref[...] = o.T.astype(o_ref.dtype)


def _diff_attn(lam, qt, k, vt, subln_g, tq=1024):
    T = k.shape[0]
    tq = min(tq, T)
    nq = T // tq
    nchunk, _, tk = vt.shape
    return pl.pallas_call(
        _attn_kernel,
        grid=(ATT_HEADS, nq),
        in_specs=[
            pl.BlockSpec(memory_space=pltpu.SMEM),
            pl.BlockSpec((LANES, tq), lambda h, i: (h, i)),
            pl.BlockSpec((LANES, tq), lambda h, i: (h, jnp.minimum(i + 1, nq - 1))),
            pl.BlockSpec((T, LANES), lambda h, i: (0, h)),
            pl.BlockSpec((nchunk, ATT_VPAD, tk), lambda h, i: (0, h, 0)),
            pl.BlockSpec((ATT_VDIM, 1), lambda h, i: (0, 0)),
        ],
        out_specs=pl.BlockSpec((tq, LANES), lambda h, i: (i, h)),
        out_shape=jax.ShapeDtypeStruct((T, ATT_WIDTH), BF16),
        scratch_shapes=[
            pltpu.VMEM((LANES, 2 * tq), BF16),
            pltpu.VMEM((LANES, 2 * tq), BF16),
            pltpu.VMEM((tk, 2 * tq), F32),
            pltpu.VMEM((tk, 2 * tq), F32),
            pltpu.VMEM((ATT_VPAD, 2 * tq), F32),
            pltpu.VMEM((8, 2 * tq), F32),
        ],
        compiler_params=_cparams(("arbitrary", "arbitrary")),
        name="diff_attn",
    )(lam, qt, qt, k, vt, subln_g)


def _hgrn_consts(C, backward):
    halves = []
    h = C // 2
    while h >= 1:
        halves.append(h)
        h //= 2
    E = np.zeros((2 + len(halves), C, C), np.float32)
    M = np.zeros((len(halves) + 1, C, C), np.float32)
    idx = np.arange(C)
    for t in range(C):
        if not backward:
            E[0, t, idx <= t] = 1.0
            E[1, t, idx > t] = 1.0
        else:
            E[0, t, idx >= t] = 1.0
            E[1, t, idx < t] = 1.0
    for li, h in enumerate(halves):
        for t in range(C):
            mid = (t // (2 * h)) * 2 * h + h
            upper = t >= mid
            if not backward:
                if upper:
                    E[2 + li, t, (idx >= mid) & (idx <= t)] = 1.0
                else:
                    E[2 + li, t, (idx > t) & (idx <= mid - 1)] = 1.0
            else:
                if not upper:
                    E[2 + li, t, (idx >= t) & (idx <= mid - 1)] = 1.0
                else:
                    E[2 + li, t, (idx >= mid) & (idx <= t - 1)] = 1.0
            for s in range(C):
                same = (s // (2 * h)) == (t // (2 * h))
                s_upper = s >= mid
                if same and ((not backward and upper and not s_upper)
                             or (backward and not upper and s_upper)):
                    M[li, t, s] = 1.0
    M[-1] = np.eye(C, dtype=np.float32)
    E = np.concatenate([E.reshape(-1, C), np.ones((8, C), np.float32)], axis=0)
    return E, M


def _split3_rows(g):
    hi = g.astype(BF16)
    r1 = g - hi.astype(F32)
    mid = r1.astype(BF16)
    lo = (r1 - mid.astype(F32)).astype(BF16)
    return jnp.concatenate([hi, mid, lo], axis=0)


def _hgrn_kernel(*refs, backward, nchunk, final):
    if final:
        (hq_ref, hf_ref, hi_ref, lb_ref, e_ref, m_ref, ob_ref, hgate_ref, gn_ref,
         o_ref, st_ref) = refs
    else:
        hq_ref, hf_ref, hi_ref, lb_ref, e_ref, m_ref, o_ref, st_ref = refs
    C = HGRN_CHUNK
    nlev = m_ref.shape[0] - 1
    nt = (((1,), (1,)), ((), ()))

    @pl.when(pl.program_id(0) == 0)
    def _():
        st_ref[...] = jnp.zeros_like(st_ref)

    lbd = lb_ref[...]
    emat = e_ref[...]
    order = range(nchunk - 1, -1, -1) if backward else range(nchunk)
    for c in order:
        rows = pl.ds(c * C, C)
        z = hf_ref[rows, :]
        hq = hq_ref[rows, :]
        f = lbd + (1.0 - lbd) * jax.nn.sigmoid(z)
        kk = (1.0 - lbd) * jax.nn.sigmoid(-z)
        g = jnp.log(f)
        q = hq * jax.nn.sigmoid(hq)
        vb = hi_ref[rows, :].astype(BF16)
        ex = jnp.dot(emat, _split3_rows(g), preferred_element_type=F32)
        ee = jnp.exp(ex)
        qi = (q * ee[0:C]).astype(BF16)
        ki = (kk * ee[C:2 * C]).astype(BF16)
        dec = ee[(2 + nlev) * C:(2 + nlev) * C + 1]
        qb = q.astype(BF16)
        kb = kk.astype(BF16)
        qlev = [(q * ee[(2 + l) * C:(3 + l) * C]).astype(BF16) for l in range(nlev)]
        klev = [(kk * ee[(2 + l) * C:(3 + l) * C]).astype(BF16) for l in range(nlev)]
        outs = []
        for hd in range(HGRN_HEADS):
            ln = slice(hd * LANES, (hd + 1) * LANES)
            a = m_ref[nlev] * lax.dot_general(qb[:, ln], kb[:, ln], nt, preferred_element_type=F32)
            for l in range(nlev):
                a = a + m_ref[l] * lax.dot_general(qlev[l][:, ln], klev[l][:, ln], nt,
                                                   preferred_element_type=F32)
            st = st_ref[hd]
            o = jnp.dot(a.astype(BF16), vb[:, ln], preferred_element_type=F32)
            o = o + lax.dot_general(qi[:, ln], st.astype(BF16), nt, preferred_element_type=F32)
            vt = hi_ref[rows, ln].T.astype(BF16)
            st_ref[hd] = st * dec[:, ln] + jnp.dot(vt, ki[:, ln], preferred_element_type=F32)
            outs.append(o)
        o_all = jnp.concatenate(outs, axis=1)
        if final:
            o_all = o_all + ob_ref[rows, :]
            res = []
            for hd in range(HGRN_HEADS):
                ln = slice(hd * LANES, (hd + 1) * LANES)
                oh = o_all[:, ln]
                ms = jnp.mean(oh * oh, axis=-1, keepdims=True)
                res.append(oh * lax.rsqrt(ms + NORM_EPS) * gn_ref[...])
            hg = hgate_ref[rows, :]
            o_all = jnp.concatenate(res, axis=1) * (hg * jax.nn.sigmoid(hg))
        o_ref[rows, :] = o_all.astype(o_ref.dtype)


def _hgrn_pass(hgrn_in, lb_row, backward, o_b=None, gnorm_g=None, tb=512):
    T = hgrn_in.shape[0]
    nblk = T // tb
    final = o_b is not None
    E, M = _hgrn_consts(HGRN_CHUNK, backward)
    e_bf = jnp.asarray(np.concatenate([E, E, E], axis=1), BF16)
    m_f = jnp.asarray(M, F32)
    blk = (lambda i: nblk - 1 - i) if backward else (lambda i: i)
    f_part = 2 if backward else 1
    in_specs = [
        pl.BlockSpec((tb, 512), lambda i: (blk(i), 0)),
        pl.BlockSpec((tb, 512), lambda i: (blk(i), f_part)),
        pl.BlockSpec((tb, 512), lambda i: (blk(i), 3)),
        pl.BlockSpec((1, 512), lambda i: (0, 0)),
        pl.BlockSpec(e_bf.shape, lambda i: (0, 0)),
        pl.BlockSpec(m_f.shape, lambda i: (0, 0, 0)),
    ]
    args = [hgrn_in, hgrn_in, hgrn_in, lb_row, e_bf, m_f]
    if final:
        in_specs += [
            pl.BlockSpec((tb, 512), lambda i: (blk(i), 0)),
            pl.BlockSpec((tb, 512), lambda i: (blk(i), 4)),
            pl.BlockSpec((1, HGRN_VDIM), lambda i: (0, 0)),
        ]
        args += [o_b, hgrn_in, gnorm_g]
    return pl.pallas_call(
        functools.partial(_hgrn_kernel, backward=backward, nchunk=tb // HGRN_CHUNK, final=final),
        grid=(nblk,),
        in_specs=in_specs,
        out_specs=pl.BlockSpec((tb, 512), lambda i: (blk(i), 0)),
        out_shape=jax.ShapeDtypeStruct((T, HGRN_WIDTH), BF16 if final else F32),
        scratch_shapes=[pltpu.VMEM((HGRN_HEADS, HGRN_VDIM, HGRN_EXPAND), F32)],
        compiler_params=_cparams(("arbitrary",)),
        name="hgrn_bwd" if backward else "hgrn_fwd",
    )(*args)


def _outproj_kernel(att_ref, rec_ref, x_ref, wo_ref, g_ref, wr_hi_ref, wr_lo_ref, rb_ref,
                    hres_ref, xt_ref, comb_ref):
    mix = (jnp.dot(att_ref[...], wo_ref[0:ATT_WIDTH, :], preferred_element_type=F32)
           + jnp.dot(rec_ref[...], wo_ref[ATT_WIDTH:, :], preferred_element_type=F32))
    hres = x_ref[...] + mix
    hres_ref[...] = hres
    ms = jnp.mean(hres * hres, axis=-1, keepdims=True)
    xt = hres * lax.rsqrt(ms + NORM_EPS) * g_ref[...]
    xt_ref[...] = xt
    xt_hi = xt.astype(BF16)
    xt_lo = (xt - xt_hi.astype(F32)).astype(BF16)
    logits = (jnp.dot(xt_hi, wr_hi_ref[...], preferred_element_type=F32)
              + jnp.dot(xt_hi, wr_lo_ref[...], preferred_element_type=F32)
              + jnp.dot(xt_lo, wr_hi_ref[...], preferred_element_type=F32)) + rb_ref[...]
    lane = lax.broadcasted_iota(jnp.int32, logits.shape, 1)
    neg = -jnp.inf
    big = jnp.int32(1 << 20)
    gmask = (lane >= N_EXPERTS) & (lane < N_EXPERTS + N_GROUPS)
    glog = jnp.where(gmask, logits, neg)
    gmax = jnp.max(glog, axis=-1, keepdims=True)
    gsum = jnp.sum(jnp.exp(glog - gmax), axis=-1, keepdims=True)
    p_g = 1.0 / gsum
    g_idx = jnp.min(jnp.where(glog == gmax, lane, big), axis=-1, keepdims=True) - N_EXPERTS
    emask = (lane < N_EXPERTS) & ((lane >> 3) == g_idx)
    elog = jnp.where(emask, logits, neg)
    e1 = jnp.max(elog, axis=-1, keepdims=True)
    i1 = jnp.min(jnp.where(elog == e1, lane, big), axis=-1, keepdims=True)
    elog2 = jnp.where(lane == i1, neg, elog)
    e2 = jnp.max(elog2, axis=-1, keepdims=True)
    i2 = jnp.min(jnp.where(elog2 == e2, lane, big), axis=-1, keepdims=True)
    r = jnp.exp(e2 - e1)
    w1 = p_g / (1.0 + r)
    w2 = p_g * r / (1.0 + r)
    comb_ref[...] = jnp.where(lane == 0, i1.astype(F32), jnp.where(
        lane == 1, i2.astype(F32), jnp.where(lane == 2, w1, jnp.where(lane == 3, w2, 0.0))))


def _out_proj(att, rec, x2, w_out_bf, g_ffn, wr_hi, wr_lo, rbias, tm=512):
    T = x2.shape[0]
    row = lambda i: (i, 0)
    fixed = lambda i: (0, 0)
    return pl.pallas_call(
        _outproj_kernel,
        grid=(T // tm,),
        in_specs=[
            pl.BlockSpec((tm, ATT_WIDTH), row),
            pl.BlockSpec((tm, HGRN_WIDTH), row),
            pl.BlockSpec((tm, D_MODEL), row),
            pl.BlockSpec((D_MODEL, D_MODEL), fixed),
            pl.BlockSpec((1, D_MODEL), fixed),
            pl.BlockSpec((D_MODEL, ROUTER_LANES), fixed),
            pl.BlockSpec((D_MODEL, ROUTER_LANES), fixed),
            pl.BlockSpec((1, ROUTER_LANES), fixed),
        ],
        out_specs=[
            pl.BlockSpec((tm, D_MODEL), row),
            pl.BlockSpec((tm, D_MODEL), row),
            pl.BlockSpec((tm, ROUTER_LANES), row),
        ],
        out_shape=[
            jax.ShapeDtypeStruct((T, D_MODEL), F32),
            jax.ShapeDtypeStruct((T, D_MODEL), F32),
            jax.ShapeDtypeStruct((T, ROUTER_LANES), F32),
        ],
        compiler_params=_cparams(("parallel",)),
        name="out_proj_router",
    )(att, rec, x2, w_out_bf, g_ffn, wr_hi, wr_lo, rbias)


def _split3_f32(w):
    hi = w.astype(BF16).astype(F32)
    r1 = w - hi
    mid = r1.astype(BF16).astype(F32)
    lo = (r1 - mid).astype(BF16).astype(F32)
    return hi, mid, lo


def _route_sort_kernel(rt_ref, ltri_ref, utri_ref, lists_ref, meta_ref):
    rt = rt_ref[...]
    bt = rt.shape[0]
    lane = lax.broadcasted_iota(jnp.int32, rt.shape, 1)
    lane_f = lane.astype(F32)
    oh1 = lane_f == rt[:, 0:1]
    oh2 = lane_f == rt[:, 1:2]
    oh = jnp.where(oh1, 1.0, jnp.where(oh2, 1.0, 0.0))
    oh_b = oh.astype(BF16)
    rank = jnp.dot(ltri_ref[...], oh_b, preferred_element_type=F32)
    below = jnp.dot(oh_b, utri_ref[...], preferred_element_type=F32)
    offs = jnp.sum(below, axis=0, keepdims=True)
    cnt = jnp.sum(oh, axis=0, keepdims=True)
    posmat = rank + offs
    pos1 = jnp.sum(jnp.where(oh1, posmat, 0.0), axis=1, keepdims=True)
    pos2 = jnp.sum(jnp.where(oh2, posmat, 0.0), axis=1, keepdims=True)
    tok = lax.broadcasted_iota(jnp.int32, rt.shape, 0)
    tok_hi = (tok >> 5).astype(F32)
    tok_lo = (tok & 31).astype(F32)

    def record(w, slot):
        hi, mid, lo = _split3_f32(w)
        d = jnp.where(lane == 0, tok_hi, jnp.where(lane == 1, tok_lo, jnp.where(
            lane == 2, hi, jnp.where(lane == 3, mid, jnp.where(lane == 4, lo, jnp.where(
                lane == 5, slot, 0.0))))))
        return d.T.astype(BF16)

    d1 = record(rt[:, 2:3], 0.0)
    d2 = record(rt[:, 3:4], 1.0)
    cw = 512
    for c in range(2 * bt // cw):
        colp = (lax.broadcasted_iota(jnp.int32, (bt, cw), 1) + c * cw).astype(F32)
        o1 = jnp.where(colp == pos1, 1.0, 0.0).astype(BF16)
        o2 = jnp.where(colp == pos2, 1.0, 0.0).astype(BF16)
        srt = (jnp.dot(d1, o1, preferred_element_type=F32)
               + jnp.dot(d2, o2, preferred_element_type=F32))
        lists_ref[0, :, c * cw:(c + 1) * cw] = srt[0:8]
    row = lax.broadcasted_iota(jnp.int32, (8, LANES), 0)
    meta_ref[0] = jnp.where(row == 0, cnt, jnp.where(row == 1, offs, 0.0))


def _route_sort(route, bt):
    T = route.shape[0]
    nb = T // bt
    ltri = jnp.asarray(np.tril(np.ones((bt, bt), np.float32), -1), BF16)
    utri = jnp.asarray(np.triu(np.ones((LANES, LANES), np.float32), 1), BF16)
    return pl.pallas_call(
        _route_sort_kernel,
        grid=(nb,),
        in_specs=[
            pl.BlockSpec((bt, ROUTER_LANES), lambda b: (b, 0)),
            pl.BlockSpec((bt, bt), lambda b: (0, 0)),
            pl.BlockSpec((LANES, LANES), lambda b: (0, 0)),
        ],
        out_specs=[
            pl.BlockSpec((1, 8, 2 * bt), lambda b: (b, 0, 0)),
            pl.BlockSpec((1, 8, LANES), lambda b: (b, 0, 0)),
        ],
        out_shape=[
            jax.ShapeDtypeStruct((nb, 8, 2 * bt), F32),
            jax.ShapeDtypeStruct((nb, 8, LANES), F32),
        ],
        compiler_params=_cparams(("parallel",)),
        name="route_sort",
    )(route, ltri, utri)


def _moe_kernel(cnt_ref, off_ref, idx_ref, ws_ref, xt_ref, hres_ref, wg_ref, wu_ref, wd_ref, gf_ref,
                o_ref, yb_ref, xs_ref, y_ref):
    b = pl.program_id(0)
    e = pl.program_id(1)
    bt = xt_ref.shape[0]
    rows = xs_ref.shape[0]

    @pl.when(e == 0)
    def _():
        xs_ref[...] = jnp.zeros_like(xs_ref)

    n = cnt_ref[b * N_EXPERTS + e]
    base = b * 2 * bt + off_ref[b * N_EXPERTS + e]

    def tile(r, carry):
        r0 = base + r * rows
        m = jnp.minimum(n - r * rows, rows)
        trips = (m + MOE_UNROLL - 1) // MOE_UNROLL

        def gather(i, c):
            src = []
            for k in range(MOE_UNROLL):
                t = idx_ref[r0 + i * MOE_UNROLL + k] & (bt - 1)
                src.append(xt_ref[pl.ds(t, 1), :])
            xs_ref[pl.ds(pl.multiple_of(i * MOE_UNROLL, MOE_UNROLL), MOE_UNROLL), :] = (
                jnp.concatenate(src, axis=0))
            return c

        lax.fori_loop(0, trips, gather, 0)
        xs = xs_ref[...].astype(BF16)
        a = jnp.dot(xs, wg_ref[0], preferred_element_type=F32)
        u = jnp.dot(xs, wu_ref[0], preferred_element_type=F32)
        hid = (a * jax.nn.sigmoid(a)) * u
        y_ref[...] = jnp.dot(hid.astype(BF16), wd_ref[0], preferred_element_type=F32)

        def scatter(i, c):
            for k in range(MOE_UNROLL):
                j = i * MOE_UNROLL + k
                dst = jnp.where(j < m, idx_ref[r0 + j], 2 * bt + k)
                yb_ref[pl.ds(dst, 1), :] = ws_ref[r0 + j] * y_ref[pl.ds(j, 1), :]
            return c

        lax.fori_loop(0, trips, scatter, 0)
        return carry

    lax.fori_loop(0, (n + rows - 1) // rows, tile, 0)

    @pl.when(e == pl.num_programs(1) - 1)
    def _():
        y = hres_ref[...] + yb_ref[0:bt, :] + yb_ref[bt:2 * bt, :]
        ms = jnp.mean(y * y, axis=-1, keepdims=True)
        o_ref[...] = y * lax.rsqrt(ms + NORM_EPS) * gf_ref[...]


def _moe(cnt, off, idx, ws, xt, hres, wg, wu, wd, g_final, bt):
    T = xt.shape[0]
    row = lambda i, e, *_: (i, 0)
    grid_spec = pltpu.PrefetchScalarGridSpec(
        num_scalar_prefetch=4,
        grid=(T // bt, N_EXPERTS),
        in_specs=[
            pl.BlockSpec((bt, D_MODEL), row, pipeline_mode=pl.Buffered(1)),
            pl.BlockSpec((bt, D_MODEL), row, pipeline_mode=pl.Buffered(1)),
            pl.BlockSpec((1, D_MODEL, D_EXPERT), lambda i, e, *_: (e, 0, 0)),
            pl.BlockSpec((1, D_MODEL, D_EXPERT), lambda i, e, *_: (e, 0, 0)),
            pl.BlockSpec((1, D_EXPERT, D_MODEL), lambda i, e, *_: (e, 0, 0)),
            pl.BlockSpec((1, D_MODEL), lambda i, e, *_: (0, 0)),
        ],
        out_specs=pl.BlockSpec((bt, D_MODEL), row, pipeline_mode=pl.Buffered(1)),
        scratch_shapes=[
            pltpu.VMEM((2 * bt + MOE_UNROLL, D_MODEL), F32),
            pltpu.VMEM((MOE_ROWS, D_MODEL), F32),
            pltpu.VMEM((MOE_ROWS, D_MODEL), F32),
        ],
    )
    return pl.pallas_call(
        _moe_kernel,
        grid_spec=grid_spec,
        out_shape=jax.ShapeDtypeStruct((T, D_MODEL), F32),
        compiler_params=_cparams(("parallel", "arbitrary")),
        name="moe",
    )(cnt, off, idx, ws, xt, hres, wg, wu, wd, g_final)


def _rope_tables(T):
    inv = ROPE_THETA ** (-np.arange(0, ATT_QKDIM, 2, dtype=np.float64) / ATT_QKDIM)
    na = -(-T // ROPE_SPLIT)
    ang_a = (np.arange(na, dtype=np.float64) * ROPE_SPLIT)[:, None] * inv[None, :]
    ang_b = np.arange(ROPE_SPLIT, dtype=np.float64)[:, None] * inv[None, :]
    ca, sa = jnp.asarray(np.cos(ang_a), F32)[:, None, :], jnp.asarray(np.sin(ang_a), F32)[:, None, :]
    cb, sb = jnp.asarray(np.cos(ang_b), F32)[None, :, :], jnp.asarray(np.sin(ang_b), F32)[None, :, :]
    c = (ca * cb - sa * sb).reshape(na * ROPE_SPLIT, -1)[:T]
    s = (sa * cb + ca * sb).reshape(na * ROPE_SPLIT, -1)[:T]
    return (jnp.concatenate([c, c, c, c], axis=1), jnp.concatenate([-s, s, -s, s], axis=1),
            c.T, s.T)


def kernel(x, w_in, w_out, g_mix, lam_params, subln_g, hgrn_gnorm_g, hgrn_lb, g_ffn, w_gr, b_gr,
           w_er, b_er, w_gate, w_up, w_down, g_final):
    B, T, D = x.shape
    x2 = x.reshape(B * T, D)
    l = 0
    w_in_bf = w_in[l].astype(BF16)
    w_out_bf = w_out[l].astype(BF16)
    lp = lam_params[l].astype(F32)
    lam = (jnp.exp(jnp.sum(lp[0] * lp[1])) - jnp.exp(jnp.sum(lp[2] * lp[3])) + LAMBDA_INIT).reshape(1, 1)
    lb = jnp.cumsum(jax.nn.softmax(hgrn_lb.astype(F32), axis=1), axis=1)[:, l]
    w_qv_t = jnp.concatenate([w_in[l][:, 0:ATT_WIDTH], w_in[l][:, 2 * ATT_WIDTH:3 * ATT_WIDTH]],
                             axis=1).T.astype(BF16)
    cos_t, sin_t, cos_tt, sin_tt = _rope_tables(T)
    w_r = jnp.concatenate([jnp.transpose(w_er[l], (1, 0, 2)).reshape(D, N_EXPERTS), w_gr[l],
                           jnp.zeros((D, ROUTER_LANES - N_EXPERTS - N_GROUPS), F32)], axis=1)
    wr_hi = w_r.astype(BF16)
    wr_lo = (w_r - wr_hi.astype(F32)).astype(BF16)
    rbias = jnp.concatenate([b_er[l].reshape(-1), b_gr[l],
                             jnp.zeros((ROUTER_LANES - N_EXPERTS - N_GROUPS,), F32)]).reshape(1, -1)
    wg = w_gate[l].reshape(N_EXPERTS, D, D_EXPERT).astype(BF16)
    wu = w_up[l].reshape(N_EXPERTS, D, D_EXPERT).astype(BF16)
    wd = w_down[l].reshape(N_EXPERTS, D_EXPERT, D).astype(BF16)

    tk = min(ATT_TK, (B * T) // 2)
    qt, k, vt, hgrn_in = _in_proj(x2, g_mix[l].reshape(1, D), w_in_bf, w_qv_t, cos_t, sin_t,
                                  cos_tt, sin_tt, tk)
    att = _diff_attn(lam, qt, k, vt, subln_g[l].reshape(-1, 1))
    o_b = _hgrn_pass(hgrn_in, lb[1:2], backward=True)
    rec = _hgrn_pass(hgrn_in, lb[0:1], backward=False, o_b=o_b,
                     gnorm_g=hgrn_gnorm_g[l].reshape(1, -1))
    hres, xt, route = _out_proj(att, rec, x2, w_out_bf, g_ffn[l].reshape(1, D), wr_hi, wr_lo, rbias)
    bt = min(MOE_BT, B * T)
    lists, meta = _route_sort(route, bt)
    pad = jnp.zeros((MOE_UNROLL,), F32)
    idx = jnp.concatenate([(lists[:, 5] * bt + lists[:, 0] * 32.0 + lists[:, 1]).reshape(-1),
                           pad]).astype(jnp.int32)
    ws = jnp.concatenate([(lists[:, 2] + lists[:, 3] + lists[:, 4]).reshape(-1), pad])
    cnt = meta[:, 0, :N_EXPERTS].astype(jnp.int32).reshape(-1)
    off = meta[:, 1, :N_EXPERTS].astype(jnp.int32).reshape(-1)
    out = _moe(cnt, off, idx, ws, xt, hres, wg, wu, wd, g_final.reshape(1, D), bt)
    return out.reshape(B, T, D)
```

```python
import functools
import math

import numpy as np
import jax
import jax.numpy as jnp
from jax import lax
from jax.experimental import pallas as pl
from jax.experimental.pallas import tpu as pltpu

D_MODEL = 1024
ATT_WIDTH = 512
ATT_HEADS = 4
ATT_VDIM = 128
ATT_QKDIM = 64
HGRN_WIDTH = 512
HGRN_HEADS = 4
HGRN_VDIM = 128
HGRN_EXPAND = 128
HGRN_FDIM = 512
N_GROUPS = 4
EXPERTS_PER_GROUP = 8
N_EXPERTS = N_GROUPS * EXPERTS_PER_GROUP
D_EXPERT = 512
ROPE_THETA = 10000.0
NORM_EPS = 1e-6
SUBLN_EPS = 1e-5
LAMBDA_INIT = 0.8 - 0.6 * math.exp(-0.3 * 0)
LOG2E = math.log2(math.e)
ROPE_SPLIT = 256
IN_COLS = 3 * ATT_WIDTH + 3 * HGRN_FDIM + 2 * HGRN_WIDTH
HGRN_COLS = IN_COLS - 3 * ATT_WIDTH

LANES = 128
VMEM_LIMIT = 56 * 1024 * 1024
HGRN_CHUNK = 64
IN_TM = 512
ATT_TK = 1024
ATT_CB = 256
ATT_KT = 256
MOE_BT = 2048
MOE_ROWS = 256
MOE_UNROLL = 8
ATT_VPAD = ATT_VDIM + 16
ROUTER_LANES = LANES

BF16 = jnp.bfloat16
F32 = jnp.float32


def _cparams(sem):
    return pltpu.CompilerParams(dimension_semantics=sem, vmem_limit_bytes=VMEM_LIMIT)


def _rot_half_64(x):
    lane = lax.broadcasted_iota(jnp.int32, x.shape, 1)
    fwd = pltpu.roll(x, 32, axis=1)
    bwd = pltpu.roll(x, 96, axis=1)
    return jnp.where((lane & 63) < 32, bwd, fwd)


def _inproj_kernel(x_ref, g_ref, w_ref, wqvt_ref, cos_ref, sin_ref, cost_ref, sint_ref,
                   qt_ref, k_ref, vt_ref, hg_ref):
    x = x_ref[...]
    tm = x.shape[0]
    ms = jnp.mean(x * x, axis=-1, keepdims=True)
    h = (x * lax.rsqrt(ms + NORM_EPS) * g_ref[...]).astype(BF16)
    nt = (((1,), (1,)), ((), ()))
    half = ATT_QKDIM // 2
    ct = cost_ref[...] * (ATT_QKDIM ** -0.5 * LOG2E)
    st = sint_ref[...] * (ATT_QKDIM ** -0.5 * LOG2E)
    for j in range(ATT_WIDTH // LANES):
        a = lax.dot_general(wqvt_ref[j * LANES:(j + 1) * LANES, :], h, nt,
                            preferred_element_type=F32)
        for c in range(LANES // ATT_QKDIM):
            x1 = a[c * ATT_QKDIM:c * ATT_QKDIM + half]
            x2 = a[c * ATT_QKDIM + half:(c + 1) * ATT_QKDIM]
            lo = j * LANES + c * ATT_QKDIM
            qt_ref[lo:lo + half, :] = (x1 * ct - x2 * st).astype(BF16)
            qt_ref[lo + half:lo + ATT_QKDIM, :] = (x1 * st + x2 * ct).astype(BF16)
    ones = jnp.ones((ATT_VPAD - ATT_VDIM, tm), BF16)
    for j in range(ATT_HEADS):
        lo = ATT_WIDTH + j * ATT_VDIM
        vt = lax.dot_general(wqvt_ref[lo:lo + ATT_VDIM, :], h, nt, preferred_element_type=F32)
        vt_ref[0, j * ATT_VPAD:j * ATT_VPAD + ATT_VDIM, :] = vt.astype(BF16)
        vt_ref[0, j * ATT_VPAD + ATT_VDIM:(j + 1) * ATT_VPAD, :] = ones
    cosv = cos_ref[...]
    sinv = sin_ref[...]
    for j in range(ATT_WIDTH // LANES):
        lo = ATT_WIDTH + j * LANES
        a = jnp.dot(h, w_ref[:, lo:lo + LANES], preferred_element_type=F32)
        k_ref[:, j * LANES:(j + 1) * LANES] = (a * cosv + _rot_half_64(a) * sinv).astype(BF16)
    for j in range(HGRN_COLS // 512):
        lo = 3 * ATT_WIDTH + j * 512
        hg_ref[:, j * 512:(j + 1) * 512] = jnp.dot(h, w_ref[:, lo:lo + 512],
                                                   preferred_element_type=F32)


def _in_proj(x2, g_mix, w_in_bf, w_qv_t, cos_t, sin_t, cos_tt, sin_tt, tk):
    T = x2.shape[0]
    tm = IN_TM
    per = tk // tm
    return pl.pallas_call(
        _inproj_kernel,
        grid=(T // tm,),
        in_specs=[
            pl.BlockSpec((tm, D_MODEL), lambda i: (i, 0)),
            pl.BlockSpec((1, D_MODEL), lambda i: (0, 0)),
            pl.BlockSpec((D_MODEL, IN_COLS), lambda i: (0, 0)),
            pl.BlockSpec((2 * ATT_WIDTH, D_MODEL), lambda i: (0, 0)),
            pl.BlockSpec((tm, LANES), lambda i: (i, 0)),
            pl.BlockSpec((tm, LANES), lambda i: (i, 0)),
            pl.BlockSpec((ATT_QKDIM // 2, tm), lambda i: (0, i)),
            pl.BlockSpec((ATT_QKDIM // 2, tm), lambda i: (0, i)),
        ],
        out_specs=[
            pl.BlockSpec((ATT_WIDTH, tm), lambda i: (0, i)),
            pl.BlockSpec((tm, ATT_WIDTH), lambda i: (i, 0)),
            pl.BlockSpec((1, ATT_HEADS * ATT_VPAD, tm), lambda i: (i // per, 0, i % per)),
            pl.BlockSpec((tm, HGRN_COLS), lambda i: (i, 0)),
        ],
        out_shape=[
            jax.ShapeDtypeStruct((ATT_WIDTH, T), BF16),
            jax.ShapeDtypeStruct((T, ATT_WIDTH), BF16),
            jax.ShapeDtypeStruct((T // tk, ATT_HEADS * ATT_VPAD, tk), BF16),
            jax.ShapeDtypeStruct((T, HGRN_COLS), F32),
        ],
        compiler_params=_cparams(("parallel",)),
        name="in_proj",
    )(x2, g_mix, w_in_bf, w_qv_t, cos_t, sin_t, cos_tt, sin_tt)


def _attn_kernel(lam_ref, qt_ref, qtn_ref, k_ref, vt_ref, g_ref, o_ref,
                 qw_ref, qwn_ref, sa_ref, sb_ref, acc_ref, st_ref):
    tq = qt_ref.shape[1]
    nchunk, _, tk = vt_ref.shape
    ncb = 2 * tq // ATT_CB

    def latch(dst_ref, src_ref):
        qt = src_ref[...].astype(F32)
        row = lax.broadcasted_iota(jnp.int32, qt.shape, 0)
        dst_ref[:, 0:tq] = jnp.where(row < ATT_QKDIM, qt, 0.0).astype(BF16)
        dst_ref[:, tq:] = jnp.where(row >= ATT_QKDIM, qt, 0.0).astype(BF16)

    latch(qw_ref, qt_ref)
    latch(qwn_ref, qtn_ref)
    acc_ref[...] = jnp.zeros_like(acc_ref)

    def stage(j_acc, s_acc, m_cur, alpha, j_sc, s_sc, m_old, q_ref):
        off = j_sc * tk
        m_new, al_new = [], []
        for cb in range(ncb):
            cols = slice(cb * ATT_CB, (cb + 1) * ATT_CB)
            part, cmax = None, None
            for t in range(tk // ATT_KT):
                rows = slice(t * ATT_KT, (t + 1) * ATT_KT)
                kc = k_ref[pl.ds(pl.multiple_of(off + t * ATT_KT, ATT_KT), ATT_KT), :]
                s = jnp.dot(kc, q_ref[:, cols], preferred_element_type=F32)
                s_sc[rows, cols] = s
                c = jnp.max(s, axis=0, keepdims=True)
                cmax = c if cmax is None else jnp.maximum(cmax, c)
                if s_acc is not None:
                    p = jnp.exp2(s_acc[rows, cols] - m_cur[cb]).astype(BF16)
                    d = jnp.dot(vt_ref[j_acc, :, rows], p, preferred_element_type=F32)
                    part = d if part is None else part + d
            if s_acc is not None:
                acc_ref[:, cols] = alpha[cb] * acc_ref[:, cols] + part
            mn = jnp.maximum(m_old[cb], cmax)
            m_new.append(mn)
            al_new.append(jnp.exp2(m_old[cb] - mn))
        return tuple(m_new), tuple(al_new)

    def save(st):
        for cb in range(ncb):
            cols = slice(cb * ATT_CB, (cb + 1) * ATT_CB)
            st_ref[0:1, cols] = st[0][cb]
            st_ref[1:2, cols] = st[1][cb]

    m0 = tuple(jnp.full((1, ATT_CB), -jnp.inf, F32) for _ in range(ncb))

    @pl.when(pl.program_id(1) == 0)
    def _():
        save(stage(0, None, None, None, 0, sa_ref, m0, qw_ref))

    state = (tuple(st_ref[0:1, cb * ATT_CB:(cb + 1) * ATT_CB] for cb in range(ncb)),
             tuple(st_ref[1:2, cb * ATT_CB:(cb + 1) * ATT_CB] for cb in range(ncb)))

    per_trip = 4 if nchunk % 4 == 0 else 2
    ntrip = nchunk // per_trip
    bufs = (sa_ref, sb_ref)

    def trip(tr, st, last):
        m_cur, al_cur = st
        for u in range(per_trip):
            j = per_trip * tr + u
            if last and u == per_trip - 1:
                m_cur, al_cur = stage(j, bufs[u % 2], m_cur, al_cur, 0, bufs[(u + 1) % 2], m0,
                                      qwn_ref)
            else:
                m_cur, al_cur = stage(j, bufs[u % 2], m_cur, al_cur, j + 1, bufs[(u + 1) % 2],
                                      m_cur, qw_ref)
        return m_cur, al_cur

    state = lax.fori_loop(0, ntrip - 1, lambda tr, st: trip(tr, st, False), state)
    save(trip(ntrip - 1, state, True))

    lam = lam_ref[0, 0]
    acc = acc_ref[...]
    o1 = acc[0:ATT_VDIM, 0:tq] / acc[ATT_VDIM:ATT_VDIM + 1, 0:tq]
    o2 = acc[0:ATT_VDIM, tq:] / acc[ATT_VDIM:ATT_VDIM + 1, tq:]
    o = o1 - lam * o2
    ms = jnp.mean(o * o, axis=0, keepdims=True)
    o = o * lax.rsqrt(ms + SUBLN_EPS) * g_ref[...] * (1.0 - LAMBDA_INIT)
    o_ref[...] = o.T.astype(o_ref.dtype)


def _diff_attn(lam, qt, k, vt, subln_g, tq=1024):
    T = k.shape[0]
    tq = min(tq, T)
    nq = T // tq
    nchunk, _, tk = vt.shape
    return pl.pallas_call(
        _attn_kernel,
        grid=(ATT_HEADS, nq),
        in_specs=[
            pl.BlockSpec(memory_space=pltpu.SMEM),
            pl.BlockSpec((LANES, tq), lambda h, i: (h, i)),
            pl.BlockSpec((LANES, tq), lambda h, i: (h, jnp.minimum(i + 1, nq - 1))),
            pl.BlockSpec((T, LANES), lambda h, i: (0, h)),
            pl.BlockSpec((nchunk, ATT_VPAD, tk), lambda h, i: (0, h, 0)),
            pl.BlockSpec((ATT_VDIM, 1), lambda h, i: (0, 0)),
        ],
        out_specs=pl.BlockSpec((tq, LANES), lambda h, i: (i, h)),
        out_shape=jax.ShapeDtypeStruct((T, ATT_WIDTH), BF16),
        scratch_shapes=[
            pltpu.VMEM((LANES, 2 * tq), BF16),
            pltpu.VMEM((LANES, 2 * tq), BF16),
            pltpu.VMEM((tk, 2 * tq), F32),
            pltpu.VMEM((tk, 2 * tq), F32),
            pltpu.VMEM((ATT_VPAD, 2 * tq), F32),
            pltpu.VMEM((8, 2 * tq), F32),
        ],
        compiler_params=_cparams(("arbitrary", "arbitrary")),
        name="diff_attn",
    )(lam, qt, qt, k, vt, subln_g)


def _hgrn_consts(C, backward):
    halves = []
    h = C // 2
    while h >= 1:
        halves.append(h)
        h //= 2
    E = np.zeros((2 + len(halves), C, C), np.float32)
    M = np.zeros((len(halves) + 1, C, C), np.float32)
    idx = np.arange(C)
    for t in range(C):
        if not backward:
            E[0, t, idx <= t] = 1.0
            E[1, t, idx > t] = 1.0
        else:
            E[0, t, idx >= t] = 1.0
            E[1, t, idx < t] = 1.0
    for li, h in enumerate(halves):
        for t in range(C):
            mid = (t // (2 * h)) * 2 * h + h
            upper = t >= mid
            if not backward:
                if upper:
                    E[2 + li, t, (idx >= mid) & (idx <= t)] = 1.0
                else:
                    E[2 + li, t, (idx > t) & (idx <= mid - 1)] = 1.0
            else:
                if not upper:
                    E[2 + li, t, (idx >= t) & (idx <= mid - 1)] = 1.0
                else:
                    E[2 + li, t, (idx >= mid) & (idx <= t - 1)] = 1.0
            for s in range(C):
                same = (s // (2 * h)) == (t // (2 * h))
                s_upper = s >= mid
                if same and ((not backward and upper and not s_upper)
                             or (backward and not upper and s_upper)):
                    M[li, t, s] = 1.0
    M[-1] = np.eye(C, dtype=np.float32)
    E = np.concatenate([E.reshape(-1, C), np.ones((8, C), np.float32)], axis=0)
    return E, M


def _split3_rows(g):
    hi = g.astype(BF16)
    r1 = g - hi.astype(F32)
    mid = r1.astype(BF16)
    lo = (r1 - mid.astype(F32)).astype(BF16)
    return jnp.concatenate([hi, mid, lo], axis=0)


def _hgrn_kernel(*refs, backward, nchunk, final):
    if final:
        (hq_ref, hf_ref, hi_ref, lb_ref, e_ref, m_ref, ob_ref, hgate_ref, gn_ref,
         o_ref, st_ref) = refs
    else:
        hq_ref, hf_ref, hi_ref, lb_ref, e_ref, m_ref, o_ref, st_ref = refs
    C = HGRN_CHUNK
    nlev = m_ref.shape[0] - 1
    nt = (((1,), (1,)), ((), ()))

    @pl.when(pl.program_id(0) == 0)
    def _():
        st_ref[...] = jnp.zeros_like(st_ref)

    lbd = lb_ref[...]
    emat = e_ref[...]
    order = range(nchunk - 1, -1, -1) if backward else range(nchunk)
    for c in order:
        rows = pl.ds(c * C, C)
        z = hf_ref[rows, :]
        hq = hq_ref[rows, :]
        f = lbd + (1.0 - lbd) * jax.nn.sigmoid(z)
        kk = (1.0 - lbd) * jax.nn.sigmoid(-z)
        g = jnp.log(f)
        q = hq * jax.nn.sigmoid(hq)
        vb = hi_ref[rows, :].astype(BF16)
        ex = jnp.dot(emat, _split3_rows(g), preferred_element_type=F32)
        ee = jnp.exp(ex)
        qi = (q * ee[0:C]).astype(BF16)
        ki = (kk * ee[C:2 * C]).astype(BF16)
        dec = ee[(2 + nlev) * C:(2 + nlev) * C + 1]
        qb = q.astype(BF16)
        kb = kk.astype(BF16)
        qlev = [(q * ee[(2 + l) * C:(3 + l) * C]).astype(BF16) for l in range(nlev)]
        klev = [(kk * ee[(2 + l) * C:(3 + l) * C]).astype(BF16) for l in range(nlev)]
        outs = []
        for hd in range(HGRN_HEADS):
            ln = slice(hd * LANES, (hd + 1) * LANES)
            a = m_ref[nlev] * lax.dot_general(qb[:, ln], kb[:, ln], nt, preferred_element_type=F32)
            for l in range(nlev):
                a = a + m_ref[l] * lax.dot_general(qlev[l][:, ln], klev[l][:, ln], nt,
                                                   preferred_element_type=F32)
            st = st_ref[hd]
            o = jnp.dot(a.astype(BF16), vb[:, ln], preferred_element_type=F32)
            o = o + lax.dot_general(qi[:, ln], st.astype(BF16), nt, preferred_element_type=F32)
            vt = hi_ref[rows, ln].T.astype(BF16)
            st_ref[hd] = st * dec[:, ln] + jnp.dot(vt, ki[:, ln], preferred_element_type=F32)
            outs.append(o)
        o_all = jnp.concatenate(outs, axis=1)
        if final:
            o_all = o_all + ob_ref[rows, :]
            res = []
            for hd in range(HGRN_HEADS):
                ln = slice(hd * LANES, (hd + 1) * LANES)
                oh = o_all[:, ln]
                ms = jnp.mean(oh * oh, axis=-1, keepdims=True)
                res.append(oh * lax.rsqrt(ms + NORM_EPS) * gn_ref[...])
            hg = hgate_ref[rows, :]
            o_all = jnp.concatenate(res, axis=1) * (hg * jax.nn.sigmoid(hg))
        o_ref[rows, :] = o_all.astype(o_ref.dtype)


def _hgrn_pass(hgrn_in, lb_row, backward, o_b=None, gnorm_g=None, tb=512):
    T = hgrn_in.shape[0]
    nblk = T // tb
    final = o_b is not None
    E, M = _hgrn_consts(HGRN_CHUNK, backward)
    e_bf = jnp.asarray(np.concatenate([E, E, E], axis=1), BF16)
    m_f = jnp.asarray(M, F32)
    blk = (lambda i: nblk - 1 - i) if backward else (lambda i: i)
    f_part = 2 if backward else 1
    in_specs = [
        pl.BlockSpec((tb, 512), lambda i: (blk(i), 0)),
        pl.BlockSpec((tb, 512), lambda i: (blk(i), f_part)),
        pl.BlockSpec((tb, 512), lambda i: (blk(i), 3)),
        pl.BlockSpec((1, 512), lambda i: (0, 0)),
        pl.BlockSpec(e_bf.shape, lambda i: (0, 0)),
        pl.BlockSpec(m_f.shape, lambda i: (0, 0, 0)),
    ]
    args = [hgrn_in, hgrn_in, hgrn_in, lb_row, e_bf, m_f]
    if final:
        in_specs += [
            pl.BlockSpec((tb, 512), lambda i: (blk(i), 0)),
            pl.BlockSpec((tb, 512), lambda i: (blk(i), 4)),
            pl.BlockSpec((1, HGRN_VDIM), lambda i: (0, 0)),
        ]
        args += [o_b, hgrn_in, gnorm_g]
    return pl.pallas_call(
        functools.partial(_hgrn_kernel, backward=backward, nchunk=tb // HGRN_CHUNK, final=final),
        grid=(nblk,),
        in_specs=in_specs,
        out_specs=pl.BlockSpec((tb, 512), lambda i: (blk(i), 0)),
        out_shape=jax.ShapeDtypeStruct((T, HGRN_WIDTH), BF16 if final else F32),
        scratch_shapes=[pltpu.VMEM((HGRN_HEADS, HGRN_VDIM, HGRN_EXPAND), F32)],
        compiler_params=_cparams(("arbitrary",)),
        name="hgrn_bwd" if backward else "hgrn_fwd",
    )(*args)


def _outproj_kernel(att_ref, rec_ref, x_ref, wo_ref, g_ref, wr_hi_ref, wr_lo_ref, rb_ref,
                    hres_ref, xt_ref, comb_ref):
    mix = (jnp.dot(att_ref[...], wo_ref[0:ATT_WIDTH, :], preferred_element_type=F32)
           + jnp.dot(rec_ref[...], wo_ref[ATT_WIDTH:, :], preferred_element_type=F32))
    hres = x_ref[...] + mix
    hres_ref[...] = hres
    ms = jnp.mean(hres * hres, axis=-1, keepdims=True)
    xt = hres * lax.rsqrt(ms + NORM_EPS) * g_ref[...]
    xt_ref[...] = xt
    xt_hi = xt.astype(BF16)
    xt_lo = (xt - xt_hi.astype(F32)).astype(BF16)
    logits = (jnp.dot(xt_hi, wr_hi_ref[...], preferred_element_type=F32)
              + jnp.dot(xt_hi, wr_lo_ref[...], preferred_element_type=F32)
              + jnp.dot(xt_lo, wr_hi_ref[...], preferred_element_type=F32)) + rb_ref[...]
    lane = lax.broadcasted_iota(jnp.int32, logits.shape, 1)
    neg = -jnp.inf
    big = jnp.int32(1 << 20)
    gmask = (lane >= N_EXPERTS) & (lane < N_EXPERTS + N_GROUPS)
    glog = jnp.where(gmask, logits, neg)
    gmax = jnp.max(glog, axis=-1, keepdims=True)
    gsum = jnp.sum(jnp.exp(glog - gmax), axis=-1, keepdims=True)
    p_g = 1.0 / gsum
    g_idx = jnp.min(jnp.where(glog == gmax, lane, big), axis=-1, keepdims=True) - N_EXPERTS
    emask = (lane < N_EXPERTS) & ((lane >> 3) == g_idx)
    elog = jnp.where(emask, logits, neg)
    e1 = jnp.max(elog, axis=-1, keepdims=True)
    i1 = jnp.min(jnp.where(elog == e1, lane, big), axis=-1, keepdims=True)
    elog2 = jnp.where(lane == i1, neg, elog)
    e2 = jnp.max(elog2, axis=-1, keepdims=True)
    i2 = jnp.min(jnp.where(elog2 == e2, lane, big), axis=-1, keepdims=True)
    r = jnp.exp(e2 - e1)
    w1 = p_g / (1.0 + r)
    w2 = p_g * r / (1.0 + r)
    comb_ref[...] = jnp.where(lane == 0, i1.astype(F32), jnp.where(
        lane == 1, i2.astype(F32), jnp.where(lane == 2, w1, jnp.where(lane == 3, w2, 0.0))))


def _out_proj(att, rec, x2, w_out_bf, g_ffn, wr_hi, wr_lo, rbias, tm=512):
    T = x2.shape[0]
    row = lambda i: (i, 0)
    fixed = lambda i: (0, 0)
    return pl.pallas_call(
        _outproj_kernel,
        grid=(T // tm,),
        in_specs=[
            pl.BlockSpec((tm, ATT_WIDTH), row),
            pl.BlockSpec((tm, HGRN_WIDTH), row),
            pl.BlockSpec((tm, D_MODEL), row),
            pl.BlockSpec((D_MODEL, D_MODEL), fixed),
            pl.BlockSpec((1, D_MODEL), fixed),
            pl.BlockSpec((D_MODEL, ROUTER_LANES), fixed),
            pl.BlockSpec((D_MODEL, ROUTER_LANES), fixed),
            pl.BlockSpec((1, ROUTER_LANES), fixed),
        ],
        out_specs=[
            pl.BlockSpec((tm, D_MODEL), row),
            pl.BlockSpec((tm, D_MODEL), row),
            pl.BlockSpec((tm, ROUTER_LANES), row),
        ],
        out_shape=[
            jax.ShapeDtypeStruct((T, D_MODEL), F32),
            jax.ShapeDtypeStruct((T, D_MODEL), F32),
            jax.ShapeDtypeStruct((T, ROUTER_LANES), F32),
        ],
        compiler_params=_cparams(("parallel",)),
        name="out_proj_router",
    )(att, rec, x2, w_out_bf, g_ffn, wr_hi, wr_lo, rbias)


def _split3_f32(w):
    hi = w.astype(BF16).astype(F32)
    r1 = w - hi
    mid = r1.astype(BF16).astype(F32)
    lo = (r1 - mid).astype(BF16).astype(F32)
    return hi, mid, lo


def _route_sort_kernel(rt_ref, ltri_ref, utri_ref, lists_ref, meta_ref):
    rt = rt_ref[...]
    bt = rt.shape[0]
    lane = lax.broadcasted_iota(jnp.int32, rt.shape, 1)
    lane_f = lane.astype(F32)
    oh1 = lane_f == rt[:, 0:1]
    oh2 = lane_f == rt[:, 1:2]
    oh = jnp.where(oh1, 1.0, jnp.where(oh2, 1.0, 0.0))
    oh_b = oh.astype(BF16)
    rank = jnp.dot(ltri_ref[...], oh_b, preferred_element_type=F32)
    below = jnp.dot(oh_b, utri_ref[...], preferred_element_type=F32)
    offs = jnp.sum(below, axis=0, keepdims=True)
    cnt = jnp.sum(oh, axis=0, keepdims=True)
    posmat = rank + offs
    pos1 = jnp.sum(jnp.where(oh1, posmat, 0.0), axis=1, keepdims=True)
    pos2 = jnp.sum(jnp.where(oh2, posmat, 0.0), axis=1, keepdims=True)
    tok = lax.broadcasted_iota(jnp.int32, rt.shape, 0)
    tok_hi = (tok >> 5).astype(F32)
    tok_lo = (tok & 31).astype(F32)

    def record(w, slot):
        hi, mid, lo = _split3_f32(w)
        d = jnp.where(lane == 0, tok_hi, jnp.where(lane == 1, tok_lo, jnp.where(
            lane == 2, hi, jnp.where(lane == 3, mid, jnp.where(lane == 4, lo, jnp.where(
                lane == 5, slot, 0.0))))))
        return d.T.astype(BF16)

    d1 = record(rt[:, 2:3], 0.0)
    d2 = record(rt[:, 3:4], 1.0)
    cw = 512
    for c in range(2 * bt // cw):
        colp = (lax.broadcasted_iota(jnp.int32, (bt, cw), 1) + c * cw).astype(F32)
        o1 = jnp.where(colp == pos1, 1.0, 0.0).astype(BF16)
        o2 = jnp.where(colp == pos2, 1.0, 0.0).astype(BF16)
        srt = (jnp.dot(d1, o1, preferred_element_type=F32)
               + jnp.dot(d2, o2, preferred_element_type=F32))
        lists_ref[0, :, c * cw:(c + 1) * cw] = srt[0:8]
    row = lax.broadcasted_iota(jnp.int32, (8, LANES), 0)
    meta_ref[0] = jnp.where(row == 0, cnt, jnp.where(row == 1, offs, 0.0))


def _route_sort(route, bt):
    T = route.shape[0]
    nb = T // bt
    ltri = jnp.asarray(np.tril(np.ones((bt, bt), np.float32), -1), BF16)
    utri = jnp.asarray(np.triu(np.ones((LANES, LANES), np.float32), 1), BF16)
    return pl.pallas_call(
        _route_sort_kernel,
        grid=(nb,),
        in_specs=[
            pl.BlockSpec((bt, ROUTER_LANES), lambda b: (b, 0)),
            pl.BlockSpec((bt, bt), lambda b: (0, 0)),
            pl.BlockSpec((LANES, LANES), lambda b: (0, 0)),
        ],
        out_specs=[
            pl.BlockSpec((1, 8, 2 * bt), lambda b: (b, 0, 0)),
            pl.BlockSpec((1, 8, LANES), lambda b: (b, 0, 0)),
        ],
        out_shape=[
            jax.ShapeDtypeStruct((nb, 8, 2 * bt), F32),
            jax.ShapeDtypeStruct((nb, 8, LANES), F32),
        ],
        compiler_params=_cparams(("parallel",)),
        name="route_sort",
    )(route, ltri, utri)


def _moe_kernel(cnt_ref, off_ref, idx_ref, ws_ref, xt_ref, hres_ref, wg_ref, wu_ref, wd_ref, gf_ref,
                o_ref, yb_ref, xs_ref, y_ref):
    b = pl.program_id(0)
    e = pl.program_id(1)
    bt = xt_ref.shape[0]
    rows = xs_ref.shape[0] * xs_ref.shape[1]

    @pl.when(e == 0)
    def _():
        xs_ref[...] = jnp.zeros_like(xs_ref)

    n = cnt_ref[b * N_EXPERTS + e]
    base = b * 2 * bt + off_ref[b * N_EXPERTS + e]

    def tile(r, carry):
        r0 = base + r * rows
        m = jnp.minimum(n - r * rows, rows)
        trips = (m + MOE_UNROLL - 1) // MOE_UNROLL

        def gather(i, c):
            for k in range(MOE_UNROLL):
                t = idx_ref[r0 + i * MOE_UNROLL + k] & (bt - 1)
                xs_ref[i, pl.ds(k, 1), :] = xt_ref[pl.ds(t, 1), :]
            return c

        lax.fori_loop(0, trips, gather, 0)
        xs = xs_ref[...].reshape(rows, D_MODEL).astype(BF16)
        a = jnp.dot(xs, wg_ref[0], preferred_element_type=F32)
        u = jnp.dot(xs, wu_ref[0], preferred_element_type=F32)
        hid = (a * jax.nn.sigmoid(a)) * u
        y_ref[...] = jnp.dot(hid.astype(BF16), wd_ref[0],
                             preferred_element_type=F32).reshape(y_ref.shape)

        def scatter(i, c):
            for k in range(MOE_UNROLL):
                j = i * MOE_UNROLL + k
                dst = jnp.where(j < m, idx_ref[r0 + j], 2 * bt + k)
                yb_ref[pl.ds(dst, 1), :] = ws_ref[r0 + j] * y_ref[i, pl.ds(k, 1), :]
            return c

        lax.fori_loop(0, trips, scatter, 0)
        return carry

    lax.fori_loop(0, (n + rows - 1) // rows, tile, 0)

    @pl.when(e == pl.num_programs(1) - 1)
    def _():
        y = hres_ref[...] + yb_ref[0:bt, :] + yb_ref[bt:2 * bt, :]
        ms = jnp.mean(y * y, axis=-1, keepdims=True)
        o_ref[...] = y * lax.rsqrt(ms + NORM_EPS) * gf_ref[...]


def _moe(cnt, off, idx, ws, xt, hres, wg, wu, wd, g_final, bt):
    T = xt.shape[0]
    row = lambda i, e, *_: (i, 0)
    grid_spec = pltpu.PrefetchScalarGridSpec(
        num_scalar_prefetch=4,
        grid=(T // bt, N_EXPERTS),
        in_specs=[
            pl.BlockSpec((bt, D_MODEL), row, pipeline_mode=pl.Buffered(1)),
            pl.BlockSpec((bt, D_MODEL), row, pipeline_mode=pl.Buffered(1)),
            pl.BlockSpec((1, D_MODEL, D_EXPERT), lambda i, e, *_: (e, 0, 0)),
            pl.BlockSpec((1, D_MODEL, D_EXPERT), lambda i, e, *_: (e, 0, 0)),
            pl.BlockSpec((1, D_EXPERT, D_MODEL), lambda i, e, *_: (e, 0, 0)),
            pl.BlockSpec((1, D_MODEL), lambda i, e, *_: (0, 0)),
        ],
        out_specs=pl.BlockSpec((bt, D_MODEL), row, pipeline_mode=pl.Buffered(1)),
        scratch_shapes=[
            pltpu.VMEM((2 * bt + MOE_UNROLL, D_MODEL), F32),
            pltpu.VMEM((MOE_ROWS // MOE_UNROLL, MOE_UNROLL, D_MODEL), F32),
            pltpu.VMEM((MOE_ROWS // MOE_UNROLL, MOE_UNROLL, D_MODEL), F32),
        ],
    )
    return pl.pallas_call(
        _moe_kernel,
        grid_spec=grid_spec,
        out_shape=jax.ShapeDtypeStruct((T, D_MODEL), F32),
        compiler_params=_cparams(("parallel", "arbitrary")),
        name="moe",
    )(cnt, off, idx, ws, xt, hres, wg, wu, wd, g_final)


def _rope_tables(T):
    inv = ROPE_THETA ** (-np.arange(0, ATT_QKDIM, 2, dtype=np.float64) / ATT_QKDIM)
    na = -(-T // ROPE_SPLIT)
    ang_a = (np.arange(na, dtype=np.float64) * ROPE_SPLIT)[:, None] * inv[None, :]
    ang_b = np.arange(ROPE_SPLIT, dtype=np.float64)[:, None] * inv[None, :]
    ca, sa = jnp.asarray(np.cos(ang_a), F32)[:, None, :], jnp.asarray(np.sin(ang_a), F32)[:, None, :]
    cb, sb = jnp.asarray(np.cos(ang_b), F32)[None, :, :], jnp.asarray(np.sin(ang_b), F32)[None, :, :]
    c = (ca * cb - sa * sb).reshape(na * ROPE_SPLIT, -1)[:T]
    s = (sa * cb + ca * sb).reshape(na * ROPE_SPLIT, -1)[:T]
    return (jnp.concatenate([c, c, c, c], axis=1), jnp.concatenate([-s, s, -s, s], axis=1),
            c.T, s.T)


def kernel(x, w_in, w_out, g_mix, lam_params, subln_g, hgrn_gnorm_g, hgrn_lb, g_ffn, w_gr, b_gr,
           w_er, b_er, w_gate, w_up, w_down, g_final):
    B, T, D = x.shape
    x2 = x.reshape(B * T, D)
    l = 0
    w_in_bf = w_in[l].astype(BF16)
    w_out_bf = w_out[l].astype(BF16)
    lp = lam_params[l].astype(F32)
    lam = (jnp.exp(jnp.sum(lp[0] * lp[1])) - jnp.exp(jnp.sum(lp[2] * lp[3])) + LAMBDA_INIT).reshape(1, 1)
    lb = jnp.cumsum(jax.nn.softmax(hgrn_lb.astype(F32), axis=1), axis=1)[:, l]
    w_qv_t = jnp.concatenate([w_in[l][:, 0:ATT_WIDTH], w_in[l][:, 2 * ATT_WIDTH:3 * ATT_WIDTH]],
                             axis=1).T.astype(BF16)
    cos_t, sin_t, cos_tt, sin_tt = _rope_tables(T)
    w_r = jnp.concatenate([jnp.transpose(w_er[l], (1, 0, 2)).reshape(D, N_EXPERTS), w_gr[l],
                           jnp.zeros((D, ROUTER_LANES - N_EXPERTS - N_GROUPS), F32)], axis=1)
    wr_hi = w_r.astype(BF16)
    wr_lo = (w_r - wr_hi.astype(F32)).astype(BF16)
    rbias = jnp.concatenate([b_er[l].reshape(-1), b_gr[l],
                             jnp.zeros((ROUTER_LANES - N_EXPERTS - N_GROUPS,), F32)]).reshape(1, -1)
    wg = w_gate[l].reshape(N_EXPERTS, D, D_EXPERT).astype(BF16)
    wu = w_up[l].reshape(N_EXPERTS, D, D_EXPERT).astype(BF16)
    wd = w_down[l].reshape(N_EXPERTS, D_EXPERT, D).astype(BF16)

    tk = min(ATT_TK, (B * T) // 2)
    qt, k, vt, hgrn_in = _in_proj(x2, g_mix[l].reshape(1, D), w_in_bf, w_qv_t, cos_t, sin_t,
                                  cos_tt, sin_tt, tk)
    att = _diff_attn(lam, qt, k, vt, subln_g[l].reshape(-1, 1))
    o_b = _hgrn_pass(hgrn_in, lb[1:2], backward=True)
    rec = _hgrn_pass(hgrn_in, lb[0:1], backward=False, o_b=o_b,
                     gnorm_g=hgrn_gnorm_g[l].reshape(1, -1))
    hres, xt, route = _out_proj(att, rec, x2, w_out_bf, g_ffn[l].reshape(1, D), wr_hi, wr_lo, rbias)
    bt = min(MOE_BT, B * T)
    lists, meta = _route_sort(route, bt)
    pad = jnp.zeros((MOE_UNROLL,), F32)
    idx = jnp.concatenate([(lists[:, 5] * bt + lists[:, 0] * 32.0 + lists[:, 1]).reshape(-1),
                           pad]).astype(jnp.int32)
    ws = jnp.concatenate([(lists[:, 2] + lists[:, 3] + lists[:, 4]).reshape(-1), pad])
    cnt = meta[:, 0, :N_EXPERTS].astype(jnp.int32).reshape(-1)
    off = meta[:, 1, :N_EXPERTS].astype(jnp.int32).reshape(-1)
    out = _moe(cnt, off, idx, ws, xt, hres, wg, wu, wd, g_final.reshape(1, D), bt)
    return out.reshape(B, T, D)
```

```python
import functools
import math

import numpy as np
import jax
import jax.numpy as jnp
from jax import lax
from jax.experimental import pallas as pl
from jax.experimental.pallas import tpu as pltpu

D_MODEL = 1024
ATT_WIDTH = 512
ATT_HEADS = 4
ATT_VDIM = 128
ATT_QKDIM = 64
HGRN_WIDTH = 512
HGRN_HEADS = 4
HGRN_VDIM = 128
HGRN_EXPAND = 128
HGRN_FDIM = 512
N_GROUPS = 4
EXPERTS_PER_GROUP = 8
N_EXPERTS = N_GROUPS * EXPERTS_PER_GROUP
D_EXPERT = 512
ROPE_THETA = 10000.0
NORM_EPS = 1e-6
SUBLN_EPS = 1e-5
LAMBDA_INIT = 0.8 - 0.6 * math.exp(-0.3 * 0)
LOG2E = math.log2(math.e)
ROPE_SPLIT = 256
IN_COLS = 3 * ATT_WIDTH + 3 * HGRN_FDIM + 2 * HGRN_WIDTH
HGRN_COLS = IN_COLS - 3 * ATT_WIDTH

LANES = 128
VMEM_LIMIT = 56 * 1024 * 1024
HGRN_CHUNK = 64
IN_TM = 512
ATT_TK = 1024
ATT_CB = 256
ATT_KT = 256
MOE_BT = 2048
MOE_ROWS = 256
MOE_UNROLL = 8
ATT_VPAD = ATT_VDIM + 16
ROUTER_LANES = LANES

BF16 = jnp.bfloat16
F32 = jnp.float32


def _cparams(sem):
    return pltpu.CompilerParams(dimension_semantics=sem, vmem_limit_bytes=VMEM_LIMIT)


def _rot_half_64(x):
    lane = lax.broadcasted_iota(jnp.int32, x.shape, 1)
    fwd = pltpu.roll(x, 32, axis=1)
    bwd = pltpu.roll(x, 96, axis=1)
    return jnp.where((lane & 63) < 32, bwd, fwd)


def _inproj_kernel(x_ref, g_ref, w_ref, wqvt_ref, cos_ref, sin_ref, cost_ref, sint_ref,
                   qt_ref, k_ref, vt_ref, hg_ref):
    x = x_ref[...]
    tm = x.shape[0]
    ms = jnp.mean(x * x, axis=-1, keepdims=True)
    h = (x * lax.rsqrt(ms + NORM_EPS) * g_ref[...]).astype(BF16)
    nt = (((1,), (1,)), ((), ()))
    half = ATT_QKDIM // 2
    ct = cost_ref[...] * (ATT_QKDIM ** -0.5 * LOG2E)
    st = sint_ref[...] * (ATT_QKDIM ** -0.5 * LOG2E)
    for j in range(ATT_WIDTH // LANES):
        a = lax.dot_general(wqvt_ref[j * LANES:(j + 1) * LANES, :], h, nt,
                            preferred_element_type=F32)
        for c in range(LANES // ATT_QKDIM):
            x1 = a[c * ATT_QKDIM:c * ATT_QKDIM + half]
            x2 = a[c * ATT_QKDIM + half:(c + 1) * ATT_QKDIM]
            lo = j * LANES + c * ATT_QKDIM
            qt_ref[lo:lo + half, :] = (x1 * ct - x2 * st).astype(BF16)
            qt_ref[lo + half:lo + ATT_QKDIM, :] = (x1 * st + x2 * ct).astype(BF16)
    ones = jnp.ones((ATT_VPAD - ATT_VDIM, tm), BF16)
    for j in range(ATT_HEADS):
        lo = ATT_WIDTH + j * ATT_VDIM
        vt = lax.dot_general(wqvt_ref[lo:lo + ATT_VDIM, :], h, nt, preferred_element_type=F32)
        vt_ref[0, j * ATT_VPAD:j * ATT_VPAD + ATT_VDIM, :] = vt.astype(BF16)
        vt_ref[0, j * ATT_VPAD + ATT_VDIM:(j + 1) * ATT_VPAD, :] = ones
    cosv = cos_ref[...]
    sinv = sin_ref[...]
    for j in range(ATT_WIDTH // LANES):
        lo = ATT_WIDTH + j * LANES
        a = jnp.dot(h, w_ref[:, lo:lo + LANES], preferred_element_type=F32)
        k_ref[:, j * LANES:(j + 1) * LANES] = (a * cosv + _rot_half_64(a) * sinv).astype(BF16)
    for j in range(HGRN_COLS // 512):
        lo = 3 * ATT_WIDTH + j * 512
        hg_ref[:, j * 512:(j + 1) * 512] = jnp.dot(h, w_ref[:, lo:lo + 512],
                                                   preferred_element_type=F32)


def _in_proj(x2, g_mix, w_in_bf, w_qv_t, cos_t, sin_t, cos_tt, sin_tt, tk):
    T = x2.shape[0]
    tm = IN_TM
    per = tk // tm
    return pl.pallas_call(
        _inproj_kernel,
        grid=(T // tm,),
        in_specs=[
            pl.BlockSpec((tm, D_MODEL), lambda i: (i, 0)),
            pl.BlockSpec((1, D_MODEL), lambda i: (0, 0)),
            pl.BlockSpec((D_MODEL, IN_COLS), lambda i: (0, 0)),
            pl.BlockSpec((2 * ATT_WIDTH, D_MODEL), lambda i: (0, 0)),
            pl.BlockSpec((tm, LANES), lambda i: (i, 0)),
            pl.BlockSpec((tm, LANES), lambda i: (i, 0)),
            pl.BlockSpec((ATT_QKDIM // 2, tm), lambda i: (0, i)),
            pl.BlockSpec((ATT_QKDIM // 2, tm), lambda i: (0, i)),
        ],
        out_specs=[
            pl.BlockSpec((ATT_WIDTH, tm), lambda i: (0, i)),
            pl.BlockSpec((tm, ATT_WIDTH), lambda i: (i, 0)),
            pl.BlockSpec((1, ATT_HEADS * ATT_VPAD, tm), lambda i: (i // per, 0, i % per)),
            pl.BlockSpec((tm, HGRN_COLS), lambda i: (i, 0)),
        ],
        out_shape=[
            jax.ShapeDtypeStruct((ATT_WIDTH, T), BF16),
            jax.ShapeDtypeStruct((T, ATT_WIDTH), BF16),
            jax.ShapeDtypeStruct((T // tk, ATT_HEADS * ATT_VPAD, tk), BF16),
            jax.ShapeDtypeStruct((T, HGRN_COLS), F32),
        ],
        compiler_params=_cparams(("parallel",)),
        name="in_proj",
    )(x2, g_mix, w_in_bf, w_qv_t, cos_t, sin_t, cos_tt, sin_tt)


def _attn_kernel(lam_ref, qt_ref, qtn_ref, k_ref, vt_ref, g_ref, o_ref,
                 qw_ref, qwn_ref, sa_ref, sb_ref, acc_ref, st_ref):
    tq = qt_ref.shape[1]
    nchunk, _, tk = vt_ref.shape
    ncb = 2 * tq // ATT_CB

    def latch(dst_ref, src_ref):
        qt = src_ref[...].astype(F32)
        row = lax.broadcasted_iota(jnp.int32, qt.shape, 0)
        dst_ref[:, 0:tq] = jnp.where(row < ATT_QKDIM, qt, 0.0).astype(BF16)
        dst_ref[:, tq:] = jnp.where(row >= ATT_QKDIM, qt, 0.0).astype(BF16)

    latch(qw_ref, qt_ref)
    latch(qwn_ref, qtn_ref)
    acc_ref[...] = jnp.zeros_like(acc_ref)

    def stage(j_acc, s_acc, m_cur, alpha, j_sc, s_sc, m_old, q_ref):
        off = j_sc * tk
        m_new, al_new = [], []
        for cb in range(ncb):
            cols = slice(cb * ATT_CB, (cb + 1) * ATT_CB)
            part, cmax = None, None
            for t in range(tk // ATT_KT):
                rows = slice(t * ATT_KT, (t + 1) * ATT_KT)
                kc = k_ref[pl.ds(pl.multiple_of(off + t * ATT_KT, ATT_KT), ATT_KT), :]
                s = jnp.dot(kc, q_ref[:, cols], preferred_element_type=F32)
                s_sc[rows, cols] = s
                c = jnp.max(s, axis=0, keepdims=True)
                cmax = c if cmax is None else jnp.maximum(cmax, c)
                if s_acc is not None:
                    p = jnp.exp2(s_acc[rows, cols] - m_cur[cb]).astype(BF16)
                    d = jnp.dot(vt_ref[j_acc, :, rows], p, preferred_element_type=F32)
                    part = d if part is None else part + d
            if s_acc is not None:
                acc_ref[:, cols] = alpha[cb] * acc_ref[:, cols] + part
            mn = jnp.maximum(m_old[cb], cmax)
            m_new.append(mn)
            al_new.append(jnp.exp2(m_old[cb] - mn))
        return tuple(m_new), tuple(al_new)

    def save(st):
        for cb in range(ncb):
            cols = slice(cb * ATT_CB, (cb + 1) * ATT_CB)
            st_ref[0:1, cols] = st[0][cb]
            st_ref[1:2, cols] = st[1][cb]

    m0 = tuple(jnp.full((1, ATT_CB), -jnp.inf, F32) for _ in range(ncb))

    @pl.when(pl.program_id(1) == 0)
    def _():
        save(stage(0, None, None, None, 0, sa_ref, m0, qw_ref))

    state = (tuple(st_ref[0:1, cb * ATT_CB:(cb + 1) * ATT_CB] for cb in range(ncb)),
             tuple(st_ref[1:2, cb * ATT_CB:(cb + 1) * ATT_CB] for cb in range(ncb)))

    per_trip = 4 if nchunk % 4 == 0 else 2
    ntrip = nchunk // per_trip
    bufs = (sa_ref, sb_ref)

    def trip(tr, st, last):
        m_cur, al_cur = st
        for u in range(per_trip):
            j = per_trip * tr + u
            if last and u == per_trip - 1:
                m_cur, al_cur = stage(j, bufs[u % 2], m_cur, al_cur, 0, bufs[(u + 1) % 2], m0,
                                      qwn_ref)
            else:
                m_cur, al_cur = stage(j, bufs[u % 2], m_cur, al_cur, j + 1, bufs[(u + 1) % 2],
                                      m_cur, qw_ref)
        return m_cur, al_cur

    state = lax.fori_loop(0, ntrip - 1, lambda tr, st: trip(tr, st, False), state)
    save(trip(ntrip - 1, state, True))

    lam = lam_ref[0, 0]
    acc = acc_ref[...]
    o1 = acc[0:ATT_VDIM, 0:tq] / acc[ATT_VDIM:ATT_VDIM + 1, 0:tq]
    o2 = acc[0:ATT_VDIM, tq:] / acc[ATT_VDIM:ATT_VDIM + 1, tq:]
    o = o1 - lam * o2
    ms = jnp.mean(o * o, axis=0, keepdims=True)
    o = o * lax.rsqrt(ms + SUBLN_EPS) * g_ref[...] * (1.0 - LAMBDA_INIT)
    o_ref[...] = o.T.astype(o_ref.dtype)


def _diff_attn(lam, qt, k, vt, subln_g, tq=1024):
    T = k.shape[0]
    tq = min(tq, T)
    nq = T // tq
    nchunk, _, tk = vt.shape
    return pl.pallas_call(
        _attn_kernel,
        grid=(ATT_HEADS, nq),
        in_specs=[
            pl.BlockSpec(memory_space=pltpu.SMEM),
            pl.BlockSpec((LANES, tq), lambda h, i: (h, i)),
            pl.BlockSpec((LANES, tq), lambda h, i: (h, jnp.minimum(i + 1, nq - 1))),
            pl.BlockSpec((T, LANES), lambda h, i: (0, h)),
            pl.BlockSpec((nchunk, ATT_VPAD, tk), lambda h, i: (0, h, 0)),
            pl.BlockSpec((ATT_VDIM, 1), lambda h, i: (0, 0)),
        ],
        out_specs=pl.BlockSpec((tq, LANES), lambda h, i: (i, h)),
        out_shape=jax.ShapeDtypeStruct((T, ATT_WIDTH), BF16),
        scratch_shapes=[
            pltpu.VMEM((LANES, 2 * tq), BF16),
            pltpu.VMEM((LANES, 2 * tq), BF16),
            pltpu.VMEM((tk, 2 * tq), F32),
            pltpu.VMEM((tk, 2 * tq), F32),
            pltpu.VMEM((ATT_VPAD, 2 * tq), F32),
            pltpu.VMEM((8, 2 * tq), F32),
        ],
        compiler_params=_cparams(("arbitrary", "arbitrary")),
        name="diff_attn",
    )(lam, qt, qt, k, vt, subln_g)


def _hgrn_consts(C, backward):
    halves = []
    h = C // 2
    while h >= 1:
        halves.append(h)
        h //= 2
    E = np.zeros((2 + len(halves), C, C), np.float32)
    M = np.zeros((len(halves) + 1, C, C), np.float32)
    idx = np.arange(C)
    for t in range(C):
        if not backward:
            E[0, t, idx <= t] = 1.0
            E[1, t, idx > t] = 1.0
        else:
            E[0, t, idx >= t] = 1.0
            E[1, t, idx < t] = 1.0
    for li, h in enumerate(halves):
        for t in range(C):
            mid = (t // (2 * h)) * 2 * h + h
            upper = t >= mid
            if not backward:
                if upper:
                    E[2 + li, t, (idx >= mid) & (idx <= t)] = 1.0
                else:
                    E[2 + li, t, (idx > t) & (idx <= mid - 1)] = 1.0
            else:
                if not upper:
                    E[2 + li, t, (idx >= t) & (idx <= mid - 1)] = 1.0
                else:
                    E[2 + li, t, (idx >= mid) & (idx <= t - 1)] = 1.0
            for s in range(C):
                same = (s // (2 * h)) == (t // (2 * h))
                s_upper = s >= mid
                if same and ((not backward and upper and not s_upper)
                             or (backward and not upper and s_upper)):
                    M[li, t, s] = 1.0
    M[-1] = np.eye(C, dtype=np.float32)
    E = np.concatenate([E.reshape(-1, C), np.ones((8, C), np.float32)], axis=0)
    return E, M


def _split3_rows(g):
    hi = g.astype(BF16)
    r1 = g - hi.astype(F32)
    mid = r1.astype(BF16)
    lo = (r1 - mid.astype(F32)).astype(BF16)
    return jnp.concatenate([hi, mid, lo], axis=0)


def _hgrn_kernel(*refs, backward, nchunk, final):
    if final:
        (hq_ref, hf_ref, hi_ref, lb_ref, e_ref, m_ref, ob_ref, hgate_ref, gn_ref,
         o_ref, st_ref) = refs
    else:
        hq_ref, hf_ref, hi_ref, lb_ref, e_ref, m_ref, o_ref, st_ref = refs
    C = HGRN_CHUNK
    nlev = m_ref.shape[0] - 1
    nt = (((1,), (1,)), ((), ()))

    @pl.when(pl.program_id(0) == 0)
    def _():
        st_ref[...] = jnp.zeros_like(st_ref)

    lbd = lb_ref[...]
    emat = e_ref[...]
    order = range(nchunk - 1, -1, -1) if backward else range(nchunk)
    for c in order:
        rows = pl.ds(c * C, C)
        z = hf_ref[rows, :]
        hq = hq_ref[rows, :]
        f = lbd + (1.0 - lbd) * jax.nn.sigmoid(z)
        kk = (1.0 - lbd) * jax.nn.sigmoid(-z)
        g = jnp.log(f)
        q = hq * jax.nn.sigmoid(hq)
        vb = hi_ref[rows, :].astype(BF16)
        ex = jnp.dot(emat, _split3_rows(g), preferred_element_type=F32)
        ee = jnp.exp(ex)
        qi = (q * ee[0:C]).astype(BF16)
        ki = (kk * ee[C:2 * C]).astype(BF16)
        dec = ee[(2 + nlev) * C:(2 + nlev) * C + 1]
        qb = q.astype(BF16)
        kb = kk.astype(BF16)
        qlev = [(q * ee[(2 + l) * C:(3 + l) * C]).astype(BF16) for l in range(nlev)]
        klev = [(kk * ee[(2 + l) * C:(3 + l) * C]).astype(BF16) for l in range(nlev)]
        outs = []
        for hd in range(HGRN_HEADS):
            ln = slice(hd * LANES, (hd + 1) * LANES)
            a = m_ref[nlev] * lax.dot_general(qb[:, ln], kb[:, ln], nt, preferred_element_type=F32)
            for l in range(nlev):
                a = a + m_ref[l] * lax.dot_general(qlev[l][:, ln], klev[l][:, ln], nt,
                                                   preferred_element_type=F32)
            st = st_ref[hd]
            o = jnp.dot(a.astype(BF16), vb[:, ln], preferred_element_type=F32)
            o = o + lax.dot_general(qi[:, ln], st.astype(BF16), nt, preferred_element_type=F32)
            vt = hi_ref[rows, ln].T.astype(BF16)
            st_ref[hd] = st * dec[:, ln] + jnp.dot(vt, ki[:, ln], preferred_element_type=F32)
            outs.append(o)
        o_all = jnp.concatenate(outs, axis=1)
        if final:
            o_all = o_all + ob_ref[rows, :]
            res = []
            for hd in range(HGRN_HEADS):
                ln = slice(hd * LANES, (hd + 1) * LANES)
                oh = o_all[:, ln]
                ms = jnp.mean(oh * oh, axis=-1, keepdims=True)
                res.append(oh * lax.rsqrt(ms + NORM_EPS) * gn_ref[...])
            hg = hgate_ref[rows, :]
            o_all = jnp.concatenate(res, axis=1) * (hg * jax.nn.sigmoid(hg))
        o_ref[rows, :] = o_all.astype(o_ref.dtype)


def _hgrn_pass(hgrn_in, lb_row, backward, o_b=None, gnorm_g=None, tb=512):
    T = hgrn_in.shape[0]
    nblk = T // tb
    final = o_b is not None
    E, M = _hgrn_consts(HGRN_CHUNK, backward)
    e_bf = jnp.asarray(np.concatenate([E, E, E], axis=1), BF16)
    m_f = jnp.asarray(M, F32)
    blk = (lambda i: nblk - 1 - i) if backward else (lambda i: i)
    f_part = 2 if backward else 1
    in_specs = [
        pl.BlockSpec((tb, 512), lambda i: (blk(i), 0)),
        pl.BlockSpec((tb, 512), lambda i: (blk(i), f_part)),
        pl.BlockSpec((tb, 512), lambda i: (blk(i), 3)),
        pl.BlockSpec((1, 512), lambda i: (0, 0)),
        pl.BlockSpec(e_bf.shape, lambda i: (0, 0)),
        pl.BlockSpec(m_f.shape, lambda i: (0, 0, 0)),
    ]
    args = [hgrn_in, hgrn_in, hgrn_in, lb_row, e_bf, m_f]
    if final:
        in_specs += [
            pl.BlockSpec((tb, 512), lambda i: (blk(i), 0)),
            pl.BlockSpec((tb, 512), lambda i: (blk(i), 4)),
            pl.BlockSpec((1, HGRN_VDIM), lambda i: (0, 0)),
        ]
        args += [o_b, hgrn_in, gnorm_g]
    return pl.pallas_call(
        functools.partial(_hgrn_kernel, backward=backward, nchunk=tb // HGRN_CHUNK, final=final),
        grid=(nblk,),
        in_specs=in_specs,
        out_specs=pl.BlockSpec((tb, 512), lambda i: (blk(i), 0)),
        out_shape=jax.ShapeDtypeStruct((T, HGRN_WIDTH), BF16 if final else F32),
        scratch_shapes=[pltpu.VMEM((HGRN_HEADS, HGRN_VDIM, HGRN_EXPAND), F32)],
        compiler_params=_cparams(("arbitrary",)),
        name="hgrn_bwd" if backward else "hgrn_fwd",
    )(*args)


def _outproj_kernel(att_ref, rec_ref, x_ref, wo_ref, g_ref, wr_hi_ref, wr_lo_ref, rb_ref,
                    hres_ref, xt_ref, comb_ref):
    mix = (jnp.dot(att_ref[...], wo_ref[0:ATT_WIDTH, :], preferred_element_type=F32)
           + jnp.dot(rec_ref[...], wo_ref[ATT_WIDTH:, :], preferred_element_type=F32))
    hres = x_ref[...] + mix
    hres_ref[...] = hres
    ms = jnp.mean(hres * hres, axis=-1, keepdims=True)
    xt = hres * lax.rsqrt(ms + NORM_EPS) * g_ref[...]
    xt_ref[...] = xt
    xt_hi = xt.astype(BF16)
    xt_lo = (xt - xt_hi.astype(F32)).astype(BF16)
    logits = (jnp.dot(xt_hi, wr_hi_ref[...], preferred_element_type=F32)
              + jnp.dot(xt_hi, wr_lo_ref[...], preferred_element_type=F32)
              + jnp.dot(xt_lo, wr_hi_ref[...], preferred_element_type=F32)) + rb_ref[...]
    lane = lax.broadcasted_iota(jnp.int32, logits.shape, 1)
    neg = -jnp.inf
    big = jnp.int32(1 << 20)
    gmask = (lane >= N_EXPERTS) & (lane < N_EXPERTS + N_GROUPS)
    glog = jnp.where(gmask, logits, neg)
    gmax = jnp.max(glog, axis=-1, keepdims=True)
    gsum = jnp.sum(jnp.exp(glog - gmax), axis=-1, keepdims=True)
    p_g = 1.0 / gsum
    g_idx = jnp.min(jnp.where(glog == gmax, lane, big), axis=-1, keepdims=True) - N_EXPERTS
    emask = (lane < N_EXPERTS) & ((lane >> 3) == g_idx)
    elog = jnp.where(emask, logits, neg)
    e1 = jnp.max(elog, axis=-1, keepdims=True)
    i1 = jnp.min(jnp.where(elog == e1, lane, big), axis=-1, keepdims=True)
    elog2 = jnp.where(lane == i1, neg, elog)
    e2 = jnp.max(elog2, axis=-1, keepdims=True)
    i2 = jnp.min(jnp.where(elog2 == e2, lane, big), axis=-1, keepdims=True)
    r = jnp.exp(e2 - e1)
    w1 = p_g / (1.0 + r)
    w2 = p_g * r / (1.0 + r)
    comb_ref[...] = jnp.where(lane == 0, i1.astype(F32), jnp.where(
        lane == 1, i2.astype(F32), jnp.where(lane == 2, w1, jnp.where(lane == 3, w2, 0.0))))


def _out_proj(att, rec, x2, w_out_bf, g_ffn, wr_hi, wr_lo, rbias, tm=512):
    T = x2.shape[0]
    row = lambda i: (i, 0)
    fixed = lambda i: (0, 0)
    return pl.pallas_call(
        _outproj_kernel,
        grid=(T // tm,),
        in_specs=[
            pl.BlockSpec((tm, ATT_WIDTH), row),
            pl.BlockSpec((tm, HGRN_WIDTH), row),
            pl.BlockSpec((tm, D_MODEL), row),
            pl.BlockSpec((D_MODEL, D_MODEL), fixed),
            pl.BlockSpec((1, D_MODEL), fixed),
            pl.BlockSpec((D_MODEL, ROUTER_LANES), fixed),
            pl.BlockSpec((D_MODEL, ROUTER_LANES), fixed),
            pl.BlockSpec((1, ROUTER_LANES), fixed),
        ],
        out_specs=[
            pl.BlockSpec((tm, D_MODEL), row),
            pl.BlockSpec((tm, D_MODEL), row),
            pl.BlockSpec((tm, ROUTER_LANES), row),
        ],
        out_shape=[
            jax.ShapeDtypeStruct((T, D_MODEL), F32),
            jax.ShapeDtypeStruct((T, D_MODEL), F32),
            jax.ShapeDtypeStruct((T, ROUTER_LANES), F32),
        ],
        compiler_params=_cparams(("parallel",)),
        name="out_proj_router",
    )(att, rec, x2, w_out_bf, g_ffn, wr_hi, wr_lo, rbias)


def _split3_f32(w):
    hi = w.astype(BF16).astype(F32)
    r1 = w - hi
    mid = r1.astype(BF16).astype(F32)
    lo = (r1 - mid).astype(BF16).astype(F32)
    return hi, mid, lo


def _route_sort_kernel(rt_ref, ltri_ref, utri_ref, lists_ref, meta_ref):
    rt = rt_ref[...]
    bt = rt.shape[0]
    lane = lax.broadcasted_iota(jnp.int32, rt.shape, 1)
    lane_f = lane.astype(F32)
    oh1 = lane_f == rt[:, 0:1]
    oh2 = lane_f == rt[:, 1:2]
    oh = jnp.where(oh1, 1.0, jnp.where(oh2, 1.0, 0.0))
    oh_b = oh.astype(BF16)
    rank = jnp.dot(ltri_ref[...], oh_b, preferred_element_type=F32)
    below = jnp.dot(oh_b, utri_ref[...], preferred_element_type=F32)
    offs = jnp.sum(below, axis=0, keepdims=True)
    cnt = jnp.sum(oh, axis=0, keepdims=True)
    posmat = rank + offs
    pos1 = jnp.sum(jnp.where(oh1, posmat, 0.0), axis=1, keepdims=True)
    pos2 = jnp.sum(jnp.where(oh2, posmat, 0.0), axis=1, keepdims=True)
    tok = lax.broadcasted_iota(jnp.int32, rt.shape, 0)
    tok_hi = (tok >> 5).astype(F32)
    tok_lo = (tok & 31).astype(F32)

    def record(w, slot):
        hi, mid, lo = _split3_f32(w)
        d = jnp.where(lane == 0, tok_hi, jnp.where(lane == 1, tok_lo, jnp.where(
            lane == 2, hi, jnp.where(lane == 3, mid, jnp.where(lane == 4, lo, jnp.where(
                lane == 5, slot, 0.0))))))
        return d.T.astype(BF16)

    d1 = record(rt[:, 2:3], 0.0)
    d2 = record(rt[:, 3:4], 1.0)
    cw = 512
    for c in range(2 * bt // cw):
        colp = (lax.broadcasted_iota(jnp.int32, (bt, cw), 1) + c * cw).astype(F32)
        o1 = jnp.where(colp == pos1, 1.0, 0.0).astype(BF16)
        o2 = jnp.where(colp == pos2, 1.0, 0.0).astype(BF16)
        srt = (jnp.dot(d1, o1, preferred_element_type=F32)
               + jnp.dot(d2, o2, preferred_element_type=F32))
        lists_ref[0, :, c * cw:(c + 1) * cw] = srt[0:8]
    row = lax.broadcasted_iota(jnp.int32, (8, LANES), 0)
    meta_ref[0] = jnp.where(row == 0, cnt, jnp.where(row == 1, offs, 0.0))


def _route_sort(route, bt):
    T = route.shape[0]
    nb = T // bt
    ltri = jnp.asarray(np.tril(np.ones((bt, bt), np.float32), -1), BF16)
    utri = jnp.asarray(np.triu(np.ones((LANES, LANES), np.float32), 1), BF16)
    return pl.pallas_call(
        _route_sort_kernel,
        grid=(nb,),
        in_specs=[
            pl.BlockSpec((bt, ROUTER_LANES), lambda b: (b, 0)),
            pl.BlockSpec((bt, bt), lambda b: (0, 0)),
            pl.BlockSpec((LANES, LANES), lambda b: (0, 0)),
        ],
        out_specs=[
            pl.BlockSpec((1, 8, 2 * bt), lambda b: (b, 0, 0)),
            pl.BlockSpec((1, 8, LANES), lambda b: (b, 0, 0)),
        ],
        out_shape=[
            jax.ShapeDtypeStruct((nb, 8, 2 * bt), F32),
            jax.ShapeDtypeStruct((nb, 8, LANES), F32),
        ],
        compiler_params=_cparams(("parallel",)),
        name="route_sort",
    )(route, ltri, utri)


def _moe_kernel(cnt_ref, off_ref, idx_ref, ws_ref, xt_ref, hres_ref, wg_ref, wu_ref, wd_ref, gf_ref,
                o_ref, yb_ref, xs_ref, y_ref):
    b = pl.program_id(0)
    e = pl.program_id(1)
    bt = xt_ref.shape[0]
    rows = xs_ref.shape[0] * xs_ref.shape[1]

    @pl.when(e == 0)
    def _():
        xs_ref[...] = jnp.zeros_like(xs_ref)

    n = cnt_ref[b * N_EXPERTS + e]
    base = b * 2 * bt + off_ref[b * N_EXPERTS + e]

    def tile(r, carry):
        r0 = base + r * rows
        m = jnp.minimum(n - r * rows, rows)
        trips = (m + MOE_UNROLL - 1) // MOE_UNROLL

        def gather(i, c):
            for k in range(MOE_UNROLL):
                t = idx_ref[r0 + i * MOE_UNROLL + k] & (bt - 1)
                xs_ref[i, pl.ds(k, 1), :] = xt_ref[pl.ds(t, 1), :]
            return c

        lax.fori_loop(0, trips, gather, 0)
        xs = xs_ref[...].reshape(rows, D_MODEL).astype(BF16)
        a = jnp.dot(xs, wg_ref[0], preferred_element_type=F32)
        u = jnp.dot(xs, wu_ref[0], preferred_element_type=F32)
        hid = (a * jax.nn.sigmoid(a)) * u
        y_ref[...] = jnp.dot(hid.astype(BF16), wd_ref[0],
                             preferred_element_type=F32).reshape(y_ref.shape)

        def scatter(i, c):
            for k in range(MOE_UNROLL):
                j = i * MOE_UNROLL + k
                dst = jnp.where(j < m, idx_ref[r0 + j], 2 * bt + k)
                yb_ref[pl.ds(dst, 1), :] = ws_ref[r0 + j] * y_ref[i, pl.ds(k, 1), :]
            return c

        lax.fori_loop(0, trips, scatter, 0)
        return carry

    lax.fori_loop(0, (n + rows - 1) // rows, tile, 0)

    @pl.when(e == pl.num_programs(1) - 1)
    def _():
        y = hres_ref[...] + yb_ref[0:bt, :] + yb_ref[bt:2 * bt, :]
        ms = jnp.mean(y * y, axis=-1, keepdims=True)
        o_ref[...] = y * lax.rsqrt(ms + NORM_EPS) * gf_ref[...]


def _moe(cnt, off, idx, ws, xt, hres, wg, wu, wd, g_final, bt):
    T = xt.shape[0]
    row = lambda i, e, *_: (i, 0)
    grid_spec = pltpu.PrefetchScalarGridSpec(
        num_scalar_prefetch=4,
        grid=(T // bt, N_EXPERTS),
        in_specs=[
            pl.BlockSpec((bt, D_MODEL), row, pipeline_mode=pl.Buffered(1)),
            pl.BlockSpec((bt, D_MODEL), row, pipeline_mode=pl.Buffered(1)),
            pl.BlockSpec((1, D_MODEL, D_EXPERT), lambda i, e, *_: (e, 0, 0)),
            pl.BlockSpec((1, D_MODEL, D_EXPERT), lambda i, e, *_: (e, 0, 0)),
            pl.BlockSpec((1, D_EXPERT, D_MODEL), lambda i, e, *_: (e, 0, 0)),
            pl.BlockSpec((1, D_MODEL), lambda i, e, *_: (0, 0)),
        ],
        out_specs=pl.BlockSpec((bt, D_MODEL), row, pipeline_mode=pl.Buffered(1)),
        scratch_shapes=[
            pltpu.VMEM((2 * bt + MOE_UNROLL, D_MODEL), F32),
            pltpu.VMEM((MOE_ROWS // MOE_UNROLL, MOE_UNROLL, D_MODEL), F32),
            pltpu.VMEM((MOE_ROWS // MOE_UNROLL, MOE_UNROLL, D_MODEL), F32),
        ],
    )
    return pl.pallas_call(
        _moe_kernel,
        grid_spec=grid_spec,
        out_shape=jax.ShapeDtypeStruct((T, D_MODEL), F32),
        compiler_params=_cparams(("parallel", "arbitrary")),
        name="moe",
    )(cnt, off, idx, ws, xt, hres, wg, wu, wd, g_final)


def _rope_tables(T):
    inv = ROPE_THETA ** (-np.arange(0, ATT_QKDIM, 2, dtype=np.float64) / ATT_QKDIM)
    na = -(-T // ROPE_SPLIT)
    ang_a = (np.arange(na, dtype=np.float64) * ROPE_SPLIT)[:, None] * inv[None, :]
    ang_b = np.arange(ROPE_SPLIT, dtype=np.float64)[:, None] * inv[None, :]
    ca, sa, cb, sb = (np.cos(ang_a), np.sin(ang_a), np.cos(ang_b), np.sin(ang_b))

    def combine(f, sign):
        a1, a2, b1, b2 = (jnp.asarray(v, F32) for v in f)
        return a1 * b1 + sign * (a2 * b2)

    rep = lambda v: np.tile(v, (1, LANES // v.shape[1]))
    sgn = np.tile(np.repeat([-1.0, 1.0], ATT_QKDIM // 2), LANES // ATT_QKDIM)
    tok = lambda v: v.reshape(-1, LANES)[:T]
    cos_t = tok(combine((rep(ca)[:, None], rep(sa)[:, None], rep(cb)[None], rep(sb)[None]), -1.0))
    sin_t = tok(combine((rep(sa)[:, None] * sgn, rep(ca)[:, None] * sgn, rep(cb)[None], rep(sb)[None]),
                        1.0))
    ft = lambda v: v.reshape(v.shape[0], -1)[:, :T]
    cos_tt = ft(combine((ca.T[:, :, None], sa.T[:, :, None], cb.T[:, None], sb.T[:, None]), -1.0))
    sin_tt = ft(combine((sa.T[:, :, None], ca.T[:, :, None], cb.T[:, None], sb.T[:, None]), 1.0))
    return cos_t, sin_t, cos_tt, sin_tt


def kernel(x, w_in, w_out, g_mix, lam_params, subln_g, hgrn_gnorm_g, hgrn_lb, g_ffn, w_gr, b_gr,
           w_er, b_er, w_gate, w_up, w_down, g_final):
    B, T, D = x.shape
    x2 = x.reshape(B * T, D)
    l = 0
    w_in_bf = w_in[l].astype(BF16)
    w_out_bf = w_out[l].astype(BF16)
    lp = lam_params[l].astype(F32)
    lam = (jnp.exp(jnp.sum(lp[0] * lp[1])) - jnp.exp(jnp.sum(lp[2] * lp[3])) + LAMBDA_INIT).reshape(1, 1)
    lb = jnp.cumsum(jax.nn.softmax(hgrn_lb.astype(F32), axis=1), axis=1)[:, l]
    w_qv_t = jnp.concatenate([w_in[l][:, 0:ATT_WIDTH], w_in[l][:, 2 * ATT_WIDTH:3 * ATT_WIDTH]],
                             axis=1).T.astype(BF16)
    cos_t, sin_t, cos_tt, sin_tt = _rope_tables(T)
    w_r = jnp.concatenate([jnp.transpose(w_er[l], (1, 0, 2)).reshape(D, N_EXPERTS), w_gr[l],
                           jnp.zeros((D, ROUTER_LANES - N_EXPERTS - N_GROUPS), F32)], axis=1)
    wr_hi = w_r.astype(BF16)
    wr_lo = (w_r - wr_hi.astype(F32)).astype(BF16)
    rbias = jnp.concatenate([b_er[l].reshape(-1), b_gr[l],
                             jnp.zeros((ROUTER_LANES - N_EXPERTS - N_GROUPS,), F32)]).reshape(1, -1)
    wg = w_gate[l].reshape(N_EXPERTS, D, D_EXPERT).astype(BF16)
    wu = w_up[l].reshape(N_EXPERTS, D, D_EXPERT).astype(BF16)
    wd = w_down[l].reshape(N_EXPERTS, D_EXPERT, D).astype(BF16)

    tk = min(ATT_TK, (B * T) // 2)
    qt, k, vt, hgrn_in = _in_proj(x2, g_mix[l].reshape(1, D), w_in_bf, w_qv_t, cos_t, sin_t,
                                  cos_tt, sin_tt, tk)
    att = _diff_attn(lam, qt, k, vt, subln_g[l].reshape(-1, 1))
    o_b = _hgrn_pass(hgrn_in, lb[1:2], backward=True)
    rec = _hgrn_pass(hgrn_in, lb[0:1], backward=False, o_b=o_b,
                     gnorm_g=hgrn_gnorm_g[l].reshape(1, -1))
    hres, xt, route = _out_proj(att, rec, x2, w_out_bf, g_ffn[l].reshape(1, D), wr_hi, wr_lo, rbias)
    bt = min(MOE_BT, B * T)
    lists, meta = _route_sort(route, bt)
    pad = jnp.zeros((MOE_UNROLL,), F32)
    idx = jnp.concatenate([(lists[:, 5] * bt + lists[:, 0] * 32.0 + lists[:, 1]).reshape(-1),
                           pad]).astype(jnp.int32)
    ws = jnp.concatenate([(lists[:, 2] + lists[:, 3] + lists[:, 4]).reshape(-1), pad])
    cnt = meta[:, 0, :N_EXPERTS].astype(jnp.int32).reshape(-1)
    off = meta[:, 1, :N_EXPERTS].astype(jnp.int32).reshape(-1)
    out = _moe(cnt, off, idx, ws, xt, hres, wg, wu, wd, g_final.reshape(1, D), bt)
    return out.reshape(B, T, D)
```

```python
import functools
import math

import numpy as np
import jax
import jax.numpy as jnp
from jax import lax
from jax.experimental import pallas as pl
from jax.experimental.pallas import tpu as pltpu

D_MODEL = 1024
ATT_WIDTH = 512
ATT_HEADS = 4
ATT_VDIM = 128
ATT_QKDIM = 64
HGRN_WIDTH = 512
HGRN_HEADS = 4
HGRN_VDIM = 128
HGRN_EXPAND = 128
HGRN_FDIM = 512
N_GROUPS = 4
EXPERTS_PER_GROUP = 8
N_EXPERTS = N_GROUPS * EXPERTS_PER_GROUP
D_EXPERT = 512
ROPE_THETA = 10000.0
NORM_EPS = 1e-6
SUBLN_EPS = 1e-5
LAMBDA_INIT = 0.8 - 0.6 * math.exp(-0.3 * 0)
LOG2E = math.log2(math.e)
ROPE_SPLIT = 256
IN_COLS = 3 * ATT_WIDTH + 3 * HGRN_FDIM + 2 * HGRN_WIDTH
HGRN_COLS = IN_COLS - 3 * ATT_WIDTH

LANES = 128
VMEM_LIMIT = 56 * 1024 * 1024
SUBLANES = 8
MXU_DIM = 256
HGRN_CHUNK = 64
IN_TM = 512
ATT_TK = 1024
ATT_CB = MXU_DIM
ATT_KT = MXU_DIM
MOE_BT = 2048
MOE_ROWS = 256
MOE_UNROLL = SUBLANES
ATT_VPAD = ATT_VDIM + 2 * SUBLANES
ROUTER_LANES = LANES

BF16 = jnp.bfloat16
F32 = jnp.float32


def _cparams(sem):
    return pltpu.CompilerParams(dimension_semantics=sem, vmem_limit_bytes=VMEM_LIMIT)


def _rot_half_64(x):
    lane = lax.broadcasted_iota(jnp.int32, x.shape, 1)
    fwd = pltpu.roll(x, 32, axis=1)
    bwd = pltpu.roll(x, 96, axis=1)
    return jnp.where((lane & 63) < 32, bwd, fwd)


def _inproj_kernel(x_ref, g_ref, w_ref, wqvt_ref, cos_ref, sin_ref, cost_ref, sint_ref,
                   qt_ref, k_ref, vt_ref, hg_ref):
    x = x_ref[...]
    tm = x.shape[0]
    ms = jnp.mean(x * x, axis=-1, keepdims=True)
    h = (x * lax.rsqrt(ms + NORM_EPS) * g_ref[...]).astype(BF16)
    nt = (((1,), (1,)), ((), ()))
    half = ATT_QKDIM // 2
    ct = cost_ref[...] * (ATT_QKDIM ** -0.5 * LOG2E)
    st = sint_ref[...] * (ATT_QKDIM ** -0.5 * LOG2E)
    for j in range(ATT_WIDTH // LANES):
        a = lax.dot_general(wqvt_ref[j * LANES:(j + 1) * LANES, :], h, nt,
                            preferred_element_type=F32)
        for c in range(LANES // ATT_QKDIM):
            x1 = a[c * ATT_QKDIM:c * ATT_QKDIM + half]
            x2 = a[c * ATT_QKDIM + half:(c + 1) * ATT_QKDIM]
            lo = j * LANES + c * ATT_QKDIM
            qt_ref[lo:lo + half, :] = (x1 * ct - x2 * st).astype(BF16)
            qt_ref[lo + half:lo + ATT_QKDIM, :] = (x1 * st + x2 * ct).astype(BF16)
    ones = jnp.ones((ATT_VPAD - ATT_VDIM, tm), BF16)
    for j in range(ATT_HEADS):
        lo = ATT_WIDTH + j * ATT_VDIM
        vt = lax.dot_general(wqvt_ref[lo:lo + ATT_VDIM, :], h, nt, preferred_element_type=F32)
        vt_ref[0, j * ATT_VPAD:j * ATT_VPAD + ATT_VDIM, :] = vt.astype(BF16)
        vt_ref[0, j * ATT_VPAD + ATT_VDIM:(j + 1) * ATT_VPAD, :] = ones
    cosv = cos_ref[...]
    sinv = sin_ref[...]
    for j in range(ATT_WIDTH // LANES):
        lo = ATT_WIDTH + j * LANES
        a = jnp.dot(h, w_ref[:, lo:lo + LANES], preferred_element_type=F32)
        k_ref[:, j * LANES:(j + 1) * LANES] = (a * cosv + _rot_half_64(a) * sinv).astype(BF16)
    for j in range(HGRN_COLS // 512):
        lo = 3 * ATT_WIDTH + j * 512
        hg_ref[:, j * 512:(j + 1) * 512] = jnp.dot(h, w_ref[:, lo:lo + 512],
                                                   preferred_element_type=F32)


def _in_proj(x2, g_mix, w_in_bf, w_qv_t, cos_t, sin_t, cos_tt, sin_tt, tk):
    T = x2.shape[0]
    tm = IN_TM
    per = tk // tm
    return pl.pallas_call(
        _inproj_kernel,
        grid=(T // tm,),
        in_specs=[
            pl.BlockSpec((tm, D_MODEL), lambda i: (i, 0)),
            pl.BlockSpec((1, D_MODEL), lambda i: (0, 0)),
            pl.BlockSpec((D_MODEL, IN_COLS), lambda i: (0, 0)),
            pl.BlockSpec((2 * ATT_WIDTH, D_MODEL), lambda i: (0, 0)),
            pl.BlockSpec((tm, LANES), lambda i: (i, 0)),
            pl.BlockSpec((tm, LANES), lambda i: (i, 0)),
            pl.BlockSpec((ATT_QKDIM // 2, tm), lambda i: (0, i)),
            pl.BlockSpec((ATT_QKDIM // 2, tm), lambda i: (0, i)),
        ],
        out_specs=[
            pl.BlockSpec((ATT_WIDTH, tm), lambda i: (0, i)),
            pl.BlockSpec((tm, ATT_WIDTH), lambda i: (i, 0)),
            pl.BlockSpec((1, ATT_HEADS * ATT_VPAD, tm), lambda i: (i // per, 0, i % per)),
            pl.BlockSpec((tm, HGRN_COLS), lambda i: (i, 0)),
        ],
        out_shape=[
            jax.ShapeDtypeStruct((ATT_WIDTH, T), BF16),
            jax.ShapeDtypeStruct((T, ATT_WIDTH), BF16),
            jax.ShapeDtypeStruct((T // tk, ATT_HEADS * ATT_VPAD, tk), BF16),
            jax.ShapeDtypeStruct((T, HGRN_COLS), F32),
        ],
        compiler_params=_cparams(("parallel",)),
        name="in_proj",
    )(x2, g_mix, w_in_bf, w_qv_t, cos_t, sin_t, cos_tt, sin_tt)


def _attn_kernel(lam_ref, qt_ref, qtn_ref, k_ref, vt_ref, g_ref, o_ref,
                 qw_ref, qwn_ref, sa_ref, sb_ref, acc_ref, st_ref):
    tq = qt_ref.shape[1]
    nchunk, _, tk = vt_ref.shape
    ncb = 2 * tq // ATT_CB

    def latch(dst_ref, src_ref):
        qt = src_ref[...].astype(F32)
        row = lax.broadcasted_iota(jnp.int32, qt.shape, 0)
        dst_ref[:, 0:tq] = jnp.where(row < ATT_QKDIM, qt, 0.0).astype(BF16)
        dst_ref[:, tq:] = jnp.where(row >= ATT_QKDIM, qt, 0.0).astype(BF16)

    latch(qw_ref, qt_ref)
    latch(qwn_ref, qtn_ref)
    acc_ref[...] = jnp.zeros_like(acc_ref)

    def stage(j_acc, s_acc, m_cur, alpha, j_sc, s_sc, m_old, q_ref):
        off = j_sc * tk
        m_new, al_new = [], []
        for cb in range(ncb):
            cols = slice(cb * ATT_CB, (cb + 1) * ATT_CB)
            part, cmax = None, None
            for t in range(tk // ATT_KT):
                rows = slice(t * ATT_KT, (t + 1) * ATT_KT)
                kc = k_ref[pl.ds(pl.multiple_of(off + t * ATT_KT, ATT_KT), ATT_KT), :]
                s = jnp.dot(kc, q_ref[:, cols], preferred_element_type=F32)
                s_sc[rows, cols] = s
                c = jnp.max(s, axis=0, keepdims=True)
                cmax = c if cmax is None else jnp.maximum(cmax, c)
                if s_acc is not None:
                    p = jnp.exp2(s_acc[rows, cols] - m_cur[cb]).astype(BF16)
                    d = jnp.dot(vt_ref[j_acc, :, rows], p, preferred_element_type=F32)
                    part = d if part is None else part + d
            if s_acc is not None:
                acc_ref[:, cols] = alpha[cb] * acc_ref[:, cols] + part
            mn = jnp.maximum(m_old[cb], cmax)
            m_new.append(mn)
            al_new.append(jnp.exp2(m_old[cb] - mn))
        return tuple(m_new), tuple(al_new)

    def save(st):
        for cb in range(ncb):
            cols = slice(cb * ATT_CB, (cb + 1) * ATT_CB)
            st_ref[0:1, cols] = st[0][cb]
            st_ref[1:2, cols] = st[1][cb]

    m0 = tuple(jnp.full((1, ATT_CB), -jnp.inf, F32) for _ in range(ncb))

    @pl.when(pl.program_id(1) == 0)
    def _():
        save(stage(0, None, None, None, 0, sa_ref, m0, qw_ref))

    state = (tuple(st_ref[0:1, cb * ATT_CB:(cb + 1) * ATT_CB] for cb in range(ncb)),
             tuple(st_ref[1:2, cb * ATT_CB:(cb + 1) * ATT_CB] for cb in range(ncb)))

    per_trip = 4 if nchunk % 4 == 0 else 2
    ntrip = nchunk // per_trip
    bufs = (sa_ref, sb_ref)

    def trip(tr, st, last):
        m_cur, al_cur = st
        for u in range(per_trip):
            j = per_trip * tr + u
            if last and u == per_trip - 1:
                m_cur, al_cur = stage(j, bufs[u % 2], m_cur, al_cur, 0, bufs[(u + 1) % 2], m0,
                                      qwn_ref)
            else:
                m_cur, al_cur = stage(j, bufs[u % 2], m_cur, al_cur, j + 1, bufs[(u + 1) % 2],
                                      m_cur, qw_ref)
        return m_cur, al_cur

    state = lax.fori_loop(0, ntrip - 1, lambda tr, st: trip(tr, st, False), state)
    save(trip(ntrip - 1, state, True))

    lam = lam_ref[0, 0]
    acc = acc_ref[...]
    o1 = acc[0:ATT_VDIM, 0:tq] / acc[ATT_VDIM:ATT_VDIM + 1, 0:tq]
    o2 = acc[0:ATT_VDIM, tq:] / acc[ATT_VDIM:ATT_VDIM + 1, tq:]
    o = o1 - lam * o2
    ms = jnp.mean(o * o, axis=0, keepdims=True)
    o = o * lax.rsqrt(ms + SUBLN_EPS) * g_ref[...] * (1.0 - LAMBDA_INIT)
    o_ref[...] = o.T.astype(o_ref.dtype)


def _diff_attn(lam, qt, k, vt, subln_g, tq=1024):
    T = k.shape[0]
    tq = min(tq, T)
    nq = T // tq
    nchunk, _, tk = vt.shape
    return pl.pallas_call(
        _attn_kernel,
        grid=(ATT_HEADS, nq),
        in_specs=[
            pl.BlockSpec(memory_space=pltpu.SMEM),
            pl.BlockSpec((LANES, tq), lambda h, i: (h, i)),
            pl.BlockSpec((LANES, tq), lambda h, i: (h, jnp.minimum(i + 1, nq - 1))),
            pl.BlockSpec((T, LANES), lambda h, i: (0, h)),
            pl.BlockSpec((nchunk, ATT_VPAD, tk), lambda h, i: (0, h, 0)),
            pl.BlockSpec((ATT_VDIM, 1), lambda h, i: (0, 0)),
        ],
        out_specs=pl.BlockSpec((tq, LANES), lambda h, i: (i, h)),
        out_shape=jax.ShapeDtypeStruct((T, ATT_WIDTH), BF16),
        scratch_shapes=[
            pltpu.VMEM((LANES, 2 * tq), BF16),
            pltpu.VMEM((LANES, 2 * tq), BF16),
            pltpu.VMEM((tk, 2 * tq), F32),
            pltpu.VMEM((tk, 2 * tq), F32),
            pltpu.VMEM((ATT_VPAD, 2 * tq), F32),
            pltpu.VMEM((8, 2 * tq), F32),
        ],
        compiler_params=_cparams(("arbitrary", "arbitrary")),
        name="diff_attn",
    )(lam, qt, qt, k, vt, subln_g)


def _hgrn_consts(C, backward):
    halves = []
    h = C // 2
    while h >= 1:
        halves.append(h)
        h //= 2
    E = np.zeros((2 + len(halves), C, C), np.float32)
    M = np.zeros((len(halves) + 1, C, C), np.float32)
    idx = np.arange(C)
    for t in range(C):
        if not backward:
            E[0, t, idx <= t] = 1.0
            E[1, t, idx > t] = 1.0
        else:
            E[0, t, idx >= t] = 1.0
            E[1, t, idx < t] = 1.0
    for li, h in enumerate(halves):
        for t in range(C):
            mid = (t // (2 * h)) * 2 * h + h
            upper = t >= mid
            if not backward:
                if upper:
                    E[2 + li, t, (idx >= mid) & (idx <= t)] = 1.0
                else:
                    E[2 + li, t, (idx > t) & (idx <= mid - 1)] = 1.0
            else:
                if not upper:
                    E[2 + li, t, (idx >= t) & (idx <= mid - 1)] = 1.0
                else:
                    E[2 + li, t, (idx >= mid) & (idx <= t - 1)] = 1.0
            for s in range(C):
                same = (s // (2 * h)) == (t // (2 * h))
                s_upper = s >= mid
                if same and ((not backward and upper and not s_upper)
                             or (backward and not upper and s_upper)):
                    M[li, t, s] = 1.0
    M[-1] = np.eye(C, dtype=np.float32)
    E = np.concatenate([E.reshape(-1, C), np.ones((8, C), np.float32)], axis=0)
    return E, M


def _split3_rows(g):
    hi = g.astype(BF16)
    r1 = g - hi.astype(F32)
    mid = r1.astype(BF16)
    lo = (r1 - mid.astype(F32)).astype(BF16)
    return jnp.concatenate([hi, mid, lo], axis=0)


def _hgrn_kernel(*refs, backward, nchunk, final):
    if final:
        (hq_ref, hf_ref, hi_ref, lb_ref, e_ref, m_ref, ob_ref, hgate_ref, gn_ref,
         o_ref, st_ref) = refs
    else:
        hq_ref, hf_ref, hi_ref, lb_ref, e_ref, m_ref, o_ref, st_ref = refs
    C = HGRN_CHUNK
    nlev = m_ref.shape[0] - 1
    nt = (((1,), (1,)), ((), ()))

    @pl.when(pl.program_id(0) == 0)
    def _():
        st_ref[...] = jnp.zeros_like(st_ref)

    lbd = lb_ref[...]
    emat = e_ref[...]
    order = range(nchunk - 1, -1, -1) if backward else range(nchunk)
    for c in order:
        rows = pl.ds(c * C, C)
        z = hf_ref[rows, :]
        hq = hq_ref[rows, :]
        f = lbd + (1.0 - lbd) * jax.nn.sigmoid(z)
        kk = (1.0 - lbd) * jax.nn.sigmoid(-z)
        g = jnp.log(f)
        q = hq * jax.nn.sigmoid(hq)
        vb = hi_ref[rows, :].astype(BF16)
        ex = jnp.dot(emat, _split3_rows(g), preferred_element_type=F32)
        ee = jnp.exp(ex)
        qi = (q * ee[0:C]).astype(BF16)
        ki = (kk * ee[C:2 * C]).astype(BF16)
        dec = ee[(2 + nlev) * C:(2 + nlev) * C + 1]
        qb = q.astype(BF16)
        kb = kk.astype(BF16)
        qlev = [(q * ee[(2 + l) * C:(3 + l) * C]).astype(BF16) for l in range(nlev)]
        klev = [(kk * ee[(2 + l) * C:(3 + l) * C]).astype(BF16) for l in range(nlev)]
        outs = []
        for hd in range(HGRN_HEADS):
            ln = slice(hd * LANES, (hd + 1) * LANES)
            a = m_ref[nlev] * lax.dot_general(qb[:, ln], kb[:, ln], nt, preferred_element_type=F32)
            for l in range(nlev):
                a = a + m_ref[l] * lax.dot_general(qlev[l][:, ln], klev[l][:, ln], nt,
                                                   preferred_element_type=F32)
            st = st_ref[hd]
            o = jnp.dot(a.astype(BF16), vb[:, ln], preferred_element_type=F32)
            o = o + lax.dot_general(qi[:, ln], st.astype(BF16), nt, preferred_element_type=F32)
            vt = hi_ref[rows, ln].T.astype(BF16)
            st_ref[hd] = st * dec[:, ln] + jnp.dot(vt, ki[:, ln], preferred_element_type=F32)
            outs.append(o)
        o_all = jnp.concatenate(outs, axis=1)
        if final:
            o_all = o_all + ob_ref[rows, :]
            res = []
            for hd in range(HGRN_HEADS):
                ln = slice(hd * LANES, (hd + 1) * LANES)
                oh = o_all[:, ln]
                ms = jnp.mean(oh * oh, axis=-1, keepdims=True)
                res.append(oh * lax.rsqrt(ms + NORM_EPS) * gn_ref[...])
            hg = hgate_ref[rows, :]
            o_all = jnp.concatenate(res, axis=1) * (hg * jax.nn.sigmoid(hg))
        o_ref[rows, :] = o_all.astype(o_ref.dtype)


def _hgrn_pass(hgrn_in, lb_row, backward, o_b=None, gnorm_g=None, tb=512):
    T = hgrn_in.shape[0]
    nblk = T // tb
    final = o_b is not None
    E, M = _hgrn_consts(HGRN_CHUNK, backward)
    e_bf = jnp.asarray(np.concatenate([E, E, E], axis=1), BF16)
    m_f = jnp.asarray(M, F32)
    blk = (lambda i: nblk - 1 - i) if backward else (lambda i: i)
    f_part = 2 if backward else 1
    in_specs = [
        pl.BlockSpec((tb, 512), lambda i: (blk(i), 0)),
        pl.BlockSpec((tb, 512), lambda i: (blk(i), f_part)),
        pl.BlockSpec((tb, 512), lambda i: (blk(i), 3)),
        pl.BlockSpec((1, 512), lambda i: (0, 0)),
        pl.BlockSpec(e_bf.shape, lambda i: (0, 0)),
        pl.BlockSpec(m_f.shape, lambda i: (0, 0, 0)),
    ]
    args = [hgrn_in, hgrn_in, hgrn_in, lb_row, e_bf, m_f]
    if final:
        in_specs += [
            pl.BlockSpec((tb, 512), lambda i: (blk(i), 0)),
            pl.BlockSpec((tb, 512), lambda i: (blk(i), 4)),
            pl.BlockSpec((1, HGRN_VDIM), lambda i: (0, 0)),
        ]
        args += [o_b, hgrn_in, gnorm_g]
    return pl.pallas_call(
        functools.partial(_hgrn_kernel, backward=backward, nchunk=tb // HGRN_CHUNK, final=final),
        grid=(nblk,),
        in_specs=in_specs,
        out_specs=pl.BlockSpec((tb, 512), lambda i: (blk(i), 0)),
        out_shape=jax.ShapeDtypeStruct((T, HGRN_WIDTH), BF16 if final else F32),
        scratch_shapes=[pltpu.VMEM((HGRN_HEADS, HGRN_VDIM, HGRN_EXPAND), F32)],
        compiler_params=_cparams(("arbitrary",)),
        name="hgrn_bwd" if backward else "hgrn_fwd",
    )(*args)


def _outproj_kernel(att_ref, rec_ref, x_ref, wo_ref, g_ref, wr_hi_ref, wr_lo_ref, rb_ref,
                    hres_ref, xt_ref, comb_ref):
    mix = (jnp.dot(att_ref[...], wo_ref[0:ATT_WIDTH, :], preferred_element_type=F32)
           + jnp.dot(rec_ref[...], wo_ref[ATT_WIDTH:, :], preferred_element_type=F32))
    hres = x_ref[...] + mix
    hres_ref[...] = hres
    ms = jnp.mean(hres * hres, axis=-1, keepdims=True)
    xt = hres * lax.rsqrt(ms + NORM_EPS) * g_ref[...]
    xt_ref[...] = xt
    xt_hi = xt.astype(BF16)
    xt_lo = (xt - xt_hi.astype(F32)).astype(BF16)
    logits = (jnp.dot(xt_hi, wr_hi_ref[...], preferred_element_type=F32)
              + jnp.dot(xt_hi, wr_lo_ref[...], preferred_element_type=F32)
              + jnp.dot(xt_lo, wr_hi_ref[...], preferred_element_type=F32)) + rb_ref[...]
    lane = lax.broadcasted_iota(jnp.int32, logits.shape, 1)
    neg = -jnp.inf
    big = jnp.int32(1 << 20)
    gmask = (lane >= N_EXPERTS) & (lane < N_EXPERTS + N_GROUPS)
    glog = jnp.where(gmask, logits, neg)
    gmax = jnp.max(glog, axis=-1, keepdims=True)
    gsum = jnp.sum(jnp.exp(glog - gmax), axis=-1, keepdims=True)
    p_g = 1.0 / gsum
    g_idx = jnp.min(jnp.where(glog == gmax, lane, big), axis=-1, keepdims=True) - N_EXPERTS
    emask = (lane < N_EXPERTS) & ((lane >> 3) == g_idx)
    elog = jnp.where(emask, logits, neg)
    e1 = jnp.max(elog, axis=-1, keepdims=True)
    i1 = jnp.min(jnp.where(elog == e1, lane, big), axis=-1, keepdims=True)
    elog2 = jnp.where(lane == i1, neg, elog)
    e2 = jnp.max(elog2, axis=-1, keepdims=True)
    i2 = jnp.min(jnp.where(elog2 == e2, lane, big), axis=-1, keepdims=True)
    r = jnp.exp(e2 - e1)
    w1 = p_g / (1.0 + r)
    w2 = p_g * r / (1.0 + r)
    comb_ref[...] = jnp.where(lane == 0, i1.astype(F32), jnp.where(
        lane == 1, i2.astype(F32), jnp.where(lane == 2, w1, jnp.where(lane == 3, w2, 0.0))))


def _out_proj(att, rec, x2, w_out_bf, g_ffn, wr_hi, wr_lo, rbias, tm=512):
    T = x2.shape[0]
    row = lambda i: (i, 0)
    fixed = lambda i: (0, 0)
    return pl.pallas_call(
        _outproj_kernel,
        grid=(T // tm,),
        in_specs=[
            pl.BlockSpec((tm, ATT_WIDTH), row),
            pl.BlockSpec((tm, HGRN_WIDTH), row),
            pl.BlockSpec((tm, D_MODEL), row),
            pl.BlockSpec((D_MODEL, D_MODEL), fixed),
            pl.BlockSpec((1, D_MODEL), fixed),
            pl.BlockSpec((D_MODEL, ROUTER_LANES), fixed),
            pl.BlockSpec((D_MODEL, ROUTER_LANES), fixed),
            pl.BlockSpec((1, ROUTER_LANES), fixed),
        ],
        out_specs=[
            pl.BlockSpec((tm, D_MODEL), row),
            pl.BlockSpec((tm, D_MODEL), row),
            pl.BlockSpec((tm, ROUTER_LANES), row),
        ],
        out_shape=[
            jax.ShapeDtypeStruct((T, D_MODEL), F32),
            jax.ShapeDtypeStruct((T, D_MODEL), F32),
            jax.ShapeDtypeStruct((T, ROUTER_LANES), F32),
        ],
        compiler_params=_cparams(("parallel",)),
        name="out_proj_router",
    )(att, rec, x2, w_out_bf, g_ffn, wr_hi, wr_lo, rbias)


def _split3_f32(w):
    hi = w.astype(BF16).astype(F32)
    r1 = w - hi
    mid = r1.astype(BF16).astype(F32)
    lo = (r1 - mid).astype(BF16).astype(F32)
    return hi, mid, lo


def _route_sort_kernel(rt_ref, ltri_ref, utri_ref, lists_ref, meta_ref):
    rt = rt_ref[...]
    bt = rt.shape[0]
    lane = lax.broadcasted_iota(jnp.int32, rt.shape, 1)
    lane_f = lane.astype(F32)
    oh1 = lane_f == rt[:, 0:1]
    oh2 = lane_f == rt[:, 1:2]
    oh = jnp.where(oh1, 1.0, jnp.where(oh2, 1.0, 0.0))
    oh_b = oh.astype(BF16)
    rank = jnp.dot(ltri_ref[...], oh_b, preferred_element_type=F32)
    below = jnp.dot(oh_b, utri_ref[...], preferred_element_type=F32)
    offs = jnp.sum(below, axis=0, keepdims=True)
    cnt = jnp.sum(oh, axis=0, keepdims=True)
    posmat = rank + offs
    pos1 = jnp.sum(jnp.where(oh1, posmat, 0.0), axis=1, keepdims=True)
    pos2 = jnp.sum(jnp.where(oh2, posmat, 0.0), axis=1, keepdims=True)
    tok = lax.broadcasted_iota(jnp.int32, rt.shape, 0)
    tok_hi = (tok >> 5).astype(F32)
    tok_lo = (tok & 31).astype(F32)

    def record(w, slot):
        hi, mid, lo = _split3_f32(w)
        d = jnp.where(lane == 0, tok_hi, jnp.where(lane == 1, tok_lo, jnp.where(
            lane == 2, hi, jnp.where(lane == 3, mid, jnp.where(lane == 4, lo, jnp.where(
                lane == 5, slot, 0.0))))))
        return d.T.astype(BF16)

    d1 = record(rt[:, 2:3], 0.0)
    d2 = record(rt[:, 3:4], 1.0)
    cw = 512
    for c in range(2 * bt // cw):
        colp = (lax.broadcasted_iota(jnp.int32, (bt, cw), 1) + c * cw).astype(F32)
        o1 = jnp.where(colp == pos1, 1.0, 0.0).astype(BF16)
        o2 = jnp.where(colp == pos2, 1.0, 0.0).astype(BF16)
        srt = (jnp.dot(d1, o1, preferred_element_type=F32)
               + jnp.dot(d2, o2, preferred_element_type=F32))
        lists_ref[0, :, c * cw:(c + 1) * cw] = srt[0:8]
    row = lax.broadcasted_iota(jnp.int32, (8, LANES), 0)
    meta_ref[0] = jnp.where(row == 0, cnt, jnp.where(row == 1, offs, 0.0))


def _route_sort(route, bt):
    T = route.shape[0]
    nb = T // bt
    ltri = jnp.asarray(np.tril(np.ones((bt, bt), np.float32), -1), BF16)
    utri = jnp.asarray(np.triu(np.ones((LANES, LANES), np.float32), 1), BF16)
    return pl.pallas_call(
        _route_sort_kernel,
        grid=(nb,),
        in_specs=[
            pl.BlockSpec((bt, ROUTER_LANES), lambda b: (b, 0)),
            pl.BlockSpec((bt, bt), lambda b: (0, 0)),
            pl.BlockSpec((LANES, LANES), lambda b: (0, 0)),
        ],
        out_specs=[
            pl.BlockSpec((1, 8, 2 * bt), lambda b: (b, 0, 0)),
            pl.BlockSpec((1, 8, LANES), lambda b: (b, 0, 0)),
        ],
        out_shape=[
            jax.ShapeDtypeStruct((nb, 8, 2 * bt), F32),
            jax.ShapeDtypeStruct((nb, 8, LANES), F32),
        ],
        compiler_params=_cparams(("parallel",)),
        name="route_sort",
    )(route, ltri, utri)


def _moe_kernel(cnt_ref, off_ref, idx_ref, ws_ref, xt_ref, hres_ref, wg_ref, wu_ref, wd_ref, gf_ref,
                o_ref, yb_ref, xs_ref, y_ref):
    b = pl.program_id(0)
    e = pl.program_id(1)
    bt = xt_ref.shape[0]
    rows = xs_ref.shape[0] * xs_ref.shape[1]

    @pl.when(e == 0)
    def _():
        xs_ref[...] = jnp.zeros_like(xs_ref)

    n = cnt_ref[b * N_EXPERTS + e]
    base = b * 2 * bt + off_ref[b * N_EXPERTS + e]

    def tile(r, carry):
        r0 = base + r * rows
        m = jnp.minimum(n - r * rows, rows)
        trips = (m + MOE_UNROLL - 1) // MOE_UNROLL

        def gather(i, c):
            for k in range(MOE_UNROLL):
                t = idx_ref[r0 + i * MOE_UNROLL + k] & (bt - 1)
                xs_ref[i, pl.ds(k, 1), :] = xt_ref[pl.ds(t, 1), :]
            return c

        lax.fori_loop(0, trips, gather, 0)
        xs = xs_ref[...].reshape(rows, D_MODEL).astype(BF16)
        a = jnp.dot(xs, wg_ref[0], preferred_element_type=F32)
        u = jnp.dot(xs, wu_ref[0], preferred_element_type=F32)
        hid = (a * jax.nn.sigmoid(a)) * u
        y_ref[...] = jnp.dot(hid.astype(BF16), wd_ref[0],
                             preferred_element_type=F32).reshape(y_ref.shape)

        def scatter(i, c):
            for k in range(MOE_UNROLL):
                j = i * MOE_UNROLL + k
                dst = jnp.where(j < m, idx_ref[r0 + j], 2 * bt + k)
                yb_ref[pl.ds(dst, 1), :] = ws_ref[r0 + j] * y_ref[i, pl.ds(k, 1), :]
            return c

        lax.fori_loop(0, trips, scatter, 0)
        return carry

    lax.fori_loop(0, (n + rows - 1) // rows, tile, 0)

    @pl.when(e == pl.num_programs(1) - 1)
    def _():
        y = hres_ref[...] + yb_ref[0:bt, :] + yb_ref[bt:2 * bt, :]
        ms = jnp.mean(y * y, axis=-1, keepdims=True)
        o_ref[...] = y * lax.rsqrt(ms + NORM_EPS) * gf_ref[...]


def _moe(cnt, off, idx, ws, xt, hres, wg, wu, wd, g_final, bt):
    T = xt.shape[0]
    row = lambda i, e, *_: (i, 0)
    grid_spec = pltpu.PrefetchScalarGridSpec(
        num_scalar_prefetch=4,
        grid=(T // bt, N_EXPERTS),
        in_specs=[
            pl.BlockSpec((bt, D_MODEL), row, pipeline_mode=pl.Buffered(1)),
            pl.BlockSpec((bt, D_MODEL), row, pipeline_mode=pl.Buffered(1)),
            pl.BlockSpec((1, D_MODEL, D_EXPERT), lambda i, e, *_: (e, 0, 0)),
            pl.BlockSpec((1, D_MODEL, D_EXPERT), lambda i, e, *_: (e, 0, 0)),
            pl.BlockSpec((1, D_EXPERT, D_MODEL), lambda i, e, *_: (e, 0, 0)),
            pl.BlockSpec((1, D_MODEL), lambda i, e, *_: (0, 0)),
        ],
        out_specs=pl.BlockSpec((bt, D_MODEL), row, pipeline_mode=pl.Buffered(1)),
        scratch_shapes=[
            pltpu.VMEM((2 * bt + MOE_UNROLL, D_MODEL), F32),
            pltpu.VMEM((MOE_ROWS // MOE_UNROLL, MOE_UNROLL, D_MODEL), F32),
            pltpu.VMEM((MOE_ROWS // MOE_UNROLL, MOE_UNROLL, D_MODEL), F32),
        ],
    )
    return pl.pallas_call(
        _moe_kernel,
        grid_spec=grid_spec,
        out_shape=jax.ShapeDtypeStruct((T, D_MODEL), F32),
        compiler_params=_cparams(("parallel", "arbitrary")),
        name="moe",
    )(cnt, off, idx, ws, xt, hres, wg, wu, wd, g_final)


def _rope_tables(T):
    inv = ROPE_THETA ** (-np.arange(0, ATT_QKDIM, 2, dtype=np.float64) / ATT_QKDIM)
    na = -(-T // ROPE_SPLIT)
    ang_a = (np.arange(na, dtype=np.float64) * ROPE_SPLIT)[:, None] * inv[None, :]
    ang_b = np.arange(ROPE_SPLIT, dtype=np.float64)[:, None] * inv[None, :]
    ca, sa, cb, sb = (np.cos(ang_a), np.sin(ang_a), np.cos(ang_b), np.sin(ang_b))

    def combine(f, sign):
        a1, a2, b1, b2 = (jnp.asarray(v, F32) for v in f)
        return a1 * b1 + sign * (a2 * b2)

    rep = lambda v: np.tile(v, (1, LANES // v.shape[1]))
    sgn = np.tile(np.repeat([-1.0, 1.0], ATT_QKDIM // 2), LANES // ATT_QKDIM)
    tok = lambda v: v.reshape(-1, LANES)[:T]
    cos_t = tok(combine((rep(ca)[:, None], rep(sa)[:, None], rep(cb)[None], rep(sb)[None]), -1.0))
    sin_t = tok(combine((rep(sa)[:, None] * sgn, rep(ca)[:, None] * sgn, rep(cb)[None], rep(sb)[None]),
                        1.0))
    ft = lambda v: v.reshape(v.shape[0], -1)[:, :T]
    cos_tt = ft(combine((ca.T[:, :, None], sa.T[:, :, None], cb.T[:, None], sb.T[:, None]), -1.0))
    sin_tt = ft(combine((sa.T[:, :, None], ca.T[:, :, None], cb.T[:, None], sb.T[:, None]), 1.0))
    return cos_t, sin_t, cos_tt, sin_tt


def kernel(x, w_in, w_out, g_mix, lam_params, subln_g, hgrn_gnorm_g, hgrn_lb, g_ffn, w_gr, b_gr,
           w_er, b_er, w_gate, w_up, w_down, g_final):
    B, T, D = x.shape
    x2 = x.reshape(B * T, D)
    l = 0
    w_in_bf = w_in[l].astype(BF16)
    w_out_bf = w_out[l].astype(BF16)
    lp = lam_params[l].astype(F32)
    lam = (jnp.exp(jnp.sum(lp[0] * lp[1])) - jnp.exp(jnp.sum(lp[2] * lp[3])) + LAMBDA_INIT).reshape(1, 1)
    lb = jnp.cumsum(jax.nn.softmax(hgrn_lb.astype(F32), axis=1), axis=1)[:, l]
    w_qv_t = jnp.concatenate([w_in[l][:, 0:ATT_WIDTH], w_in[l][:, 2 * ATT_WIDTH:3 * ATT_WIDTH]],
                             axis=1).T.astype(BF16)
    cos_t, sin_t, cos_tt, sin_tt = _rope_tables(T)
    w_r = jnp.concatenate([jnp.transpose(w_er[l], (1, 0, 2)).reshape(D, N_EXPERTS), w_gr[l],
                           jnp.zeros((D, ROUTER_LANES - N_EXPERTS - N_GROUPS), F32)], axis=1)
    wr_hi = w_r.astype(BF16)
    wr_lo = (w_r - wr_hi.astype(F32)).astype(BF16)
    rbias = jnp.concatenate([b_er[l].reshape(-1), b_gr[l],
                             jnp.zeros((ROUTER_LANES - N_EXPERTS - N_GROUPS,), F32)]).reshape(1, -1)
    wg = w_gate[l].reshape(N_EXPERTS, D, D_EXPERT).astype(BF16)
    wu = w_up[l].reshape(N_EXPERTS, D, D_EXPERT).astype(BF16)
    wd = w_down[l].reshape(N_EXPERTS, D_EXPERT, D).astype(BF16)

    tk = min(ATT_TK, (B * T) // 2)
    qt, k, vt, hgrn_in = _in_proj(x2, g_mix[l].reshape(1, D), w_in_bf, w_qv_t, cos_t, sin_t,
                                  cos_tt, sin_tt, tk)
    att = _diff_attn(lam, qt, k, vt, subln_g[l].reshape(-1, 1))
    o_b = _hgrn_pass(hgrn_in, lb[1:2], backward=True)
    rec = _hgrn_pass(hgrn_in, lb[0:1], backward=False, o_b=o_b,
                     gnorm_g=hgrn_gnorm_g[l].reshape(1, -1))
    hres, xt, route = _out_proj(att, rec, x2, w_out_bf, g_ffn[l].reshape(1, D), wr_hi, wr_lo, rbias)
    bt = min(MOE_BT, B * T)
    lists, meta = _route_sort(route, bt)
    pad = jnp.zeros((MOE_UNROLL,), F32)
    idx = jnp.concatenate([(lists[:, 5] * bt + lists[:, 0] * 32.0 + lists[:, 1]).reshape(-1),
                           pad]).astype(jnp.int32)
    ws = jnp.concatenate([(lists[:, 2] + lists[:, 3] + lists[:, 4]).reshape(-1), pad])
    cnt = meta[:, 0, :N_EXPERTS].astype(jnp.int32).reshape(-1)
    off = meta[:, 1, :N_EXPERTS].astype(jnp.int32).reshape(-1)
    out = _moe(cnt, off, idx, ws, xt, hres, wg, wu, wd, g_final.reshape(1, D), bt)
    return out.reshape(B, T, D)
```

```python
import functools
import math

import numpy as np
import jax
import jax.numpy as jnp
from jax import lax
from jax.experimental import pallas as pl
from jax.experimental.pallas import tpu as pltpu

D_MODEL = 1024
ATT_WIDTH = 512
ATT_HEADS = 4
ATT_VDIM = 128
ATT_QKDIM = 64
HGRN_WIDTH = 512
HGRN_HEADS = 4
HGRN_VDIM = 128
HGRN_EXPAND = 128
HGRN_FDIM = 512
N_GROUPS = 4
EXPERTS_PER_GROUP = 8
N_EXPERTS = N_GROUPS * EXPERTS_PER_GROUP
D_EXPERT = 512
ROPE_THETA = 10000.0
NORM_EPS = 1e-6
SUBLN_EPS = 1e-5
LAMBDA_INIT = 0.8 - 0.6 * math.exp(-0.3 * 0)
LOG2E = math.log2(math.e)
ROPE_SPLIT = 256
IN_COLS = 3 * ATT_WIDTH + 3 * HGRN_FDIM + 2 * HGRN_WIDTH
HGRN_COLS = IN_COLS - 3 * ATT_WIDTH

LANES = 128
VMEM_LIMIT = 56 * 1024 * 1024
SUBLANES = 8
MXU_DIM = 256
HGRN_CHUNK = 64
IN_TM = 512
ATT_TK = 1024
ATT_CB = MXU_DIM
ATT_KT = MXU_DIM
MOE_BT = 2048
MOE_ROWS = 256
MOE_UNROLL = SUBLANES
ATT_VPAD = ATT_VDIM + 2 * SUBLANES
ROUTER_LANES = LANES

BF16 = jnp.bfloat16
F32 = jnp.float32


def _cparams(sem):
    return pltpu.CompilerParams(dimension_semantics=sem, vmem_limit_bytes=VMEM_LIMIT)


def _rot_half_64(x):
    lane = lax.broadcasted_iota(jnp.int32, x.shape, 1)
    fwd = pltpu.roll(x, 32, axis=1)
    bwd = pltpu.roll(x, 96, axis=1)
    return jnp.where((lane & 63) < 32, bwd, fwd)


def _inproj_kernel(x_ref, g_ref, w_ref, wqvt_ref, cos_ref, sin_ref, cost_ref, sint_ref,
                   qt_ref, k_ref, vt_ref, hg_ref):
    x = x_ref[...]
    tm = x.shape[0]
    ms = jnp.mean(x * x, axis=-1, keepdims=True)
    h = (x * lax.rsqrt(ms + NORM_EPS) * g_ref[...]).astype(BF16)
    nt = (((1,), (1,)), ((), ()))
    half = ATT_QKDIM // 2
    ct = cost_ref[...] * (ATT_QKDIM ** -0.5 * LOG2E)
    st = sint_ref[...] * (ATT_QKDIM ** -0.5 * LOG2E)
    for j in range(ATT_WIDTH // LANES):
        a = lax.dot_general(wqvt_ref[j * LANES:(j + 1) * LANES, :], h, nt,
                            preferred_element_type=F32)
        for c in range(LANES // ATT_QKDIM):
            x1 = a[c * ATT_QKDIM:c * ATT_QKDIM + half]
            x2 = a[c * ATT_QKDIM + half:(c + 1) * ATT_QKDIM]
            lo = j * LANES + c * ATT_QKDIM
            qt_ref[lo:lo + half, :] = (x1 * ct - x2 * st).astype(BF16)
            qt_ref[lo + half:lo + ATT_QKDIM, :] = (x1 * st + x2 * ct).astype(BF16)
    ones = jnp.ones((ATT_VPAD - ATT_VDIM, tm), BF16)
    for j in range(ATT_HEADS):
        lo = ATT_WIDTH + j * ATT_VDIM
        vt = lax.dot_general(wqvt_ref[lo:lo + ATT_VDIM, :], h, nt, preferred_element_type=F32)
        vt_ref[0, j * ATT_VPAD:j * ATT_VPAD + ATT_VDIM, :] = vt.astype(BF16)
        vt_ref[0, j * ATT_VPAD + ATT_VDIM:(j + 1) * ATT_VPAD, :] = ones
    cosv = cos_ref[...]
    sinv = sin_ref[...]
    for j in range(ATT_WIDTH // LANES):
        lo = ATT_WIDTH + j * LANES
        a = jnp.dot(h, w_ref[:, lo:lo + LANES], preferred_element_type=F32)
        k_ref[:, j * LANES:(j + 1) * LANES] = (a * cosv + _rot_half_64(a) * sinv).astype(BF16)
    for j in range(HGRN_COLS // 512):
        lo = 3 * ATT_WIDTH + j * 512
        hg_ref[:, j * 512:(j + 1) * 512] = jnp.dot(h, w_ref[:, lo:lo + 512],
                                                   preferred_element_type=F32)


def _in_proj(x2, g_mix, w_in_bf, w_qv_t, cos_t, sin_t, cos_tt, sin_tt, tk):
    T = x2.shape[0]
    tm = IN_TM
    per = tk // tm
    return pl.pallas_call(
        _inproj_kernel,
        grid=(T // tm,),
        in_specs=[
            pl.BlockSpec((tm, D_MODEL), lambda i: (i, 0)),
            pl.BlockSpec((1, D_MODEL), lambda i: (0, 0)),
            pl.BlockSpec((D_MODEL, IN_COLS), lambda i: (0, 0)),
            pl.BlockSpec((2 * ATT_WIDTH, D_MODEL), lambda i: (0, 0)),
            pl.BlockSpec((tm, LANES), lambda i: (i, 0)),
            pl.BlockSpec((tm, LANES), lambda i: (i, 0)),
            pl.BlockSpec((ATT_QKDIM // 2, tm), lambda i: (0, i)),
            pl.BlockSpec((ATT_QKDIM // 2, tm), lambda i: (0, i)),
        ],
        out_specs=[
            pl.BlockSpec((ATT_WIDTH, tm), lambda i: (0, i)),
            pl.BlockSpec((tm, ATT_WIDTH), lambda i: (i, 0)),
            pl.BlockSpec((1, ATT_HEADS * ATT_VPAD, tm), lambda i: (i // per, 0, i % per)),
            pl.BlockSpec((tm, HGRN_COLS), lambda i: (i, 0)),
        ],
        out_shape=[
            jax.ShapeDtypeStruct((ATT_WIDTH, T), BF16),
            jax.ShapeDtypeStruct((T, ATT_WIDTH), BF16),
            jax.ShapeDtypeStruct((T // tk, ATT_HEADS * ATT_VPAD, tk), BF16),
            jax.ShapeDtypeStruct((T, HGRN_COLS), F32),
        ],
        compiler_params=_cparams(("parallel",)),
        name="in_proj",
    )(x2, g_mix, w_in_bf, w_qv_t, cos_t, sin_t, cos_tt, sin_tt)


def _attn_kernel(lam_ref, qt_ref, qtn_ref, k_ref, vt_ref, g_ref, o_ref,
                 qw_ref, qwn_ref, sa_ref, sb_ref, acc_ref, st_ref):
    tq = qt_ref.shape[1]
    nchunk, _, tk = vt_ref.shape
    ncb = 2 * tq // ATT_CB

    def latch(dst_ref, src_ref):
        qt = src_ref[...].astype(F32)
        row = lax.broadcasted_iota(jnp.int32, qt.shape, 0)
        dst_ref[:, 0:tq] = jnp.where(row < ATT_QKDIM, qt, 0.0).astype(BF16)
        dst_ref[:, tq:] = jnp.where(row >= ATT_QKDIM, qt, 0.0).astype(BF16)

    latch(qw_ref, qt_ref)
    latch(qwn_ref, qtn_ref)
    acc_ref[...] = jnp.zeros_like(acc_ref)

    def stage(j_acc, s_acc, m_cur, alpha, j_sc, s_sc, m_old, q_ref):
        off = j_sc * tk
        m_new, al_new = [], []
        for cb in range(ncb):
            cols = slice(cb * ATT_CB, (cb + 1) * ATT_CB)
            part, cmax = None, None
            for t in range(tk // ATT_KT):
                rows = slice(t * ATT_KT, (t + 1) * ATT_KT)
                kc = k_ref[pl.ds(pl.multiple_of(off + t * ATT_KT, ATT_KT), ATT_KT), :]
                s = jnp.dot(kc, q_ref[:, cols], preferred_element_type=F32)
                s_sc[rows, cols] = s
                c = jnp.max(s, axis=0, keepdims=True)
                cmax = c if cmax is None else jnp.maximum(cmax, c)
                if s_acc is not None:
                    p = jnp.exp2(s_acc[rows, cols] - m_cur[cb]).astype(BF16)
                    d = jnp.dot(vt_ref[j_acc, :, rows], p, preferred_element_type=F32)
                    part = d if part is None else part + d
            if s_acc is not None:
                acc_ref[:, cols] = alpha[cb] * acc_ref[:, cols] + part
            mn = jnp.maximum(m_old[cb], cmax)
            m_new.append(mn)
            al_new.append(jnp.exp2(m_old[cb] - mn))
        return tuple(m_new), tuple(al_new)

    def save(st):
        for cb in range(ncb):
            cols = slice(cb * ATT_CB, (cb + 1) * ATT_CB)
            st_ref[0:1, cols] = st[0][cb]
            st_ref[1:2, cols] = st[1][cb]

    m0 = tuple(jnp.full((1, ATT_CB), -jnp.inf, F32) for _ in range(ncb))

    @pl.when(pl.program_id(1) == 0)
    def _():
        save(stage(0, None, None, None, 0, sa_ref, m0, qw_ref))

    state = (tuple(st_ref[0:1, cb * ATT_CB:(cb + 1) * ATT_CB] for cb in range(ncb)),
             tuple(st_ref[1:2, cb * ATT_CB:(cb + 1) * ATT_CB] for cb in range(ncb)))

    per_trip = 4 if nchunk % 4 == 0 else 2
    ntrip = nchunk // per_trip
    bufs = (sa_ref, sb_ref)

    def trip(tr, st, last):
        m_cur, al_cur = st
        for u in range(per_trip):
            j = per_trip * tr + u
            if last and u == per_trip - 1:
                m_cur, al_cur = stage(j, bufs[u % 2], m_cur, al_cur, 0, bufs[(u + 1) % 2], m0,
                                      qwn_ref)
            else:
                m_cur, al_cur = stage(j, bufs[u % 2], m_cur, al_cur, j + 1, bufs[(u + 1) % 2],
                                      m_cur, qw_ref)
        return m_cur, al_cur

    state = lax.fori_loop(0, ntrip - 1, lambda tr, st: trip(tr, st, False), state)
    save(trip(ntrip - 1, state, True))

    lam = lam_ref[0, 0]
    acc = acc_ref[...]
    o1 = acc[0:ATT_VDIM, 0:tq] / acc[ATT_VDIM:ATT_VDIM + 1, 0:tq]
    o2 = acc[0:ATT_VDIM, tq:] / acc[ATT_VDIM:ATT_VDIM + 1, tq:]
    o = o1 - lam * o2
    ms = jnp.mean(o * o, axis=0, keepdims=True)
    o = o * lax.rsqrt(ms + SUBLN_EPS) * g_ref[...] * (1.0 - LAMBDA_INIT)
    o_ref[...] = o.T.astype(o_ref.dtype)


def _diff_attn(lam, qt, k, vt, subln_g, tq=1024):
    T = k.shape[0]
    tq = min(tq, T)
    nq = T // tq
    nchunk, _, tk = vt.shape
    return pl.pallas_call(
        _attn_kernel,
        grid=(ATT_HEADS, nq),
        in_specs=[
            pl.BlockSpec(memory_space=pltpu.SMEM),
            pl.BlockSpec((LANES, tq), lambda h, i: (h, i)),
            pl.BlockSpec((LANES, tq), lambda h, i: (h, jnp.minimum(i + 1, nq - 1))),
            pl.BlockSpec((T, LANES), lambda h, i: (0, h)),
            pl.BlockSpec((nchunk, ATT_VPAD, tk), lambda h, i: (0, h, 0)),
            pl.BlockSpec((ATT_VDIM, 1), lambda h, i: (0, 0)),
        ],
        out_specs=pl.BlockSpec((tq, LANES), lambda h, i: (i, h)),
        out_shape=jax.ShapeDtypeStruct((T, ATT_WIDTH), BF16),
        scratch_shapes=[
            pltpu.VMEM((LANES, 2 * tq), BF16),
            pltpu.VMEM((LANES, 2 * tq), BF16),
            pltpu.VMEM((tk, 2 * tq), F32),
            pltpu.VMEM((tk, 2 * tq), F32),
            pltpu.VMEM((ATT_VPAD, 2 * tq), F32),
            pltpu.VMEM((8, 2 * tq), F32),
        ],
        compiler_params=_cparams(("arbitrary", "arbitrary")),
        name="diff_attn",
    )(lam, qt, qt, k, vt, subln_g)


def _hgrn_consts(C, backward):
    halves = []
    h = C // 2
    while h >= 1:
        halves.append(h)
        h //= 2
    E = np.zeros((2 + len(halves), C, C), np.float32)
    M = np.zeros((len(halves) + 1, C, C), np.float32)
    idx = np.arange(C)
    for t in range(C):
        if not backward:
            E[0, t, idx <= t] = 1.0
            E[1, t, idx > t] = 1.0
        else:
            E[0, t, idx >= t] = 1.0
            E[1, t, idx < t] = 1.0
    for li, h in enumerate(halves):
        for t in range(C):
            mid = (t // (2 * h)) * 2 * h + h
            upper = t >= mid
            if not backward:
                if upper:
                    E[2 + li, t, (idx >= mid) & (idx <= t)] = 1.0
                else:
                    E[2 + li, t, (idx > t) & (idx <= mid - 1)] = 1.0
            else:
                if not upper:
                    E[2 + li, t, (idx >= t) & (idx <= mid - 1)] = 1.0
                else:
                    E[2 + li, t, (idx >= mid) & (idx <= t - 1)] = 1.0
            for s in range(C):
                same = (s // (2 * h)) == (t // (2 * h))
                s_upper = s >= mid
                if same and ((not backward and upper and not s_upper)
                             or (backward and not upper and s_upper)):
                    M[li, t, s] = 1.0
    M[-1] = np.eye(C, dtype=np.float32)
    E = np.concatenate([E.reshape(-1, C), np.ones((8, C), np.float32)], axis=0)
    return E, M


def _split3_rows(g):
    hi = g.astype(BF16)
    r1 = g - hi.astype(F32)
    mid = r1.astype(BF16)
    lo = (r1 - mid.astype(F32)).astype(BF16)
    return jnp.concatenate([hi, mid, lo], axis=0)


def _hgrn_kernel(*refs, backward, nchunk, final):
    if final:
        (hq_ref, hf_ref, hi_ref, lb_ref, e_ref, m_ref, ob_ref, hgate_ref, gn_ref,
         o_ref, st_ref) = refs
    else:
        hq_ref, hf_ref, hi_ref, lb_ref, e_ref, m_ref, o_ref, st_ref = refs
    C = HGRN_CHUNK
    nlev = m_ref.shape[0] - 1
    nt = (((1,), (1,)), ((), ()))

    @pl.when(pl.program_id(0) == 0)
    def _():
        st_ref[...] = jnp.zeros_like(st_ref)

    lbd = lb_ref[...]
    emat = e_ref[...]
    order = range(nchunk - 1, -1, -1) if backward else range(nchunk)
    for c in order:
        rows = pl.ds(c * C, C)
        z = hf_ref[rows, :]
        hq = hq_ref[rows, :]
        f = lbd + (1.0 - lbd) * jax.nn.sigmoid(z)
        kk = (1.0 - lbd) * jax.nn.sigmoid(-z)
        g = jnp.log(f)
        q = hq * jax.nn.sigmoid(hq)
        vb = hi_ref[rows, :].astype(BF16)
        ex = jnp.dot(emat, _split3_rows(g), preferred_element_type=F32)
        ee = jnp.exp(ex)
        qi = (q * ee[0:C]).astype(BF16)
        ki = (kk * ee[C:2 * C]).astype(BF16)
        dec = ee[(2 + nlev) * C:(2 + nlev) * C + 1]
        qb = q.astype(BF16)
        kb = kk.astype(BF16)
        qlev = [(q * ee[(2 + l) * C:(3 + l) * C]).astype(BF16) for l in range(nlev)]
        klev = [(kk * ee[(2 + l) * C:(3 + l) * C]).astype(BF16) for l in range(nlev)]
        outs = []
        for hd in range(HGRN_HEADS):
            ln = slice(hd * LANES, (hd + 1) * LANES)
            a = m_ref[nlev] * lax.dot_general(qb[:, ln], kb[:, ln], nt, preferred_element_type=F32)
            for l in range(nlev):
                a = a + m_ref[l] * lax.dot_general(qlev[l][:, ln], klev[l][:, ln], nt,
                                                   preferred_element_type=F32)
            st = st_ref[hd]
            o = jnp.dot(a.astype(BF16), vb[:, ln], preferred_element_type=F32)
            o = o + lax.dot_general(qi[:, ln], st.astype(BF16), nt, preferred_element_type=F32)
            vt = hi_ref[rows, ln].T.astype(BF16)
            st_ref[hd] = st * dec[:, ln] + jnp.dot(vt, ki[:, ln], preferred_element_type=F32)
            outs.append(o)
        o_all = jnp.concatenate(outs, axis=1)
        if final:
            o_all = o_all + ob_ref[rows, :]
            res = []
            for hd in range(HGRN_HEADS):
                ln = slice(hd * LANES, (hd + 1) * LANES)
                oh = o_all[:, ln]
                ms = jnp.mean(oh * oh, axis=-1, keepdims=True)
                res.append(oh * lax.rsqrt(ms + NORM_EPS) * gn_ref[...])
            hg = hgate_ref[rows, :]
            o_all = jnp.concatenate(res, axis=1) * (hg * jax.nn.sigmoid(hg))
        o_ref[rows, :] = o_all.astype(o_ref.dtype)


def _hgrn_pass(hgrn_in, lb_row, backward, o_b=None, gnorm_g=None, tb=512):
    T = hgrn_in.shape[0]
    nblk = T // tb
    final = o_b is not None
    E, M = _hgrn_consts(HGRN_CHUNK, backward)
    e_bf = jnp.asarray(np.concatenate([E, E, E], axis=1), BF16)
    m_f = jnp.asarray(M, F32)
    blk = (lambda i: nblk - 1 - i) if backward else (lambda i: i)
    f_part = 2 if backward else 1
    in_specs = [
        pl.BlockSpec((tb, 512), lambda i: (blk(i), 0)),
        pl.BlockSpec((tb, 512), lambda i: (blk(i), f_part)),
        pl.BlockSpec((tb, 512), lambda i: (blk(i), 3)),
        pl.BlockSpec((1, 512), lambda i: (0, 0)),
        pl.BlockSpec(e_bf.shape, lambda i: (0, 0)),
        pl.BlockSpec(m_f.shape, lambda i: (0, 0, 0)),
    ]
    args = [hgrn_in, hgrn_in, hgrn_in, lb_row, e_bf, m_f]
    if final:
        in_specs += [
            pl.BlockSpec((tb, 512), lambda i: (blk(i), 0)),
            pl.BlockSpec((tb, 512), lambda i: (blk(i), 4)),
            pl.BlockSpec((1, HGRN_VDIM), lambda i: (0, 0)),
        ]
        args += [o_b, hgrn_in, gnorm_g]
    return pl.pallas_call(
        functools.partial(_hgrn_kernel, backward=backward, nchunk=tb // HGRN_CHUNK, final=final),
        grid=(nblk,),
        in_specs=in_specs,
        out_specs=pl.BlockSpec((tb, 512), lambda i: (blk(i), 0)),
        out_shape=jax.ShapeDtypeStruct((T, HGRN_WIDTH), BF16 if final else F32),
        scratch_shapes=[pltpu.VMEM((HGRN_HEADS, HGRN_VDIM, HGRN_EXPAND), F32)],
        compiler_params=_cparams(("arbitrary",)),
        name="hgrn_bwd" if backward else "hgrn_fwd",
    )(*args)


def _outproj_kernel(att_ref, rec_ref, x_ref, wo_ref, g_ref, wr_ref, rb_ref,
                    hres_ref, xt_ref, comb_ref):
    mix = (jnp.dot(att_ref[...], wo_ref[0:ATT_WIDTH, :], preferred_element_type=F32)
           + jnp.dot(rec_ref[...], wo_ref[ATT_WIDTH:, :], preferred_element_type=F32))
    hres = x_ref[...] + mix
    hres_ref[...] = hres
    ms = jnp.mean(hres * hres, axis=-1, keepdims=True)
    xt = hres * lax.rsqrt(ms + NORM_EPS) * g_ref[...]
    xt_ref[...] = xt
    xt_hi = xt.astype(BF16)
    xt_lo = (xt - xt_hi.astype(F32)).astype(BF16)
    both = jnp.dot(xt_hi, wr_ref[...], preferred_element_type=F32)
    logits = (both[:, 0:ROUTER_LANES] + both[:, ROUTER_LANES:]
              + jnp.dot(xt_lo, wr_ref[:, 0:ROUTER_LANES], preferred_element_type=F32)) + rb_ref[...]
    lane = lax.broadcasted_iota(jnp.int32, logits.shape, 1)
    neg = -jnp.inf
    big = jnp.int32(1 << 20)
    gmask = (lane >= N_EXPERTS) & (lane < N_EXPERTS + N_GROUPS)
    glog = jnp.where(gmask, logits, neg)
    gmax = jnp.max(glog, axis=-1, keepdims=True)
    gsum = jnp.sum(jnp.exp(glog - gmax), axis=-1, keepdims=True)
    p_g = 1.0 / gsum
    g_idx = jnp.min(jnp.where(glog == gmax, lane, big), axis=-1, keepdims=True) - N_EXPERTS
    emask = (lane < N_EXPERTS) & ((lane >> 3) == g_idx)
    elog = jnp.where(emask, logits, neg)
    e1 = jnp.max(elog, axis=-1, keepdims=True)
    i1 = jnp.min(jnp.where(elog == e1, lane, big), axis=-1, keepdims=True)
    elog2 = jnp.where(lane == i1, neg, elog)
    e2 = jnp.max(elog2, axis=-1, keepdims=True)
    i2 = jnp.min(jnp.where(elog2 == e2, lane, big), axis=-1, keepdims=True)
    r = jnp.exp(e2 - e1)
    w1 = p_g / (1.0 + r)
    w2 = p_g * r / (1.0 + r)
    comb_ref[...] = jnp.where(lane == 0, i1.astype(F32), jnp.where(
        lane == 1, i2.astype(F32), jnp.where(lane == 2, w1, jnp.where(lane == 3, w2, 0.0))))


def _out_proj(att, rec, x2, w_out_bf, g_ffn, wr, rbias, tm=512):
    T = x2.shape[0]
    row = lambda i: (i, 0)
    fixed = lambda i: (0, 0)
    return pl.pallas_call(
        _outproj_kernel,
        grid=(T // tm,),
        in_specs=[
            pl.BlockSpec((tm, ATT_WIDTH), row),
            pl.BlockSpec((tm, HGRN_WIDTH), row),
            pl.BlockSpec((tm, D_MODEL), row),
            pl.BlockSpec((D_MODEL, D_MODEL), fixed),
            pl.BlockSpec((1, D_MODEL), fixed),
            pl.BlockSpec((D_MODEL, 2 * ROUTER_LANES), fixed),
            pl.BlockSpec((1, ROUTER_LANES), fixed),
        ],
        out_specs=[
            pl.BlockSpec((tm, D_MODEL), row),
            pl.BlockSpec((tm, D_MODEL), row),
            pl.BlockSpec((tm, ROUTER_LANES), row),
        ],
        out_shape=[
            jax.ShapeDtypeStruct((T, D_MODEL), F32),
            jax.ShapeDtypeStruct((T, D_MODEL), F32),
            jax.ShapeDtypeStruct((T, ROUTER_LANES), F32),
        ],
        compiler_params=_cparams(("parallel",)),
        name="out_proj_router",
    )(att, rec, x2, w_out_bf, g_ffn, wr, rbias)


def _split3_f32(w):
    hi = w.astype(BF16).astype(F32)
    r1 = w - hi
    mid = r1.astype(BF16).astype(F32)
    lo = (r1 - mid).astype(BF16).astype(F32)
    return hi, mid, lo


def _route_sort_kernel(rt_ref, ltri_ref, utri_ref, lists_ref, meta_ref):
    rt = rt_ref[...]
    bt = rt.shape[0]
    lane = lax.broadcasted_iota(jnp.int32, rt.shape, 1)
    lane_f = lane.astype(F32)
    oh1 = lane_f == rt[:, 0:1]
    oh2 = lane_f == rt[:, 1:2]
    oh = jnp.where(oh1, 1.0, jnp.where(oh2, 1.0, 0.0))
    oh_b = oh.astype(BF16)
    rank = jnp.dot(ltri_ref[...], oh_b, preferred_element_type=F32)
    below = jnp.dot(oh_b, utri_ref[...], preferred_element_type=F32)
    offs = jnp.sum(below, axis=0, keepdims=True)
    cnt = jnp.sum(oh, axis=0, keepdims=True)
    posmat = rank + offs
    pos1 = jnp.sum(jnp.where(oh1, posmat, 0.0), axis=1, keepdims=True)
    pos2 = jnp.sum(jnp.where(oh2, posmat, 0.0), axis=1, keepdims=True)
    tok = lax.broadcasted_iota(jnp.int32, rt.shape, 0)
    tok_hi = (tok >> 5).astype(F32)
    tok_lo = (tok & 31).astype(F32)

    def record(w, slot):
        hi, mid, lo = _split3_f32(w)
        d = jnp.where(lane == 0, tok_hi, jnp.where(lane == 1, tok_lo, jnp.where(
            lane == 2, hi, jnp.where(lane == 3, mid, jnp.where(lane == 4, lo, jnp.where(
                lane == 5, slot, 0.0))))))
        return d.T.astype(BF16)

    d1 = record(rt[:, 2:3], 0.0)
    d2 = record(rt[:, 3:4], 1.0)
    cw = 512
    for c in range(2 * bt // cw):
        colp = (lax.broadcasted_iota(jnp.int32, (bt, cw), 1) + c * cw).astype(F32)
        o1 = jnp.where(colp == pos1, 1.0, 0.0).astype(BF16)
        o2 = jnp.where(colp == pos2, 1.0, 0.0).astype(BF16)
        srt = (jnp.dot(d1, o1, preferred_element_type=F32)
               + jnp.dot(d2, o2, preferred_element_type=F32))
        lists_ref[0, :, c * cw:(c + 1) * cw] = srt[0:8]
    row = lax.broadcasted_iota(jnp.int32, (8, LANES), 0)
    meta_ref[0] = jnp.where(row == 0, cnt, jnp.where(row == 1, offs, 0.0))


def _route_sort(route, bt):
    T = route.shape[0]
    nb = T // bt
    ltri = jnp.asarray(np.tril(np.ones((bt, bt), np.float32), -1), BF16)
    utri = jnp.asarray(np.triu(np.ones((LANES, LANES), np.float32), 1), BF16)
    return pl.pallas_call(
        _route_sort_kernel,
        grid=(nb,),
        in_specs=[
            pl.BlockSpec((bt, ROUTER_LANES), lambda b: (b, 0)),
            pl.BlockSpec((bt, bt), lambda b: (0, 0)),
            pl.BlockSpec((LANES, LANES), lambda b: (0, 0)),
        ],
        out_specs=[
            pl.BlockSpec((1, 8, 2 * bt), lambda b: (b, 0, 0)),
            pl.BlockSpec((1, 8, LANES), lambda b: (b, 0, 0)),
        ],
        out_shape=[
            jax.ShapeDtypeStruct((nb, 8, 2 * bt), F32),
            jax.ShapeDtypeStruct((nb, 8, LANES), F32),
        ],
        compiler_params=_cparams(("parallel",)),
        name="route_sort",
    )(route, ltri, utri)


def _moe_kernel(cnt_ref, off_ref, idx_ref, ws_ref, xt_ref, hres_ref, wg_ref, wu_ref, wd_ref, gf_ref,
                o_ref, yb_ref, xs_ref, y_ref):
    b = pl.program_id(0)
    e = pl.program_id(1)
    bt = xt_ref.shape[0]
    rows = xs_ref.shape[0] * xs_ref.shape[1]

    @pl.when(e == 0)
    def _():
        xs_ref[...] = jnp.zeros_like(xs_ref)

    n = cnt_ref[b * N_EXPERTS + e]
    base = b * 2 * bt + off_ref[b * N_EXPERTS + e]

    def tile(r, carry):
        r0 = base + r * rows
        m = jnp.minimum(n - r * rows, rows)
        trips = (m + MOE_UNROLL - 1) // MOE_UNROLL

        def gather(i, c):
            for k in range(MOE_UNROLL):
                t = idx_ref[r0 + i * MOE_UNROLL + k] & (bt - 1)
                xs_ref[i, pl.ds(k, 1), :] = xt_ref[pl.ds(t, 1), :]
            return c

        lax.fori_loop(0, trips, gather, 0)
        xs = xs_ref[...].reshape(rows, D_MODEL).astype(BF16)
        a = jnp.dot(xs, wg_ref[0], preferred_element_type=F32)
        u = jnp.dot(xs, wu_ref[0], preferred_element_type=F32)
        hid = (a * jax.nn.sigmoid(a)) * u
        y_ref[...] = jnp.dot(hid.astype(BF16), wd_ref[0],
                             preferred_element_type=F32).reshape(y_ref.shape)

        def scatter(i, c):
            for k in range(MOE_UNROLL):
                j = i * MOE_UNROLL + k
                dst = jnp.where(j < m, idx_ref[r0 + j], 2 * bt + k)
                yb_ref[pl.ds(dst, 1), :] = ws_ref[r0 + j] * y_ref[i, pl.ds(k, 1), :]
            return c

        lax.fori_loop(0, trips, scatter, 0)
        return carry

    lax.fori_loop(0, (n + rows - 1) // rows, tile, 0)

    @pl.when(e == pl.num_programs(1) - 1)
    def _():
        y = hres_ref[...] + yb_ref[0:bt, :] + yb_ref[bt:2 * bt, :]
        ms = jnp.mean(y * y, axis=-1, keepdims=True)
        o_ref[...] = y * lax.rsqrt(ms + NORM_EPS) * gf_ref[...]


def _moe(cnt, off, idx, ws, xt, hres, wg, wu, wd, g_final, bt):
    T = xt.shape[0]
    row = lambda i, e, *_: (i, 0)
    grid_spec = pltpu.PrefetchScalarGridSpec(
        num_scalar_prefetch=4,
        grid=(T // bt, N_EXPERTS),
        in_specs=[
            pl.BlockSpec((bt, D_MODEL), row, pipeline_mode=pl.Buffered(1)),
            pl.BlockSpec((bt, D_MODEL), row, pipeline_mode=pl.Buffered(1)),
            pl.BlockSpec((1, D_MODEL, D_EXPERT), lambda i, e, *_: (e, 0, 0)),
            pl.BlockSpec((1, D_MODEL, D_EXPERT), lambda i, e, *_: (e, 0, 0)),
            pl.BlockSpec((1, D_EXPERT, D_MODEL), lambda i, e, *_: (e, 0, 0)),
            pl.BlockSpec((1, D_MODEL), lambda i, e, *_: (0, 0)),
        ],
        out_specs=pl.BlockSpec((bt, D_MODEL), row, pipeline_mode=pl.Buffered(1)),
        scratch_shapes=[
            pltpu.VMEM((2 * bt + MOE_UNROLL, D_MODEL), F32),
            pltpu.VMEM((MOE_ROWS // MOE_UNROLL, MOE_UNROLL, D_MODEL), F32),
            pltpu.VMEM((MOE_ROWS // MOE_UNROLL, MOE_UNROLL, D_MODEL), F32),
        ],
    )
    return pl.pallas_call(
        _moe_kernel,
        grid_spec=grid_spec,
        out_shape=jax.ShapeDtypeStruct((T, D_MODEL), F32),
        compiler_params=_cparams(("parallel", "arbitrary")),
        name="moe",
    )(cnt, off, idx, ws, xt, hres, wg, wu, wd, g_final)


def _rope_tables(T):
    inv = ROPE_THETA ** (-np.arange(0, ATT_QKDIM, 2, dtype=np.float64) / ATT_QKDIM)
    na = -(-T // ROPE_SPLIT)
    ang_a = (np.arange(na, dtype=np.float64) * ROPE_SPLIT)[:, None] * inv[None, :]
    ang_b = np.arange(ROPE_SPLIT, dtype=np.float64)[:, None] * inv[None, :]
    ca, sa, cb, sb = (np.cos(ang_a), np.sin(ang_a), np.cos(ang_b), np.sin(ang_b))

    def combine(f, sign):
        a1, a2, b1, b2 = (jnp.asarray(v, F32) for v in f)
        return a1 * b1 + sign * (a2 * b2)

    rep = lambda v: np.tile(v, (1, LANES // v.shape[1]))
    sgn = np.tile(np.repeat([-1.0, 1.0], ATT_QKDIM // 2), LANES // ATT_QKDIM)
    tok = lambda v: v.reshape(-1, LANES)[:T]
    cos_t = tok(combine((rep(ca)[:, None], rep(sa)[:, None], rep(cb)[None], rep(sb)[None]), -1.0))
    sin_t = tok(combine((rep(sa)[:, None] * sgn, rep(ca)[:, None] * sgn, rep(cb)[None], rep(sb)[None]),
                        1.0))
    ft = lambda v: v.reshape(v.shape[0], -1)[:, :T]
    cos_tt = ft(combine((ca.T[:, :, None], sa.T[:, :, None], cb.T[:, None], sb.T[:, None]), -1.0))
    sin_tt = ft(combine((sa.T[:, :, None], ca.T[:, :, None], cb.T[:, None], sb.T[:, None]), 1.0))
    return cos_t, sin_t, cos_tt, sin_tt


def kernel(x, w_in, w_out, g_mix, lam_params, subln_g, hgrn_gnorm_g, hgrn_lb, g_ffn, w_gr, b_gr,
           w_er, b_er, w_gate, w_up, w_down, g_final):
    B, T, D = x.shape
    x2 = x.reshape(B * T, D)
    l = 0
    w_in_bf = w_in[l].astype(BF16)
    w_out_bf = w_out[l].astype(BF16)
    lp = lam_params[l].astype(F32)
    lam = (jnp.exp(jnp.sum(lp[0] * lp[1])) - jnp.exp(jnp.sum(lp[2] * lp[3])) + LAMBDA_INIT).reshape(1, 1)
    lb = jnp.cumsum(jax.nn.softmax(hgrn_lb.astype(F32), axis=1), axis=1)[:, l]
    w_qv_t = jnp.concatenate([w_in[l][:, 0:ATT_WIDTH], w_in[l][:, 2 * ATT_WIDTH:3 * ATT_WIDTH]],
                             axis=1).T.astype(BF16)
    cos_t, sin_t, cos_tt, sin_tt = _rope_tables(T)
    w_r = jnp.concatenate([jnp.transpose(w_er[l], (1, 0, 2)).reshape(D, N_EXPERTS), w_gr[l],
                           jnp.zeros((D, ROUTER_LANES - N_EXPERTS - N_GROUPS), F32)], axis=1)
    wr_hi = w_r.astype(BF16)
    wr_lo = (w_r - wr_hi.astype(F32)).astype(BF16)
    rbias = jnp.concatenate([b_er[l].reshape(-1), b_gr[l],
                             jnp.zeros((ROUTER_LANES - N_EXPERTS - N_GROUPS,), F32)]).reshape(1, -1)
    wg = w_gate[l].reshape(N_EXPERTS, D, D_EXPERT).astype(BF16)
    wu = w_up[l].reshape(N_EXPERTS, D, D_EXPERT).astype(BF16)
    wd = w_down[l].reshape(N_EXPERTS, D_EXPERT, D).astype(BF16)

    tk = min(ATT_TK, (B * T) // 2)
    qt, k, vt, hgrn_in = _in_proj(x2, g_mix[l].reshape(1, D), w_in_bf, w_qv_t, cos_t, sin_t,
                                  cos_tt, sin_tt, tk)
    att = _diff_attn(lam, qt, k, vt, subln_g[l].reshape(-1, 1))
    o_b = _hgrn_pass(hgrn_in, lb[1:2], backward=True)
    rec = _hgrn_pass(hgrn_in, lb[0:1], backward=False, o_b=o_b,
                     gnorm_g=hgrn_gnorm_g[l].reshape(1, -1))
    hres, xt, route = _out_proj(att, rec, x2, w_out_bf, g_ffn[l].reshape(1, D),
                                jnp.concatenate([wr_hi, wr_lo], axis=1), rbias)
    bt = min(MOE_BT, B * T)
    lists, meta = _route_sort(route, bt)
    pad = jnp.zeros((MOE_UNROLL,), F32)
    idx = jnp.concatenate([(lists[:, 5] * bt + lists[:, 0] * 32.0 + lists[:, 1]).reshape(-1),
                           pad]).astype(jnp.int32)
    ws = jnp.concatenate([(lists[:, 2] + lists[:, 3] + lists[:, 4]).reshape(-1), pad])
    cnt = meta[:, 0, :N_EXPERTS].astype(jnp.int32).reshape(-1)
    off = meta[:, 1, :N_EXPERTS].astype(jnp.int32).reshape(-1)
    out = _moe(cnt, off, idx, ws, xt, hres, wg, wu, wd, g_final.reshape(1, D), bt)
    return out.reshape(B, T, D)
```
